```python
import math
import jax, jax.numpy as jnp
from jax import lax
import numpy as np

D_MODEL = 1024
BATCH = 16
SEQ = 4096
DEPTH = 1

CHUNK = 64
Q_BLOCK = 128
MLA_HEADS = 8
Q_LORA = 384
KV_LORA = 256
NOPE_DIM = 64
ROPE_DIM = 32
V_DIM = 64
ROPE_THETA = 10000.0
POOL_WIDTH = 512
POOL_WINDOWS = (2, 4, 8, 16)
N_POOL_GROUPS = len(POOL_WINDOWS)
POOL_GROUP_DIM = POOL_WIDTH // N_POOL_GROUPS
MEM_LEN = 256
MEM_HEADS = 4
MEM_HEAD_DIM = D_MODEL // MEM_HEADS
N_GROUPS = 4
EXPERTS_PER_GROUP = 8
N_EXPERTS = N_GROUPS * EXPERTS_PER_GROUP
EXPERT_FF = 256
TOP_K = 2
EPS = 1e-6
IN_SPLITS = tuple(int(v) for v in np.cumsum([Q_LORA, KV_LORA, ROPE_DIM, POOL_WIDTH, D_MODEL]))
IN_COLS = Q_LORA + KV_LORA + ROPE_DIM + POOL_WIDTH + 2 * D_MODEL

kernel_name = "hybrid_mla_pool_memxattn_hmoe"


def rms_norm(x, g):
    xf = x.astype(jnp.float32)
    y = xf * lax.rsqrt(jnp.mean(xf * xf, axis=-1, keepdims=True) + EPS)
    return (y * g.astype(jnp.float32)).astype(x.dtype)


def rope_tables(positions):
    inv_freq = 1.0 / (ROPE_THETA ** (jnp.arange(0, ROPE_DIM, 2, dtype=jnp.float32) / ROPE_DIM))
    ang = positions.astype(jnp.float32)[..., None] * inv_freq
    return jnp.cos(ang), jnp.sin(ang)


def apply_rope(x, cos, sin):
    x1, x2 = jnp.split(x.astype(jnp.float32), 2, axis=-1)
    return jnp.concatenate([x1 * cos - x2 * sin, x2 * cos + x1 * sin], axis=-1).astype(x.dtype)


def mla_attention(q_nope, q_rope, k_nope, k_rope, v):
    S = q_nope.shape[1]
    scale = 1.0 / math.sqrt(NOPE_DIM + ROPE_DIM)
    outs = []
    for blk in range(S // Q_BLOCK):
        s0, s1 = blk * Q_BLOCK, (blk + 1) * Q_BLOCK
        sc = (jnp.einsum('bqhd,bkhd->bhqk', q_nope[:, s0:s1], k_nope[:, :s1])
              + jnp.einsum('bqhd,bkd->bhqk', q_rope[:, s0:s1], k_rope[:, :s1]))
        sc = sc.astype(jnp.float32) * scale
        q_chunk = (s0 + jnp.arange(Q_BLOCK)) // CHUNK
        k_chunk = jnp.arange(s1) // CHUNK
        mask = k_chunk[None, :] <= q_chunk[:, None]
        p = jax.nn.softmax(jnp.where(mask, sc, -jnp.inf), axis=-1).astype(v.dtype)
        outs.append(jnp.einsum('bhqk,bkhd->bqhd', p, v[:, :s1]))
    return jnp.concatenate(outs, axis=1)


def multiscale_pool(u, group_w, scale):
    B, S, C = u.shape
    uf = u.astype(jnp.float32)
    cs = jnp.concatenate([jnp.zeros((B, 1, C), jnp.float32), jnp.cumsum(uf, axis=1)], axis=1)
    t = jnp.arange(S)
    outs = []
    for g, w in enumerate(POOL_WINDOWS):
        c0, c1 = g * POOL_GROUP_DIM, (g + 1) * POOL_GROUP_DIM
        csg = cs[:, :, c0:c1]
        lower = jnp.concatenate([jnp.zeros((B, w - 1, POOL_GROUP_DIM), jnp.float32), csg[:, :S - w + 1]], axis=1)
        count = jnp.minimum(t + 1, w).astype(jnp.float32)[None, :, None]
        outs.append((csg[:, 1:] - lower) / count - uf[:, :, c0:c1])
    d = jnp.stack(outs, axis=2).astype(u.dtype)
    y = jnp.einsum('bsgc,gce->bsge', d, group_w).reshape(B, S, C)
    return y * scale


def memory_cross_attention(hn, memn, w_q, w_kv, w_o):
    B, S, _ = hn.shape
    q = (hn @ w_q).reshape(B, S, MEM_HEADS, MEM_HEAD_DIM)
    k, v = jnp.split(memn @ w_kv, 2, axis=-1)
    k = k.reshape(B, MEM_LEN, MEM_HEADS, MEM_HEAD_DIM)
    v = v.reshape(B, MEM_LEN, MEM_HEADS, MEM_HEAD_DIM)
    sc = jnp.einsum('bqhd,bmhd->bhqm', q, k).astype(jnp.float32) / math.sqrt(MEM_HEAD_DIM)
    p = jax.nn.softmax(sc, axis=-1).astype(v.dtype)
    o = jnp.einsum('bhqm,bmhd->bqhd', p, v).reshape(B, S, D_MODEL)
    return o @ w_o


def hierarchical_moe(xn, w_rg, b_rg, w_re, b_re, w_gate, w_up, w_down):
    B, S, _ = xn.shape
    group_prob = jax.nn.softmax((xn @ w_rg).astype(jnp.float32) + b_rg.astype(jnp.float32), axis=-1)
    g_w, g_idx = lax.top_k(group_prob, 1)
    e_logits = ((xn @ w_re).astype(jnp.float32) + b_re.astype(jnp.float32)).reshape(B, S, N_GROUPS, EXPERTS_PER_GROUP)
    sel = jnp.einsum('bsge,bsg->bse', e_logits, jax.nn.one_hot(g_idx[..., 0], N_GROUPS, dtype=jnp.float32))
    e_w, e_idx = lax.top_k(jax.nn.softmax(sel, axis=-1), TOP_K)
    e_w = e_w / jnp.sum(e_w, axis=-1, keepdims=True)
    weights = g_w * e_w
    global_idx = g_idx * EXPERTS_PER_GROUP + e_idx
    cw = jnp.einsum('bsk,bskn->bsn', weights, jax.nn.one_hot(global_idx, N_EXPERTS, dtype=jnp.float32)).astype(xn.dtype)
    out = jnp.zeros_like(xn)
    for e in range(N_EXPERTS):
        hid = jax.nn.silu(xn @ w_gate[e]) * (xn @ w_up[e])
        out = out + (hid * cw[..., e:e + 1]) @ w_down[e]
    return out


def setup_inputs(seed: int = 0) -> dict:
    key = jax.random.key(seed)
    ks = iter(jax.random.split(key, 40))
    f32 = jnp.float32

    def w(shape, fan_in):
        return jax.random.normal(next(ks), shape, f32) * (fan_in ** -0.5)

    def gain(shape):
        return 1.0 + 0.02 * jax.random.normal(next(ks), shape, f32)

    def bias(shape):
        return 0.01 * jax.random.normal(next(ks), shape, f32)

    L = DEPTH
    x = jax.random.normal(next(ks), (BATCH, SEQ, D_MODEL), f32)
    mem = jax.random.normal(next(ks), (BATCH, MEM_LEN, D_MODEL), f32)
    offsets = jax.random.randint(next(ks), (BATCH, 1), 0, 4096, dtype=jnp.int32)
    positions = (offsets + jnp.arange(SEQ, dtype=jnp.int32)[None, :]).astype(jnp.int32)
    return {
        "x": x,
        "mem": mem,
        "positions": positions,
        "mix_norm_g": gain((L, D_MODEL)),
        "w_in": w((L, D_MODEL, IN_COLS), D_MODEL),
        "q_norm_g": gain((L, Q_LORA)),
        "w_q_up": w((L, Q_LORA, MLA_HEADS * (NOPE_DIM + ROPE_DIM)), Q_LORA),
        "kv_norm_g": gain((L, KV_LORA)),
        "w_kv_up": w((L, KV_LORA, MLA_HEADS * (NOPE_DIM + V_DIM)), KV_LORA),
        "w_attn_branch": w((L, MLA_HEADS * V_DIM, D_MODEL), MLA_HEADS * V_DIM),
        "pool_w": w((L, N_POOL_GROUPS, POOL_GROUP_DIM, POOL_GROUP_DIM), POOL_GROUP_DIM),
        "pool_scale": 1.0 + 0.1 * jax.random.normal(next(ks), (L, POOL_WIDTH), f32),
        "w_pool_branch": w((L, POOL_WIDTH, D_MODEL), POOL_WIDTH),
        "w_mix_out": w((L, D_MODEL, D_MODEL), D_MODEL),
        "xattn_norm_g": gain((L, D_MODEL)),
        "mem_norm_g": gain((L, D_MODEL)),
        "w_xq": w((L, D_MODEL, D_MODEL), D_MODEL),
        "w_xkv": w((L, D_MODEL, 2 * D_MODEL), D_MODEL),
        "w_xo": w((L, D_MODEL, D_MODEL), D_MODEL),
        "ffn_norm_g": gain((L, D_MODEL)),
        "w_router_group": w((L, D_MODEL, N_GROUPS), D_MODEL),
        "b_router_group": bias((L, N_GROUPS)),
        "w_router_expert": w((L, D_MODEL, N_EXPERTS), D_MODEL),
        "b_router_expert": bias((L, N_EXPERTS)),
        "w_exp_gate": w((L, N_EXPERTS, D_MODEL, EXPERT_FF), D_MODEL),
        "w_exp_up": w((L, N_EXPERTS, D_MODEL, EXPERT_FF), D_MODEL),
        "w_exp_down": w((L, N_EXPERTS, EXPERT_FF, D_MODEL), EXPERT_FF),
        "final_norm_g": gain((D_MODEL,)),
    }


def reference(x, mem, positions, mix_norm_g, w_in, q_norm_g, w_q_up, kv_norm_g, w_kv_up,
              w_attn_branch, pool_w, pool_scale, w_pool_branch, w_mix_out,
              xattn_norm_g, mem_norm_g, w_xq, w_xkv, w_xo,
              ffn_norm_g, w_router_group, b_router_group, w_router_expert, b_router_expert,
              w_exp_gate, w_exp_up, w_exp_down, final_norm_g):
    B, S, _ = x.shape
    cos, sin = rope_tables(positions)
    h = x
    for l in range(DEPTH):
        hn = rms_norm(h, mix_norm_g[l])
        z = hn @ w_in[l]
        q_lat, kv_lat, k_rope, u_pool, gate_a, gate_b = jnp.split(z, IN_SPLITS, axis=-1)
        q = (rms_norm(q_lat, q_norm_g[l]) @ w_q_up[l]).reshape(B, S, MLA_HEADS, NOPE_DIM + ROPE_DIM)
        q_nope, q_rope = q[..., :NOPE_DIM], q[..., NOPE_DIM:]
        q_rope = apply_rope(q_rope, cos[:, :, None, :], sin[:, :, None, :])
        kv = (rms_norm(kv_lat, kv_norm_g[l]) @ w_kv_up[l]).reshape(B, S, MLA_HEADS, NOPE_DIM + V_DIM)
        k_nope, v = kv[..., :NOPE_DIM], kv[..., NOPE_DIM:]
        k_rope = apply_rope(k_rope, cos, sin)
        attn = mla_attention(q_nope, q_rope, k_nope, k_rope, v).reshape(B, S, MLA_HEADS * V_DIM)
        y_a = attn @ w_attn_branch[l]
        y_b = multiscale_pool(u_pool, pool_w[l], pool_scale[l]) @ w_pool_branch[l]
        g_a = jax.nn.sigmoid(gate_a.astype(jnp.float32)).astype(h.dtype)
        g_b = jax.nn.sigmoid(gate_b.astype(jnp.float32)).astype(h.dtype)
        h = h + (g_a * y_a + g_b * y_b) @ w_mix_out[l]
        h = h + memory_cross_attention(rms_norm(h, xattn_norm_g[l]), rms_norm(mem, mem_norm_g[l]),
                                       w_xq[l], w_xkv[l], w_xo[l])
        h = h + hierarchical_moe(rms_norm(h, ffn_norm_g[l]), w_router_group[l], b_router_group[l],
                                 w_router_expert[l], b_router_expert[l],
                                 w_exp_gate[l], w_exp_up[l], w_exp_down[l])
    return rms_norm(h, final_norm_g)
```

```python
import functools
import math

import jax
import jax.numpy as jnp
from jax import lax
from jax.experimental import pallas as pl
from jax.experimental.pallas import tpu as pltpu

F32 = jnp.float32
BF16 = jnp.bfloat16

D_MODEL = 1024
CHUNK = 64
MLA_HEADS = 8
Q_LORA = 384
KV_LORA = 256
NOPE_DIM = 64
ROPE_DIM = 32
V_DIM = 64
ROPE_THETA = 10000.0
POOL_WIDTH = 512
POOL_WINDOWS = (2, 4, 8, 16)
POOL_GROUP_DIM = POOL_WIDTH // len(POOL_WINDOWS)
POOL_HALO = 16
MEM_HEADS = 4
MEM_HEAD_DIM = D_MODEL // MEM_HEADS
N_GROUPS = 4
EXPERTS_PER_GROUP = 8
N_EXPERTS = N_GROUPS * EXPERTS_PER_GROUP
EXPERT_FF = 256
EPS = 1e-6

LANES = 128
HEAD_PAD = LANES

_C_Q = 0
_C_KV = _C_Q + Q_LORA
_C_KR = _C_KV + KV_LORA
_C_POOL = _C_KR + LANES
_C_GA = _C_POOL + POOL_WIDTH
_C_GB = _C_GA + D_MODEL
_C_END = _C_GB + D_MODEL

VMEM_LIMIT = 56 * 1024 * 1024


def _rms(x, g):
    return x * lax.rsqrt(jnp.mean(x * x, axis=-1, keepdims=True) + EPS) * g


def _dot(a, b):
    return jnp.dot(a, b, preferred_element_type=F32)


def _dot_nt(a, b):
    return lax.dot_general(a, b, (((1,), (1,)), ((), ())), preferred_element_type=F32)


def _mem_kv_kernel(mem_ref, g_ref, w_ref, kv_ref):
    mn = _rms(mem_ref[...], g_ref[...]).astype(BF16)
    kv_ref[...] = _dot(mn, w_ref[...]).astype(BF16)


def _mem_kv(mem2d, g, w_xkv):
    rows = mem2d.shape[0]
    tm = 512
    return pl.pallas_call(
        _mem_kv_kernel,
        grid=(rows // tm,),
        in_specs=[
            pl.BlockSpec((tm, D_MODEL), lambda i: (i, 0)),
            pl.BlockSpec((1, D_MODEL), lambda i: (0, 0)),
            pl.BlockSpec((D_MODEL, 2 * D_MODEL), lambda i: (0, 0)),
        ],
        out_specs=pl.BlockSpec((tm, 2 * D_MODEL), lambda i: (i, 0)),
        out_shape=jax.ShapeDtypeStruct((rows, 2 * D_MODEL), BF16),
        compiler_params=pltpu.CompilerParams(vmem_limit_bytes=VMEM_LIMIT),
        name="mem_kv",
    )(mem2d, g, w_xkv)


def _rope(t, c, sa, sb):
    w = t.shape[-1]
    return t * c + pltpu.roll(t, ROPE_DIM // 2, 1) * sa + pltpu.roll(t, w - ROPE_DIM // 2, 1) * sb


def _in_proj_kernel(x_ref, c_ref, sa_ref, sb_ref, g_ref, win_ref, qg_ref, wq_ref, kvg_ref, wkv_ref,
                    poolw_ref, pscale_ref, wpb_ref,
                    q_out, k_out, v_out, ga_out, gyb_out, hist_ref):
    tm = x_ref.shape[0]
    i = pl.program_id(1)
    hn = _rms(x_ref[...], g_ref[...]).astype(BF16)

    c1 = c_ref[...]
    sa1 = sa_ref[...]
    sb1 = sb_ref[...]
    c8 = jnp.tile(c1, (1, MLA_HEADS))
    sa8 = jnp.tile(sa1, (1, MLA_HEADS))
    sb8 = jnp.tile(sb1, (1, MLA_HEADS))

    q_lat = _dot(hn, win_ref[:, _C_Q:_C_KV])
    qn = _rms(q_lat, qg_ref[...]).astype(BF16)
    q = _dot(qn, wq_ref[...])
    q_out[...] = _rope(q, c8, sa8, sb8).astype(BF16)

    kv_lat = _dot(hn, win_ref[:, _C_KV:_C_KR])
    kvn = _rms(kv_lat, kvg_ref[...]).astype(BF16)
    k_nope = _dot(kvn, wkv_ref[:, 0:MLA_HEADS * HEAD_PAD])
    kr = _rope(_dot(hn, win_ref[:, _C_KR:_C_POOL]), c1, sa1, sb1)
    k_out[...] = (k_nope + jnp.tile(kr, (1, MLA_HEADS))).astype(BF16)
    v_out[...] = _dot(kvn, wkv_ref[:, MLA_HEADS * HEAD_PAD:]).astype(BF16)

    u = _dot(hn, win_ref[:, _C_POOL:_C_GA])

    @pl.when(i == 0)
    def _():
        hist_ref[...] = jnp.zeros_like(hist_ref)

    ext = jnp.concatenate([hist_ref[...], u], axis=0)
    hist_ref[...] = u[tm - POOL_HALO:, :]
    t_idx = i * tm + lax.broadcasted_iota(jnp.int32, (tm, POOL_GROUP_DIM), 0)
    ys = []
    for g, w in enumerate(POOL_WINDOWS):
        c0 = g * POOL_GROUP_DIM
        run = ext[:, c0:c0 + POOL_GROUP_DIM]
        span = 1
        while span < w:
            run = run + pltpu.roll(run, span, 0)
            span *= 2
        cnt = jnp.minimum(t_idx + 1, w).astype(F32)
        d = run[POOL_HALO:, :] / cnt - u[:, c0:c0 + POOL_GROUP_DIM]
        ys.append(_dot(d.astype(BF16), poolw_ref[g]))
    y = (jnp.concatenate(ys, axis=1) * pscale_ref[...]).astype(BF16)
    y_b = _dot(y, wpb_ref[...])

    g_a = jax.nn.sigmoid(_dot(hn, win_ref[:, _C_GA:_C_GB]))
    ga_out[...] = g_a.astype(BF16)
    g_b = jax.nn.sigmoid(_dot(hn, win_ref[:, _C_GB:_C_END]))
    gyb_out[...] = (g_b * y_b).astype(BF16)


def _in_proj(x, ctab, satab, sbtab, g, win, qg, wq, kvg, wkv, poolw, pscale, wpb, tm):
    B, S, _ = x.shape
    row = lambda b, i: (b, i, 0)
    const2 = lambda b, i: (0, 0)
    const3 = lambda b, i: (0, 0, 0)
    wide = jax.ShapeDtypeStruct((B, S, MLA_HEADS * HEAD_PAD), BF16)
    return pl.pallas_call(
        _in_proj_kernel,
        grid=(B, S // tm),
        in_specs=[
            pl.BlockSpec((None, tm, D_MODEL), row),
            pl.BlockSpec((None, tm, LANES), row),
            pl.BlockSpec((None, tm, LANES), row),
            pl.BlockSpec((None, tm, LANES), row),
            pl.BlockSpec((1, D_MODEL), const2),
            pl.BlockSpec(win.shape, const2),
            pl.BlockSpec((1, Q_LORA), const2),
            pl.BlockSpec(wq.shape, const2),
            pl.BlockSpec((1, KV_LORA), const2),
            pl.BlockSpec(wkv.shape, const2),
            pl.BlockSpec(poolw.shape, const3),
            pl.BlockSpec((1, POOL_WIDTH), const2),
            pl.BlockSpec(wpb.shape, const2),
        ],
        out_specs=[
            pl.BlockSpec((None, tm, MLA_HEADS * HEAD_PAD), row),
            pl.BlockSpec((None, tm, MLA_HEADS * HEAD_PAD), row),
            pl.BlockSpec((None, tm, MLA_HEADS * HEAD_PAD), row),
            pl.BlockSpec((None, tm, D_MODEL), row),
            pl.BlockSpec((None, tm, D_MODEL), row),
        ],
        out_shape=[wide, wide, wide,
                   jax.ShapeDtypeStruct((B, S, D_MODEL), BF16),
                   jax.ShapeDtypeStruct((B, S, D_MODEL), BF16)],
        scratch_shapes=[pltpu.VMEM((POOL_HALO, POOL_WIDTH), F32)],
        compiler_params=pltpu.CompilerParams(
            dimension_semantics=("arbitrary", "arbitrary"), vmem_limit_bytes=VMEM_LIMIT),
        name="in_proj",
    )(x, ctab, satab, sbtab, g, win, qg, wq, kvg, wkv, poolw, pscale, wpb)


def _mla_kernel(q_ref, k_ref, v_ref, o_ref, m_ref, l_ref, acc_ref):
    tq = q_ref.shape[0]
    tk = tq
    i = pl.program_id(1)
    row_c = lax.broadcasted_iota(jnp.int32, (tq, tk), 0) // CHUNK
    col_c = lax.broadcasted_iota(jnp.int32, (tq, tk), 1) // CHUNK
    diag_mask = col_c <= row_c

    outs = []
    for h in range(MLA_HEADS):
        hs = slice(h * HEAD_PAD, (h + 1) * HEAD_PAD)
        qh = q_ref[:, hs]

        s = _dot_nt(qh, k_ref[pl.ds(pl.multiple_of(i * tk, tk), tk), hs])
        s = jnp.where(diag_mask, s, -jnp.inf)
        m0 = jnp.max(s, axis=-1, keepdims=True)
        p = jnp.exp(s - m0)
        m_ref[...] = m0
        l_ref[...] = jnp.sum(p, axis=-1, keepdims=True)
        acc_ref[...] = _dot(p.astype(BF16), v_ref[pl.ds(pl.multiple_of(i * tk, tk), tk), hs])

        def body(j, carry):
            off = pl.multiple_of(j * tk, tk)
            s = _dot_nt(qh, k_ref[pl.ds(off, tk), hs])
            m_prev = m_ref[...]
            m_new = jnp.maximum(m_prev, jnp.max(s, axis=-1, keepdims=True))
            alpha = jnp.exp(m_prev - m_new)
            p = jnp.exp(s - m_new)
            l_ref[...] = alpha * l_ref[...] + jnp.sum(p, axis=-1, keepdims=True)
            acc_ref[...] = alpha * acc_ref[...] + _dot(p.astype(BF16), v_ref[pl.ds(off, tk), hs])
            m_ref[...] = m_new
            return carry

        lax.fori_loop(0, i, body, 0)
        outs.append(acc_ref[...] / l_ref[...])

    for pair in range(MLA_HEADS // 2):
        packed = outs[2 * pair] + pltpu.roll(outs[2 * pair + 1], V_DIM, 1)
        o_ref[:, pair * LANES:(pair + 1) * LANES] = packed.astype(BF16)


def _mla_attention(q, k, v, tq):
    B, S, W = q.shape
    return pl.pallas_call(
        _mla_kernel,
        grid=(B, S // tq),
        in_specs=[
            pl.BlockSpec((None, tq, W), lambda b, i: (b, i, 0)),
            pl.BlockSpec((None, S, W), lambda b, i: (b, 0, 0)),
            pl.BlockSpec((None, S, W), lambda b, i: (b, 0, 0)),
        ],
        out_specs=pl.BlockSpec((None, tq, MLA_HEADS * V_DIM), lambda b, i: (b, i, 0)),
        out_shape=jax.ShapeDtypeStruct((B, S, MLA_HEADS * V_DIM), BF16),
        scratch_shapes=[pltpu.VMEM((tq, 1), F32), pltpu.VMEM((tq, 1), F32), pltpu.VMEM((tq, HEAD_PAD), F32)],
        compiler_params=pltpu.CompilerParams(
            dimension_semantics=("arbitrary", "arbitrary"), vmem_limit_bytes=VMEM_LIMIT),
        name="mla_attn",
    )(q, k, v)


def _route(logits, bias):
    tm = logits.shape[0]
    lane = lax.broadcasted_iota(jnp.int32, (tm, LANES), 1)
    neg = -jnp.inf
    lg = jnp.where(lane < N_GROUPS, logits[:, LANES:] + bias[:, LANES:], neg)
    ge = jnp.exp(lg - jnp.max(lg, axis=-1, keepdims=True))
    gp = ge / jnp.sum(ge, axis=-1, keepdims=True)
    g_w = jnp.max(gp, axis=-1, keepdims=True)
    g_idx = jnp.min(jnp.where(gp == g_w, lane, LANES), axis=-1, keepdims=True)

    sel = (lane // EXPERTS_PER_GROUP == g_idx) & (lane < N_EXPERTS)
    le = jnp.where(sel, logits[:, :LANES] + bias[:, :LANES], neg)
    ee = jnp.exp(le - jnp.max(le, axis=-1, keepdims=True))
    ep = jnp.where(sel, ee / jnp.sum(ee, axis=-1, keepdims=True), -1.0)
    w1 = jnp.max(ep, axis=-1, keepdims=True)
    i1 = jnp.min(jnp.where(ep == w1, lane, LANES), axis=-1, keepdims=True)
    ep2 = jnp.where(lane == i1, -1.0, ep)
    w2 = jnp.max(ep2, axis=-1, keepdims=True)
    i2 = jnp.min(jnp.where(ep2 == w2, lane, LANES), axis=-1, keepdims=True)
    den = w1 + w2
    return jnp.where(lane == i1, g_w * (w1 / den), 0.0) + jnp.where(lane == i2, g_w * (w2 / den), 0.0)


def _mix_kernel(x_ref, attn_ref, ga_ref, gyb_ref, kv_ref, wab_ref, wmix_ref, xg_ref, wxq_ref, wxo_ref,
                fg_ref, wr_hi_ref, wr_lo_ref, rb_ref,
                h_out, xn_out, cw_out):
    y_a = _dot(attn_ref[...], wab_ref[...])
    merged = (ga_ref[...].astype(F32) * y_a + gyb_ref[...].astype(F32)).astype(BF16)
    h1 = x_ref[...] + _dot(merged, wmix_ref[...])

    hn = _rms(h1, xg_ref[...]).astype(BF16)
    q = _dot(hn, wxq_ref[...]).astype(BF16)
    heads = []
    for h in range(MEM_HEADS):
        hs = slice(h * MEM_HEAD_DIM, (h + 1) * MEM_HEAD_DIM)
        s = _dot_nt(q[:, hs], kv_ref[:, hs])
        p = jnp.exp(s - jnp.max(s, axis=-1, keepdims=True))
        o = _dot(p.astype(BF16), kv_ref[:, D_MODEL + h * MEM_HEAD_DIM:D_MODEL + (h + 1) * MEM_HEAD_DIM])
        heads.append((o / jnp.sum(p, axis=-1, keepdims=True)).astype(BF16))
    h2 = h1 + _dot(jnp.concatenate(heads, axis=1), wxo_ref[...])
    h_out[...] = h2

    xn = _rms(h2, fg_ref[...])
    xn_hi = xn.astype(BF16)
    xn_out[...] = xn_hi
    xn_lo = (xn - xn_hi.astype(F32)).astype(BF16)
    logits = _dot(xn_hi, wr_hi_ref[...]) + (_dot(xn_hi, wr_lo_ref[...]) + _dot(xn_lo, wr_hi_ref[...]))
    cw_out[...] = _route(logits, rb_ref[...])


def _mix_xattn(x, attn, ga, gyb, memkv, wab, wmix, xg, wxq, wxo, fg, wr_hi, wr_lo, rb, tm):
    B, S, _ = x.shape
    M = memkv.shape[1]
    row = lambda b, i: (b, i, 0)
    const2 = lambda b, i: (0, 0)
    return pl.pallas_call(
        _mix_kernel,
        grid=(B, S // tm),
        in_specs=[
            pl.BlockSpec((None, tm, D_MODEL), row),
            pl.BlockSpec((None, tm, MLA_HEADS * V_DIM), row),
            pl.BlockSpec((None, tm, D_MODEL), row),
            pl.BlockSpec((None, tm, D_MODEL), row),
            pl.BlockSpec((None, M, 2 * D_MODEL), lambda b, i: (b, 0, 0)),
            pl.BlockSpec(wab.shape, const2),
            pl.BlockSpec(wmix.shape, const2),
            pl.BlockSpec((1, D_MODEL), const2),
            pl.BlockSpec(wxq.shape, const2),
            pl.BlockSpec(wxo.shape, const2),
            pl.BlockSpec((1, D_MODEL), const2),
            pl.BlockSpec(wr_hi.shape, const2),
            pl.BlockSpec(wr_lo.shape, const2),
            pl.BlockSpec((1, 2 * LANES), const2),
        ],
        out_specs=[
            pl.BlockSpec((None, tm, D_MODEL), row),
            pl.BlockSpec((None, tm, D_MODEL), row),
            pl.BlockSpec((None, tm, LANES), row),
        ],
        out_shape=[jax.ShapeDtypeStruct((B, S, D_MODEL), F32),
                   jax.ShapeDtypeStruct((B, S, D_MODEL), BF16),
                   jax.ShapeDtypeStruct((B, S, LANES), F32)],
        compiler_params=pltpu.CompilerParams(
            dimension_semantics=("arbitrary", "arbitrary"), vmem_limit_bytes=VMEM_LIMIT),
        name="mix_xattn",
    )(x, attn, ga, gyb, memkv, wab, wmix, xg, wxq, wxo, fg, wr_hi, wr_lo, rb)


def _moe_kernel(h_ref, xn_ref, cw_ref, wgu_ref, wd_ref, fg_ref, o_ref, acc_ref):
    e = pl.program_id(1)

    @pl.when(e == 0)
    def _():
        acc_ref[...] = h_ref[...]

    xn = xn_ref[...]
    gu = _dot(xn, wgu_ref[...])
    gate = gu[:, :EXPERT_FF]
    hid = gate * jax.nn.sigmoid(gate) * gu[:, EXPERT_FF:]
    lane = lax.broadcasted_iota(jnp.int32, cw_ref.shape, 1)
    cw_e = jnp.sum(jnp.where(lane == e, cw_ref[...], 0.0), axis=-1, keepdims=True)
    acc_ref[...] += _dot((hid * cw_e).astype(BF16), wd_ref[...])

    @pl.when(e == pl.num_programs(1) - 1)
    def _():
        o_ref[...] = _rms(acc_ref[...], fg_ref[...])


def _moe(h2, xn, cw, wgu, wd, fg, tm):
    T = h2.shape[0]
    return pl.pallas_call(
        _moe_kernel,
        grid=(T // tm, N_EXPERTS),
        in_specs=[
            pl.BlockSpec((tm, D_MODEL), lambda i, e: (i, 0)),
            pl.BlockSpec((tm, D_MODEL), lambda i, e: (i, 0)),
            pl.BlockSpec((tm, LANES), lambda i, e: (i, 0)),
            pl.BlockSpec((None, D_MODEL, 2 * EXPERT_FF), lambda i, e: (e, 0, 0)),
            pl.BlockSpec((None, EXPERT_FF, D_MODEL), lambda i, e: (e, 0, 0)),
            pl.BlockSpec((1, D_MODEL), lambda i, e: (0, 0)),
        ],
        out_specs=pl.BlockSpec((tm, D_MODEL), lambda i, e: (i, 0)),
        out_shape=jax.ShapeDtypeStruct((T, D_MODEL), F32),
        scratch_shapes=[pltpu.VMEM((tm, D_MODEL), F32)],
        compiler_params=pltpu.CompilerParams(
            dimension_semantics=("arbitrary", "arbitrary"), vmem_limit_bytes=VMEM_LIMIT),
        name="moe",
    )(h2, xn, cw, wgu, wd, fg)


def _rope_tables(positions):
    inv_freq = 1.0 / (ROPE_THETA ** (jnp.arange(0, ROPE_DIM, 2, dtype=F32) / ROPE_DIM))
    ang = positions.astype(F32)[..., None] * inv_freq
    cos, sin = jnp.cos(ang), jnp.sin(ang)
    half = ROPE_DIM // 2
    z = lambda n: jnp.zeros(cos.shape[:-1] + (n,), F32)
    ctab = jnp.concatenate([jnp.ones(cos.shape[:-1] + (NOPE_DIM,), F32), cos, cos, z(LANES - NOPE_DIM - ROPE_DIM)], -1)
    satab = jnp.concatenate([z(NOPE_DIM + half), sin, z(LANES - NOPE_DIM - ROPE_DIM)], -1)
    sbtab = jnp.concatenate([z(NOPE_DIM), -sin, z(half + LANES - NOPE_DIM - ROPE_DIM)], -1)
    return ctab, satab, sbtab


def _pad_heads(w, heads, width):
    k = w.shape[0]
    w = w.reshape(k, heads, width)
    w = jnp.pad(w, ((0, 0), (0, 0), (0, HEAD_PAD - width)))
    return w.reshape(k, heads * HEAD_PAD)


def _layer(l, h, mem, tables, mix_norm_g, w_in, q_norm_g, w_q_up, kv_norm_g, w_kv_up, w_attn_branch,
           pool_w, pool_scale, w_pool_branch, w_mix_out, xattn_norm_g, mem_norm_g, w_xq, w_xkv, w_xo,
           ffn_norm_g, w_router_group, b_router_group, w_router_expert, b_router_expert,
           w_exp_gate, w_exp_up, w_exp_down, out_g, tm_proj, tq, tm_mix, tm_moe):
    B, S, _ = h.shape
    row2 = lambda v: v.reshape(1, -1).astype(F32)

    wi = w_in[l]
    kr_cols = jnp.pad(wi[:, Q_LORA + KV_LORA:Q_LORA + KV_LORA + ROPE_DIM],
                      ((0, 0), (NOPE_DIM, LANES - NOPE_DIM - ROPE_DIM)))
    win = jnp.concatenate([wi[:, :Q_LORA + KV_LORA], kr_cols, wi[:, Q_LORA + KV_LORA + ROPE_DIM:]], axis=1).astype(BF16)
    scale = 1.0 / math.sqrt(NOPE_DIM + ROPE_DIM)
    wq = _pad_heads(w_q_up[l] * scale, MLA_HEADS, NOPE_DIM + ROPE_DIM).astype(BF16)
    wkv3 = w_kv_up[l].reshape(KV_LORA, MLA_HEADS, NOPE_DIM + V_DIM)
    wkv = jnp.concatenate([
        _pad_heads(wkv3[:, :, :NOPE_DIM].reshape(KV_LORA, -1), MLA_HEADS, NOPE_DIM),
        _pad_heads(wkv3[:, :, NOPE_DIM:].reshape(KV_LORA, -1), MLA_HEADS, V_DIM)], axis=1).astype(BF16)

    memkv = _mem_kv(mem.reshape(-1, D_MODEL), row2(mem_norm_g[l]), w_xkv[l].astype(BF16))
    memkv = memkv.reshape(B, -1, 2 * D_MODEL)

    q, k, v, ga, gyb = _in_proj(
        h, *tables, row2(mix_norm_g[l]), win, row2(q_norm_g[l]), wq, row2(kv_norm_g[l]), wkv,
        pool_w[l].astype(BF16), row2(pool_scale[l]), w_pool_branch[l].astype(BF16), tm_proj)
    attn = _mla_attention(q, k, v, tq)

    w_r = jnp.zeros((D_MODEL, 2 * LANES), F32)
    w_r = w_r.at[:, :N_EXPERTS].set(w_router_expert[l]).at[:, LANES:LANES + N_GROUPS].set(w_router_group[l])
    wr_hi = w_r.astype(BF16)
    wr_lo = (w_r - wr_hi.astype(F32)).astype(BF16)
    rb = jnp.zeros((1, 2 * LANES), F32)
    rb = rb.at[0, :N_EXPERTS].set(b_router_expert[l]).at[0, LANES:LANES + N_GROUPS].set(b_router_group[l])

    h2, xn, cw = _mix_xattn(
        h, attn, ga, gyb, memkv, w_attn_branch[l].astype(BF16), w_mix_out[l].astype(BF16),
        row2(xattn_norm_g[l]), (w_xq[l] * (1.0 / math.sqrt(MEM_HEAD_DIM))).astype(BF16), w_xo[l].astype(BF16),
        row2(ffn_norm_g[l]), wr_hi, wr_lo, rb, tm_mix)

    wgu = jnp.concatenate([w_exp_gate[l], w_exp_up[l]], axis=-1).astype(BF16)
    out = _moe(h2.reshape(B * S, D_MODEL), xn.reshape(B * S, D_MODEL), cw.reshape(B * S, LANES),
               wgu, w_exp_down[l].astype(BF16), row2(out_g), tm_moe)
    return out.reshape(B, S, D_MODEL)


def kernel(x, mem, positions, mix_norm_g, w_in, q_norm_g, w_q_up, kv_norm_g, w_kv_up, w_attn_branch, pool_w, pool_scale, w_pool_branch, w_mix_out, xattn_norm_g, mem_norm_g, w_xq, w_xkv, w_xo, ffn_norm_g, w_router_group, b_router_group, w_router_expert, b_router_expert, w_exp_gate, w_exp_up, w_exp_down, final_norm_g):
    depth = w_in.shape[0]
    assert depth == 1, "the experts kernel fuses the final RMSNorm, which is only valid after the last layer"
    S = x.shape[1]
    tables = _rope_tables(positions)
    tm_proj = min(512, S)
    tq = min(256, S)
    tm_mix = min(512, S)
    tm_moe = min(1024, S)
    return _layer(0, x, mem, tables, mix_norm_g, w_in, q_norm_g, w_q_up, kv_norm_g, w_kv_up, w_attn_branch,
                  pool_w, pool_scale, w_pool_branch, w_mix_out, xattn_norm_g, mem_norm_g, w_xq, w_xkv, w_xo,
                  ffn_norm_g, w_router_group, b_router_group, w_router_expert, b_router_expert,
                  w_exp_gate, w_exp_up, w_exp_down, final_norm_g, tm_proj, tq, tm_mix, tm_moe)
```

```python
import functools
import math

import jax
import jax.numpy as jnp
from jax import lax
from jax.experimental import pallas as pl
from jax.experimental.pallas import tpu as pltpu

F32 = jnp.float32
BF16 = jnp.bfloat16

D_MODEL = 1024
CHUNK = 64
MLA_HEADS = 8
Q_LORA = 384
KV_LORA = 256
NOPE_DIM = 64
ROPE_DIM = 32
V_DIM = 64
ROPE_THETA = 10000.0
POOL_WIDTH = 512
POOL_WINDOWS = (2, 4, 8, 16)
POOL_GROUP_DIM = POOL_WIDTH // len(POOL_WINDOWS)
POOL_HALO = 16
MEM_HEADS = 4
MEM_HEAD_DIM = D_MODEL // MEM_HEADS
N_GROUPS = 4
EXPERTS_PER_GROUP = 8
N_EXPERTS = N_GROUPS * EXPERTS_PER_GROUP
EXPERT_FF = 256
EPS = 1e-6

LANES = 128
HEAD_PAD = LANES
Q_TILE = 512
KV_TILE = 256
KEY_SUB = 128
V_ROWS = V_DIM + 16
QK_AHEAD = 3
PV_BEHIND = 2

_C_Q = 0
_C_KV = _C_Q + Q_LORA
_C_KR = _C_KV + KV_LORA
_C_POOL = _C_KR + LANES
_C_GA = _C_POOL + POOL_WIDTH
_C_GB = _C_GA + D_MODEL
_C_END = _C_GB + D_MODEL

VMEM_LIMIT = 56 * 1024 * 1024


def _rms(x, g):
    return x * lax.rsqrt(jnp.mean(x * x, axis=-1, keepdims=True) + EPS) * g


def _dot(a, b):
    return jnp.dot(a, b, preferred_element_type=F32)


def _dot_nt(a, b):
    return lax.dot_general(a, b, (((1,), (1,)), ((), ())), preferred_element_type=F32)


def _mem_kv_kernel(mem_ref, g_ref, w_ref, kv_ref):
    mn = _rms(mem_ref[...], g_ref[...]).astype(BF16)
    kv_ref[...] = _dot(mn, w_ref[...]).astype(BF16)


def _mem_kv(mem2d, g, w_xkv):
    rows = mem2d.shape[0]
    tm = 512
    return pl.pallas_call(
        _mem_kv_kernel,
        grid=(rows // tm,),
        in_specs=[
            pl.BlockSpec((tm, D_MODEL), lambda i: (i, 0)),
            pl.BlockSpec((1, D_MODEL), lambda i: (0, 0)),
            pl.BlockSpec((D_MODEL, 2 * D_MODEL), lambda i: (0, 0)),
        ],
        out_specs=pl.BlockSpec((tm, 2 * D_MODEL), lambda i: (i, 0)),
        out_shape=jax.ShapeDtypeStruct((rows, 2 * D_MODEL), BF16),
        compiler_params=pltpu.CompilerParams(vmem_limit_bytes=VMEM_LIMIT),
        name="mem_kv",
    )(mem2d, g, w_xkv)


def _rope(t, c, sa, sb):
    w = t.shape[-1]
    return t * c + pltpu.roll(t, ROPE_DIM // 2, 1) * sa + pltpu.roll(t, w - ROPE_DIM // 2, 1) * sb


def _in_proj_kernel(x_ref, c_ref, sa_ref, sb_ref, g_ref, win_ref, qg_ref, wq_ref, kvg_ref, wkv_ref,
                    poolw_ref, pscale_ref, wpb_ref,
                    qT_out, k_out, vT_out, ga_out, gyb_out, hist_ref):
    tm = x_ref.shape[0]
    i = pl.program_id(1)
    hn = _rms(x_ref[...], g_ref[...]).astype(BF16)

    c1 = c_ref[...]
    sa1 = sa_ref[...]
    sb1 = sb_ref[...]
    c8 = jnp.tile(c1, (1, MLA_HEADS))
    sa8 = jnp.tile(sa1, (1, MLA_HEADS))
    sb8 = jnp.tile(sb1, (1, MLA_HEADS))

    q_lat = _dot(hn, win_ref[:, _C_Q:_C_KV])
    qn = _rms(q_lat, qg_ref[...]).astype(BF16)
    q = _rope(_dot(qn, wq_ref[...]), c8, sa8, sb8)

    kv_lat = _dot(hn, win_ref[:, _C_KV:_C_KR])
    kvn = _rms(kv_lat, kvg_ref[...]).astype(BF16)
    k_nope = _dot(kvn, wkv_ref[:, 0:MLA_HEADS * HEAD_PAD])
    kr = _rope(_dot(hn, win_ref[:, _C_KR:_C_POOL]), c1, sa1, sb1)
    k_out[...] = (k_nope + jnp.tile(kr, (1, MLA_HEADS))).astype(BF16)
    v = _dot(kvn, wkv_ref[:, MLA_HEADS * HEAD_PAD:])
    for t in range(tm // Q_TILE):
        qT_out[t] = q[t * Q_TILE:(t + 1) * Q_TILE, :].T.astype(BF16)
    vT = v.T
    ones = jnp.ones((V_ROWS - V_DIM, tm), F32)
    vT = jnp.concatenate(
        [blk for h in range(MLA_HEADS) for blk in (vT[h * V_DIM:(h + 1) * V_DIM, :], ones)], axis=0).astype(BF16)
    for t in range(tm // KV_TILE):
        vT_out[t] = vT[:, t * KV_TILE:(t + 1) * KV_TILE]

    u = _dot(hn, win_ref[:, _C_POOL:_C_GA])

    @pl.when(i == 0)
    def _():
        hist_ref[...] = jnp.zeros_like(hist_ref)

    ext = jnp.concatenate([hist_ref[...], u], axis=0)
    hist_ref[...] = u[tm - POOL_HALO:, :]
    t_idx = i * tm + lax.broadcasted_iota(jnp.int32, (tm, POOL_GROUP_DIM), 0)
    ys = []
    for g, w in enumerate(POOL_WINDOWS):
        c0 = g * POOL_GROUP_DIM
        run = ext[:, c0:c0 + POOL_GROUP_DIM]
        span = 1
        while span < w:
            run = run + pltpu.roll(run, span, 0)
            span *= 2
        cnt = jnp.minimum(t_idx + 1, w).astype(F32)
        d = run[POOL_HALO:, :] / cnt - u[:, c0:c0 + POOL_GROUP_DIM]
        ys.append(_dot(d.astype(BF16), poolw_ref[g]))
    y = (jnp.concatenate(ys, axis=1) * pscale_ref[...]).astype(BF16)
    y_b = _dot(y, wpb_ref[...])

    g_a = jax.nn.sigmoid(_dot(hn, win_ref[:, _C_GA:_C_GB]))
    ga_out[...] = g_a.astype(BF16)
    g_b = jax.nn.sigmoid(_dot(hn, win_ref[:, _C_GB:_C_END]))
    gyb_out[...] = (g_b * y_b).astype(BF16)


def _in_proj(x, ctab, satab, sbtab, g, win, qg, wq, kvg, wkv, poolw, pscale, wpb, tm):
    B, S, _ = x.shape
    row = lambda b, i: (b, i, 0)
    const2 = lambda b, i: (0, 0)
    const3 = lambda b, i: (0, 0, 0)
    slab = lambda b, i: (b, i, 0, 0)
    return pl.pallas_call(
        _in_proj_kernel,
        grid=(B, S // tm),
        in_specs=[
            pl.BlockSpec((None, tm, D_MODEL), row),
            pl.BlockSpec((None, tm, LANES), row),
            pl.BlockSpec((None, tm, LANES), row),
            pl.BlockSpec((None, tm, LANES), row),
            pl.BlockSpec((1, D_MODEL), const2),
            pl.BlockSpec(win.shape, const2),
            pl.BlockSpec((1, Q_LORA), const2),
            pl.BlockSpec(wq.shape, const2),
            pl.BlockSpec((1, KV_LORA), const2),
            pl.BlockSpec(wkv.shape, const2),
            pl.BlockSpec(poolw.shape, const3),
            pl.BlockSpec((1, POOL_WIDTH), const2),
            pl.BlockSpec(wpb.shape, const2),
        ],
        out_specs=[
            pl.BlockSpec((None, tm // Q_TILE, MLA_HEADS * HEAD_PAD, Q_TILE), slab),
            pl.BlockSpec((None, tm, MLA_HEADS * HEAD_PAD), row),
            pl.BlockSpec((None, tm // KV_TILE, MLA_HEADS * V_ROWS, KV_TILE), slab),
            pl.BlockSpec((None, tm, D_MODEL), row),
            pl.BlockSpec((None, tm, D_MODEL), row),
        ],
        out_shape=[jax.ShapeDtypeStruct((B, S // Q_TILE, MLA_HEADS * HEAD_PAD, Q_TILE), BF16),
                   jax.ShapeDtypeStruct((B, S, MLA_HEADS * HEAD_PAD), BF16),
                   jax.ShapeDtypeStruct((B, S // KV_TILE, MLA_HEADS * V_ROWS, KV_TILE), BF16),
                   jax.ShapeDtypeStruct((B, S, D_MODEL), BF16),
                   jax.ShapeDtypeStruct((B, S, D_MODEL), BF16)],
        scratch_shapes=[pltpu.VMEM((POOL_HALO, POOL_WIDTH), F32)],
        compiler_params=pltpu.CompilerParams(
            dimension_semantics=("arbitrary", "arbitrary"), vmem_limit_bytes=VMEM_LIMIT),
        name="in_proj",
    )(x, ctab, satab, sbtab, g, win, qg, wq, kvg, wkv, poolw, pscale, wpb)


def _mla_kernel(qT_ref, k_ref, vT_ref, o_ref, m_ref, acc_ref):
    i = pl.program_id(1)
    n_sub = KV_TILE // KEY_SUB
    units = [(h, c) for c in range(n_sub) for h in range(MLA_HEADS)]
    qry_c = lax.broadcasted_iota(jnp.int32, (KEY_SUB, Q_TILE), 1) // CHUNK

    def scores(j, h, c):
        hs = slice(h * HEAD_PAD, (h + 1) * HEAD_PAD)
        rows = pl.ds(pl.multiple_of(j * KV_TILE + c * KEY_SUB, KEY_SUB), KEY_SUB)
        return _dot(k_ref[rows, hs], qT_ref[hs, :])

    def fold(h, alpha, pv):
        acc_ref[h] = pv if alpha is None else alpha * acc_ref[h] + pv

    def sweep(j, diag):
        ahead = [scores(j, *u) for u in units[:QK_AHEAD]]
        pending = []
        for n, (h, c) in enumerate(units):
            s = ahead.pop(0)
            if n + QK_AHEAD < len(units):
                ahead.append(scores(j, *units[n + QK_AHEAD]))
            first = False
            if diag is not None:
                key_c0 = (diag * KV_TILE + c * KEY_SUB) // CHUNK
                key_c = key_c0 + lax.broadcasted_iota(jnp.int32, (KEY_SUB, Q_TILE), 0) // CHUNK
                s = jnp.where(key_c <= qry_c, s, -jnp.inf)
                first = diag == 0 and c == 0
            s_max = jnp.max(s, axis=0, keepdims=True)
            m_new = s_max if first else jnp.maximum(m_ref[h], s_max)
            p = jnp.exp2(s - m_new).astype(BF16)
            pv = _dot(vT_ref[j, h * V_ROWS:(h + 1) * V_ROWS, c * KEY_SUB:(c + 1) * KEY_SUB], p)
            alpha = None if first else jnp.exp2(m_ref[h] - m_new)
            m_ref[h] = m_new
            pending.append((h, alpha, pv))
            if len(pending) > PV_BEHIND:
                fold(*pending.pop(0))
        for item in pending:
            fold(*item)

    n_diag = Q_TILE // KV_TILE
    for d in range(n_diag):
        sweep(i * n_diag + d, d)

    def body(j, carry):
        sweep(j, None)
        return carry

    lax.fori_loop(0, i * n_diag, body, 0)
    oT = jnp.concatenate([acc_ref[h, :V_DIM, :] / acc_ref[h, V_DIM:V_DIM + 1, :] for h in range(MLA_HEADS)], axis=0)
    o_ref[...] = oT.T.astype(BF16)


def _mla_attention(qT, k, vT):
    B, S, W = k.shape
    return pl.pallas_call(
        _mla_kernel,
        grid=(B, S // Q_TILE),
        in_specs=[
            pl.BlockSpec((None, None, W, Q_TILE), lambda b, i: (b, i, 0, 0)),
            pl.BlockSpec((None, S, W), lambda b, i: (b, 0, 0)),
            pl.BlockSpec((None, S // KV_TILE, MLA_HEADS * V_ROWS, KV_TILE), lambda b, i: (b, 0, 0, 0)),
        ],
        out_specs=pl.BlockSpec((None, Q_TILE, MLA_HEADS * V_DIM), lambda b, i: (b, i, 0)),
        out_shape=jax.ShapeDtypeStruct((B, S, MLA_HEADS * V_DIM), BF16),
        scratch_shapes=[pltpu.VMEM((MLA_HEADS, 1, Q_TILE), F32),
                        pltpu.VMEM((MLA_HEADS, V_ROWS, Q_TILE), F32)],
        compiler_params=pltpu.CompilerParams(
            dimension_semantics=("arbitrary", "arbitrary"), vmem_limit_bytes=VMEM_LIMIT),
        name="mla_attn",
    )(qT, k, vT)


def _route(logits, bias):
    tm = logits.shape[0]
    lane = lax.broadcasted_iota(jnp.int32, (tm, LANES), 1)
    neg = -jnp.inf
    lg = jnp.where(lane < N_GROUPS, logits[:, LANES:] + bias[:, LANES:], neg)
    ge = jnp.exp(lg - jnp.max(lg, axis=-1, keepdims=True))
    gp = ge / jnp.sum(ge, axis=-1, keepdims=True)
    g_w = jnp.max(gp, axis=-1, keepdims=True)
    g_idx = jnp.min(jnp.where(gp == g_w, lane, LANES), axis=-1, keepdims=True)

    sel = (lane // EXPERTS_PER_GROUP == g_idx) & (lane < N_EXPERTS)
    le = jnp.where(sel, logits[:, :LANES] + bias[:, :LANES], neg)
    ee = jnp.exp(le - jnp.max(le, axis=-1, keepdims=True))
    ep = jnp.where(sel, ee / jnp.sum(ee, axis=-1, keepdims=True), -1.0)
    w1 = jnp.max(ep, axis=-1, keepdims=True)
    i1 = jnp.min(jnp.where(ep == w1, lane, LANES), axis=-1, keepdims=True)
    ep2 = jnp.where(lane == i1, -1.0, ep)
    w2 = jnp.max(ep2, axis=-1, keepdims=True)
    i2 = jnp.min(jnp.where(ep2 == w2, lane, LANES), axis=-1, keepdims=True)
    den = w1 + w2
    return jnp.where(lane == i1, g_w * (w1 / den), 0.0) + jnp.where(lane == i2, g_w * (w2 / den), 0.0)


def _mix_kernel(x_ref, attn_ref, ga_ref, gyb_ref, kv_ref, wab_ref, wmix_ref, xg_ref, wxq_ref, wxo_ref,
                fg_ref, wr_hi_ref, wr_lo_ref, rb_ref,
                h_out, xn_out, cw_out):
    y_a = _dot(attn_ref[...], wab_ref[...])
    merged = (ga_ref[...].astype(F32) * y_a + gyb_ref[...].astype(F32)).astype(BF16)
    h1 = x_ref[...] + _dot(merged, wmix_ref[...])

    hn = _rms(h1, xg_ref[...]).astype(BF16)
    q = _dot(hn, wxq_ref[...]).astype(BF16)
    heads = []
    for h in range(MEM_HEADS):
        hs = slice(h * MEM_HEAD_DIM, (h + 1) * MEM_HEAD_DIM)
        s = _dot_nt(q[:, hs], kv_ref[:, hs])
        p = jnp.exp(s - jnp.max(s, axis=-1, keepdims=True))
        o = _dot(p.astype(BF16), kv_ref[:, D_MODEL + h * MEM_HEAD_DIM:D_MODEL + (h + 1) * MEM_HEAD_DIM])
        heads.append((o / jnp.sum(p, axis=-1, keepdims=True)).astype(BF16))
    h2 = h1 + _dot(jnp.concatenate(heads, axis=1), wxo_ref[...])
    h_out[...] = h2

    xn = _rms(h2, fg_ref[...])
    xn_hi = xn.astype(BF16)
    xn_out[...] = xn_hi
    xn_lo = (xn - xn_hi.astype(F32)).astype(BF16)
    logits = _dot(xn_hi, wr_hi_ref[...]) + (_dot(xn_hi, wr_lo_ref[...]) + _dot(xn_lo, wr_hi_ref[...]))
    cw_out[...] = _route(logits, rb_ref[...])


def _mix_xattn(x, attn, ga, gyb, memkv, wab, wmix, xg, wxq, wxo, fg, wr_hi, wr_lo, rb, tm):
    B, S, _ = x.shape
    M = memkv.shape[1]
    row = lambda b, i: (b, i, 0)
    const2 = lambda b, i: (0, 0)
    return pl.pallas_call(
        _mix_kernel,
        grid=(B, S // tm),
        in_specs=[
            pl.BlockSpec((None, tm, D_MODEL), row),
            pl.BlockSpec((None, tm, MLA_HEADS * V_DIM), row),
            pl.BlockSpec((None, tm, D_MODEL), row),
            pl.BlockSpec((None, tm, D_MODEL), row),
            pl.BlockSpec((None, M, 2 * D_MODEL), lambda b, i: (b, 0, 0)),
            pl.BlockSpec(wab.shape, const2),
            pl.BlockSpec(wmix.shape, const2),
            pl.BlockSpec((1, D_MODEL), const2),
            pl.BlockSpec(wxq.shape, const2),
            pl.BlockSpec(wxo.shape, const2),
            pl.BlockSpec((1, D_MODEL), const2),
            pl.BlockSpec(wr_hi.shape, const2),
            pl.BlockSpec(wr_lo.shape, const2),
            pl.BlockSpec((1, 2 * LANES), const2),
        ],
        out_specs=[
            pl.BlockSpec((None, tm, D_MODEL), row),
            pl.BlockSpec((None, tm, D_MODEL), row),
            pl.BlockSpec((None, tm, LANES), row),
        ],
        out_shape=[jax.ShapeDtypeStruct((B, S, D_MODEL), F32),
                   jax.ShapeDtypeStruct((B, S, D_MODEL), BF16),
                   jax.ShapeDtypeStruct((B, S, LANES), F32)],
        compiler_params=pltpu.CompilerParams(
            dimension_semantics=("arbitrary", "arbitrary"), vmem_limit_bytes=VMEM_LIMIT),
        name="mix_xattn",
    )(x, attn, ga, gyb, memkv, wab, wmix, xg, wxq, wxo, fg, wr_hi, wr_lo, rb)


def _moe_kernel(h_ref, xn_ref, cw_ref, wgu_ref, wd_ref, fg_ref, o_ref, acc_ref):
    e = pl.program_id(1)

    @pl.when(e == 0)
    def _():
        acc_ref[...] = h_ref[...]

    xn = xn_ref[...]
    gu = _dot(xn, wgu_ref[...])
    gate = gu[:, :EXPERT_FF]
    hid = gate * jax.nn.sigmoid(gate) * gu[:, EXPERT_FF:]
    lane = lax.broadcasted_iota(jnp.int32, cw_ref.shape, 1)
    cw_e = jnp.sum(jnp.where(lane == e, cw_ref[...], 0.0), axis=-1, keepdims=True)
    acc_ref[...] += _dot((hid * cw_e).astype(BF16), wd_ref[...])

    @pl.when(e == pl.num_programs(1) - 1)
    def _():
        o_ref[...] = _rms(acc_ref[...], fg_ref[...])


def _moe(h2, xn, cw, wgu, wd, fg, tm):
    T = h2.shape[0]
    return pl.pallas_call(
        _moe_kernel,
        grid=(T // tm, N_EXPERTS),
        in_specs=[
            pl.BlockSpec((tm, D_MODEL), lambda i, e: (i, 0)),
            pl.BlockSpec((tm, D_MODEL), lambda i, e: (i, 0)),
            pl.BlockSpec((tm, LANES), lambda i, e: (i, 0)),
            pl.BlockSpec((None, D_MODEL, 2 * EXPERT_FF), lambda i, e: (e, 0, 0)),
            pl.BlockSpec((None, EXPERT_FF, D_MODEL), lambda i, e: (e, 0, 0)),
            pl.BlockSpec((1, D_MODEL), lambda i, e: (0, 0)),
        ],
        out_specs=pl.BlockSpec((tm, D_MODEL), lambda i, e: (i, 0)),
        out_shape=jax.ShapeDtypeStruct((T, D_MODEL), F32),
        scratch_shapes=[pltpu.VMEM((tm, D_MODEL), F32)],
        compiler_params=pltpu.CompilerParams(
            dimension_semantics=("arbitrary", "arbitrary"), vmem_limit_bytes=VMEM_LIMIT),
        name="moe",
    )(h2, xn, cw, wgu, wd, fg)


def _rope_tables(positions):
    inv_freq = 1.0 / (ROPE_THETA ** (jnp.arange(0, ROPE_DIM, 2, dtype=F32) / ROPE_DIM))
    ang = positions.astype(F32)[..., None] * inv_freq
    cos, sin = jnp.cos(ang), jnp.sin(ang)
    half = ROPE_DIM // 2
    z = lambda n: jnp.zeros(cos.shape[:-1] + (n,), F32)
    ctab = jnp.concatenate([jnp.ones(cos.shape[:-1] + (NOPE_DIM,), F32), cos, cos, z(LANES - NOPE_DIM - ROPE_DIM)], -1)
    satab = jnp.concatenate([z(NOPE_DIM + half), sin, z(LANES - NOPE_DIM - ROPE_DIM)], -1)
    sbtab = jnp.concatenate([z(NOPE_DIM), -sin, z(half + LANES - NOPE_DIM - ROPE_DIM)], -1)
    return ctab, satab, sbtab


def _pad_heads(w, heads, width):
    k = w.shape[0]
    w = w.reshape(k, heads, width)
    w = jnp.pad(w, ((0, 0), (0, 0), (0, HEAD_PAD - width)))
    return w.reshape(k, heads * HEAD_PAD)


def _layer(l, h, mem, tables, mix_norm_g, w_in, q_norm_g, w_q_up, kv_norm_g, w_kv_up, w_attn_branch,
           pool_w, pool_scale, w_pool_branch, w_mix_out, xattn_norm_g, mem_norm_g, w_xq, w_xkv, w_xo,
           ffn_norm_g, w_router_group, b_router_group, w_router_expert, b_router_expert,
           w_exp_gate, w_exp_up, w_exp_down, out_g, tm_proj, tm_mix, tm_moe):
    B, S, _ = h.shape
    row2 = lambda v: v.reshape(1, -1).astype(F32)

    wi = w_in[l]
    kr_cols = jnp.pad(wi[:, Q_LORA + KV_LORA:Q_LORA + KV_LORA + ROPE_DIM],
                      ((0, 0), (NOPE_DIM, LANES - NOPE_DIM - ROPE_DIM)))
    win = jnp.concatenate([wi[:, :Q_LORA + KV_LORA], kr_cols, wi[:, Q_LORA + KV_LORA + ROPE_DIM:]], axis=1).astype(BF16)
    scale = math.log2(math.e) / math.sqrt(NOPE_DIM + ROPE_DIM)
    wq = _pad_heads(w_q_up[l] * scale, MLA_HEADS, NOPE_DIM + ROPE_DIM).astype(BF16)
    wkv3 = w_kv_up[l].reshape(KV_LORA, MLA_HEADS, NOPE_DIM + V_DIM)
    wkv = jnp.concatenate([
        _pad_heads(wkv3[:, :, :NOPE_DIM].reshape(KV_LORA, -1), MLA_HEADS, NOPE_DIM),
        wkv3[:, :, NOPE_DIM:].reshape(KV_LORA, -1)], axis=1).astype(BF16)

    memkv = _mem_kv(mem.reshape(-1, D_MODEL), row2(mem_norm_g[l]), w_xkv[l].astype(BF16))
    memkv = memkv.reshape(B, -1, 2 * D_MODEL)

    qT, k, vT, ga, gyb = _in_proj(
        h, *tables, row2(mix_norm_g[l]), win, row2(q_norm_g[l]), wq, row2(kv_norm_g[l]), wkv,
        pool_w[l].astype(BF16), row2(pool_scale[l]), w_pool_branch[l].astype(BF16), tm_proj)
    attn = _mla_attention(qT, k, vT)

    w_r = jnp.zeros((D_MODEL, 2 * LANES), F32)
    w_r = w_r.at[:, :N_EXPERTS].set(w_router_expert[l]).at[:, LANES:LANES + N_GROUPS].set(w_router_group[l])
    wr_hi = w_r.astype(BF16)
    wr_lo = (w_r - wr_hi.astype(F32)).astype(BF16)
    rb = jnp.zeros((1, 2 * LANES), F32)
    rb = rb.at[0, :N_EXPERTS].set(b_router_expert[l]).at[0, LANES:LANES + N_GROUPS].set(b_router_group[l])

    h2, xn, cw = _mix_xattn(
        h, attn, ga, gyb, memkv, w_attn_branch[l].astype(BF16), w_mix_out[l].astype(BF16),
        row2(xattn_norm_g[l]), (w_xq[l] * (1.0 / math.sqrt(MEM_HEAD_DIM))).astype(BF16), w_xo[l].astype(BF16),
        row2(ffn_norm_g[l]), wr_hi, wr_lo, rb, tm_mix)

    wgu = jnp.concatenate([w_exp_gate[l], w_exp_up[l]], axis=-1).astype(BF16)
    out = _moe(h2.reshape(B * S, D_MODEL), xn.reshape(B * S, D_MODEL), cw.reshape(B * S, LANES),
               wgu, w_exp_down[l].astype(BF16), row2(out_g), tm_moe)
    return out.reshape(B, S, D_MODEL)


def kernel(x, mem, positions, mix_norm_g, w_in, q_norm_g, w_q_up, kv_norm_g, w_kv_up, w_attn_branch, pool_w, pool_scale, w_pool_branch, w_mix_out, xattn_norm_g, mem_norm_g, w_xq, w_xkv, w_xo, ffn_norm_g, w_router_group, b_router_group, w_router_expert, b_router_expert, w_exp_gate, w_exp_up, w_exp_down, final_norm_g):
    depth = w_in.shape[0]
    assert depth == 1, "the experts kernel fuses the final RMSNorm, which is only valid after the last layer"
    S = x.shape[1]
    tables = _rope_tables(positions)
    tm_proj = min(512, S)
    tm_mix = min(512, S)
    tm_moe = min(1024, S)
    return _layer(0, x, mem, tables, mix_norm_g, w_in, q_norm_g, w_q_up, kv_norm_g, w_kv_up, w_attn_branch,
                  pool_w, pool_scale, w_pool_branch, w_mix_out, xattn_norm_g, mem_norm_g, w_xq, w_xkv, w_xo,
                  ffn_norm_g, w_router_group, b_router_group, w_router_expert, b_router_expert,
                  w_exp_gate, w_exp_up, w_exp_down, final_norm_g, tm_proj, tm_mix, tm_moe)
```

```python
import functools
import math

import jax
import jax.numpy as jnp
from jax import lax
from jax.experimental import pallas as pl
from jax.experimental.pallas import tpu as pltpu

F32 = jnp.float32
BF16 = jnp.bfloat16

D_MODEL = 1024
CHUNK = 64
MLA_HEADS = 8
Q_LORA = 384
KV_LORA = 256
NOPE_DIM = 64
ROPE_DIM = 32
V_DIM = 64
ROPE_THETA = 10000.0
POOL_WIDTH = 512
POOL_WINDOWS = (2, 4, 8, 16)
POOL_GROUP_DIM = POOL_WIDTH // len(POOL_WINDOWS)
POOL_HALO = 16
MEM_HEADS = 4
MEM_HEAD_DIM = D_MODEL // MEM_HEADS
N_GROUPS = 4
EXPERTS_PER_GROUP = 8
N_EXPERTS = N_GROUPS * EXPERTS_PER_GROUP
EXPERT_FF = 256
EPS = 1e-6

LANES = 128
HEAD_PAD = LANES
Q_TILE = 512
KV_TILE = 256
KEY_SUB = 128
V_ROWS = V_DIM + 16
IN_PROJ_TILE = 512
MOE_CHUNK = 512
PIECE = 16
MAX_PIECES = 2 * MOE_CHUNK // PIECE + N_EXPERTS
CHUNK_ROWS = MAX_PIECES * PIECE
EXPERT_TILE = 256
QK_AHEAD = 3
PV_BEHIND = 2

_C_Q = 0
_C_KV = _C_Q + Q_LORA
_C_KR = _C_KV + KV_LORA
_C_POOL = _C_KR + LANES
_C_GA = _C_POOL + POOL_WIDTH
_C_GB = _C_GA + D_MODEL
_C_END = _C_GB + D_MODEL

VMEM_LIMIT = 56 * 1024 * 1024


def _rms(x, g):
    return x * lax.rsqrt(jnp.mean(x * x, axis=-1, keepdims=True) + EPS) * g


def _dot(a, b):
    return jnp.dot(a, b, preferred_element_type=F32)


def _dot_nt(a, b):
    return lax.dot_general(a, b, (((1,), (1,)), ((), ())), preferred_element_type=F32)


def _mem_kv_kernel(mem_ref, g_ref, w_ref, kv_ref):
    mn = _rms(mem_ref[...], g_ref[...]).astype(BF16)
    kv_ref[...] = _dot(mn, w_ref[...]).astype(BF16)


def _mem_kv(mem2d, g, w_xkv):
    rows = mem2d.shape[0]
    tm = min(512, rows)
    assert rows % tm == 0
    return pl.pallas_call(
        _mem_kv_kernel,
        grid=(rows // tm,),
        in_specs=[
            pl.BlockSpec((tm, D_MODEL), lambda i: (i, 0)),
            pl.BlockSpec((1, D_MODEL), lambda i: (0, 0)),
            pl.BlockSpec((D_MODEL, 2 * D_MODEL), lambda i: (0, 0)),
        ],
        out_specs=pl.BlockSpec((tm, 2 * D_MODEL), lambda i: (i, 0)),
        out_shape=jax.ShapeDtypeStruct((rows, 2 * D_MODEL), BF16),
        compiler_params=pltpu.CompilerParams(vmem_limit_bytes=VMEM_LIMIT),
        name="mem_kv",
    )(mem2d, g, w_xkv)


def _rope(t, c, sa, sb):
    w = t.shape[-1]
    return t * c + pltpu.roll(t, ROPE_DIM // 2, 1) * sa + pltpu.roll(t, w - ROPE_DIM // 2, 1) * sb


def _in_proj_kernel(x_ref, c_ref, sa_ref, sb_ref, g_ref, win_ref, qg_ref, wq_ref, kvg_ref, wkv_ref,
                    poolw_ref, pscale_ref, wpb_ref,
                    qT_out, k_out, vT_out, ga_out, gyb_out, hist_ref):
    tm = x_ref.shape[0]
    i = pl.program_id(1)
    hn = _rms(x_ref[...], g_ref[...]).astype(BF16)

    c1 = c_ref[...]
    sa1 = sa_ref[...]
    sb1 = sb_ref[...]
    c8 = jnp.tile(c1, (1, MLA_HEADS))
    sa8 = jnp.tile(sa1, (1, MLA_HEADS))
    sb8 = jnp.tile(sb1, (1, MLA_HEADS))

    q_lat = _dot(hn, win_ref[:, _C_Q:_C_KV])
    qn = _rms(q_lat, qg_ref[...]).astype(BF16)
    q = _rope(_dot(qn, wq_ref[...]), c8, sa8, sb8)

    kv_lat = _dot(hn, win_ref[:, _C_KV:_C_KR])
    kvn = _rms(kv_lat, kvg_ref[...]).astype(BF16)
    k_nope = _dot(kvn, wkv_ref[:, 0:MLA_HEADS * HEAD_PAD])
    kr = _rope(_dot(hn, win_ref[:, _C_KR:_C_POOL]), c1, sa1, sb1)
    k_out[...] = (k_nope + jnp.tile(kr, (1, MLA_HEADS))).astype(BF16)
    v = _dot(kvn, wkv_ref[:, MLA_HEADS * HEAD_PAD:])
    for t in range(tm // Q_TILE):
        qT_out[t] = q[t * Q_TILE:(t + 1) * Q_TILE, :].T.astype(BF16)
    vT = v.T
    ones = jnp.ones((V_ROWS - V_DIM, tm), F32)
    vT = jnp.concatenate(
        [blk for h in range(MLA_HEADS) for blk in (vT[h * V_DIM:(h + 1) * V_DIM, :], ones)], axis=0).astype(BF16)
    for t in range(tm // KV_TILE):
        vT_out[t] = vT[:, t * KV_TILE:(t + 1) * KV_TILE]

    u = _dot(hn, win_ref[:, _C_POOL:_C_GA])

    @pl.when(i == 0)
    def _():
        hist_ref[...] = jnp.zeros_like(hist_ref)

    ext = jnp.concatenate([hist_ref[...], u], axis=0)
    hist_ref[...] = u[tm - POOL_HALO:, :]
    t_idx = i * tm + lax.broadcasted_iota(jnp.int32, (tm, POOL_GROUP_DIM), 0)
    ys = []
    for g, w in enumerate(POOL_WINDOWS):
        c0 = g * POOL_GROUP_DIM
        run = ext[:, c0:c0 + POOL_GROUP_DIM]
        span = 1
        while span < w:
            run = run + pltpu.roll(run, span, 0)
            span *= 2
        cnt = jnp.minimum(t_idx + 1, w).astype(F32)
        d = run[POOL_HALO:, :] / cnt - u[:, c0:c0 + POOL_GROUP_DIM]
        ys.append(_dot(d.astype(BF16), poolw_ref[g]))
    y = (jnp.concatenate(ys, axis=1) * pscale_ref[...]).astype(BF16)
    y_b = _dot(y, wpb_ref[...])

    g_a = jax.nn.sigmoid(_dot(hn, win_ref[:, _C_GA:_C_GB]))
    ga_out[...] = g_a.astype(BF16)
    g_b = jax.nn.sigmoid(_dot(hn, win_ref[:, _C_GB:_C_END]))
    gyb_out[...] = (g_b * y_b).astype(BF16)


def _in_proj(x, ctab, satab, sbtab, g, win, qg, wq, kvg, wkv, poolw, pscale, wpb, tm):
    B, S, _ = x.shape
    row = lambda b, i: (b, i, 0)
    const2 = lambda b, i: (0, 0)
    const3 = lambda b, i: (0, 0, 0)
    slab = lambda b, i: (b, i, 0, 0)
    return pl.pallas_call(
        _in_proj_kernel,
        grid=(B, S // tm),
        in_specs=[
            pl.BlockSpec((None, tm, D_MODEL), row),
            pl.BlockSpec((None, tm, LANES), row),
            pl.BlockSpec((None, tm, LANES), row),
            pl.BlockSpec((None, tm, LANES), row),
            pl.BlockSpec((1, D_MODEL), const2),
            pl.BlockSpec(win.shape, const2),
            pl.BlockSpec((1, Q_LORA), const2),
            pl.BlockSpec(wq.shape, const2),
            pl.BlockSpec((1, KV_LORA), const2),
            pl.BlockSpec(wkv.shape, const2),
            pl.BlockSpec(poolw.shape, const3),
            pl.BlockSpec((1, POOL_WIDTH), const2),
            pl.BlockSpec(wpb.shape, const2),
        ],
        out_specs=[
            pl.BlockSpec((None, tm // Q_TILE, MLA_HEADS * HEAD_PAD, Q_TILE), slab),
            pl.BlockSpec((None, tm, MLA_HEADS * HEAD_PAD), row),
            pl.BlockSpec((None, tm // KV_TILE, MLA_HEADS * V_ROWS, KV_TILE), slab),
            pl.BlockSpec((None, tm, D_MODEL), row),
            pl.BlockSpec((None, tm, D_MODEL), row),
        ],
        out_shape=[jax.ShapeDtypeStruct((B, S // Q_TILE, MLA_HEADS * HEAD_PAD, Q_TILE), BF16),
                   jax.ShapeDtypeStruct((B, S, MLA_HEADS * HEAD_PAD), BF16),
                   jax.ShapeDtypeStruct((B, S // KV_TILE, MLA_HEADS * V_ROWS, KV_TILE), BF16),
                   jax.ShapeDtypeStruct((B, S, D_MODEL), BF16),
                   jax.ShapeDtypeStruct((B, S, D_MODEL), BF16)],
        scratch_shapes=[pltpu.VMEM((POOL_HALO, POOL_WIDTH), F32)],
        compiler_params=pltpu.CompilerParams(
            dimension_semantics=("arbitrary", "arbitrary"), vmem_limit_bytes=VMEM_LIMIT),
        name="in_proj",
    )(x, ctab, satab, sbtab, g, win, qg, wq, kvg, wkv, poolw, pscale, wpb)


def _mla_kernel(qT_ref, k_ref, vT_ref, o_ref, m_ref, acc_ref):
    i = pl.program_id(1)
    n_sub = KV_TILE // KEY_SUB
    units = [(h, c) for c in range(n_sub) for h in range(MLA_HEADS)]
    qry_c = lax.broadcasted_iota(jnp.int32, (KEY_SUB, Q_TILE), 1) // CHUNK

    def scores(j, h, c):
        hs = slice(h * HEAD_PAD, (h + 1) * HEAD_PAD)
        rows = pl.ds(pl.multiple_of(j * KV_TILE + c * KEY_SUB, KEY_SUB), KEY_SUB)
        return _dot(k_ref[rows, hs], qT_ref[hs, :])

    def fold(h, alpha, pv):
        acc_ref[h] = pv if alpha is None else alpha * acc_ref[h] + pv

    def sweep(j, diag):
        ahead = [scores(j, *u) for u in units[:QK_AHEAD]]
        pending = []
        for n, (h, c) in enumerate(units):
            s = ahead.pop(0)
            if n + QK_AHEAD < len(units):
                ahead.append(scores(j, *units[n + QK_AHEAD]))
            first = False
            if diag is not None:
                key_c0 = (diag * KV_TILE + c * KEY_SUB) // CHUNK
                key_c = key_c0 + lax.broadcasted_iota(jnp.int32, (KEY_SUB, Q_TILE), 0) // CHUNK
                s = jnp.where(key_c <= qry_c, s, -jnp.inf)
                first = diag == 0 and c == 0
            s_max = jnp.max(s, axis=0, keepdims=True)
            m_new = s_max if first else jnp.maximum(m_ref[h], s_max)
            p = jnp.exp2(s - m_new).astype(BF16)
            pv = _dot(vT_ref[j, h * V_ROWS:(h + 1) * V_ROWS, c * KEY_SUB:(c + 1) * KEY_SUB], p)
            alpha = None if first else jnp.exp2(m_ref[h] - m_new)
            m_ref[h] = m_new
            pending.append((h, alpha, pv))
            if len(pending) > PV_BEHIND:
                fold(*pending.pop(0))
        for item in pending:
            fold(*item)

    n_diag = Q_TILE // KV_TILE
    for d in range(n_diag):
        sweep(i * n_diag + d, d)

    def body(j, carry):
        sweep(j, None)
        return carry

    lax.fori_loop(0, i * n_diag, body, 0)
    oT = jnp.concatenate([acc_ref[h, :V_DIM, :] / acc_ref[h, V_DIM:V_DIM + 1, :] for h in range(MLA_HEADS)], axis=0)
    o_ref[...] = oT.T.astype(BF16)


def _mla_attention(qT, k, vT):
    B, S, W = k.shape
    return pl.pallas_call(
        _mla_kernel,
        grid=(B, S // Q_TILE),
        in_specs=[
            pl.BlockSpec((None, None, W, Q_TILE), lambda b, i: (b, i, 0, 0)),
            pl.BlockSpec((None, S, W), lambda b, i: (b, 0, 0)),
            pl.BlockSpec((None, S // KV_TILE, MLA_HEADS * V_ROWS, KV_TILE), lambda b, i: (b, 0, 0, 0)),
        ],
        out_specs=pl.BlockSpec((None, Q_TILE, MLA_HEADS * V_DIM), lambda b, i: (b, i, 0)),
        out_shape=jax.ShapeDtypeStruct((B, S, MLA_HEADS * V_DIM), BF16),
        scratch_shapes=[pltpu.VMEM((MLA_HEADS, 1, Q_TILE), F32),
                        pltpu.VMEM((MLA_HEADS, V_ROWS, Q_TILE), F32)],
        compiler_params=pltpu.CompilerParams(
            dimension_semantics=("arbitrary", "arbitrary"), vmem_limit_bytes=VMEM_LIMIT),
        name="mla_attn",
    )(qT, k, vT)


def _route(logits, bias, tri, upper):
    tm = logits.shape[0]
    lane = lax.broadcasted_iota(jnp.int32, (tm, LANES), 1)
    neg = -jnp.inf
    lg = jnp.where(lane < N_GROUPS, logits[:, LANES:] + bias[:, LANES:], neg)
    ge = jnp.exp(lg - jnp.max(lg, axis=-1, keepdims=True))
    gp = ge / jnp.sum(ge, axis=-1, keepdims=True)
    g_w = jnp.max(gp, axis=-1, keepdims=True)
    g_idx = jnp.min(jnp.where(gp == g_w, lane, LANES), axis=-1, keepdims=True)

    sel = (lane // EXPERTS_PER_GROUP == g_idx) & (lane < N_EXPERTS)
    le = jnp.where(sel, logits[:, :LANES] + bias[:, :LANES], neg)
    ee = jnp.exp(le - jnp.max(le, axis=-1, keepdims=True))
    ep = jnp.where(sel, ee / jnp.sum(ee, axis=-1, keepdims=True), -1.0)
    w1 = jnp.max(ep, axis=-1, keepdims=True)
    i1 = jnp.min(jnp.where(ep == w1, lane, LANES), axis=-1, keepdims=True)
    ep2 = jnp.where(lane == i1, -1.0, ep)
    w2 = jnp.max(ep2, axis=-1, keepdims=True)
    i2 = jnp.min(jnp.where(ep2 == w2, lane, LANES), axis=-1, keepdims=True)
    den = w1 + w2
    c1 = g_w * (w1 / den)
    c2 = g_w * (w2 / den)

    oh1 = (lane == i1).astype(F32)
    oh2 = (lane == i2).astype(F32)
    both = oh1 + oh2
    earlier = _dot(tri, both.astype(BF16))
    pieces = jnp.floor((jnp.sum(both, axis=0, keepdims=True) + (PIECE - 1)) * (1.0 / PIECE))
    start = _dot(jnp.broadcast_to(pieces, (8, LANES)).astype(BF16), upper)[0:1, :] * PIECE
    pos1 = jnp.sum(oh1 * (earlier + start), axis=-1, keepdims=True)
    pos2 = jnp.sum(oh2 * (earlier + start), axis=-1, keepdims=True)
    info = jnp.where(lane == 0, pos1, jnp.where(lane == 1, pos2, jnp.where(lane == 2, c1, jnp.where(lane == 3, c2, 0.0))))
    return info, pieces


def _mix_kernel(x_ref, attn_ref, ga_ref, gyb_ref, kv_ref, wab_ref, wmix_ref, xg_ref, wxq_ref, wxo_ref,
                fg_ref, wr_hi_ref, wr_lo_ref, rb_ref, tri_ref, upper_ref,
                h_out, xn_out, info_out, infoT_out, pieces_out):
    y_a = _dot(attn_ref[...], wab_ref[...])
    merged = (ga_ref[...].astype(F32) * y_a + gyb_ref[...].astype(F32)).astype(BF16)
    h1 = x_ref[...] + _dot(merged, wmix_ref[...])

    hn = _rms(h1, xg_ref[...]).astype(BF16)
    q = _dot(hn, wxq_ref[...]).astype(BF16)
    heads = []
    for h in range(MEM_HEADS):
        hs = slice(h * MEM_HEAD_DIM, (h + 1) * MEM_HEAD_DIM)
        s = _dot_nt(q[:, hs], kv_ref[:, hs])
        p = jnp.exp(s - jnp.max(s, axis=-1, keepdims=True))
        o = _dot(p.astype(BF16), kv_ref[:, D_MODEL + h * MEM_HEAD_DIM:D_MODEL + (h + 1) * MEM_HEAD_DIM])
        heads.append((o / jnp.sum(p, axis=-1, keepdims=True)).astype(BF16))
    h2 = h1 + _dot(jnp.concatenate(heads, axis=1), wxo_ref[...])
    h_out[...] = h2

    xn = _rms(h2, fg_ref[...])
    xn_hi = xn.astype(BF16)
    xn_out[...] = xn_hi
    xn_lo = (xn - xn_hi.astype(F32)).astype(BF16)
    logits = _dot(xn_hi, wr_hi_ref[...]) + (_dot(xn_hi, wr_lo_ref[...]) + _dot(xn_lo, wr_hi_ref[...]))
    info, pieces = _route(logits, rb_ref[...], tri_ref[...], upper_ref[...])
    info_out[...] = info
    infoT_out[...] = info.T[0:8, :]
    pieces_out[...] = jnp.broadcast_to(pieces, (8, LANES))


def _mix_xattn(x, attn, ga, gyb, memkv, wab, wmix, xg, wxq, wxo, fg, wr_hi, wr_lo, rb):
    B, S, _ = x.shape
    M = memkv.shape[1]
    tm = MOE_CHUNK
    nt = S // tm
    row = lambda b, i: (b, i, 0)
    const2 = lambda b, i: (0, 0)
    tri = (lax.broadcasted_iota(jnp.int32, (tm, tm), 0) > lax.broadcasted_iota(jnp.int32, (tm, tm), 1)).astype(BF16)
    upper = (lax.broadcasted_iota(jnp.int32, (LANES, LANES), 0)
             < lax.broadcasted_iota(jnp.int32, (LANES, LANES), 1)).astype(BF16)
    return pl.pallas_call(
        _mix_kernel,
        grid=(B, S // tm),
        in_specs=[
            pl.BlockSpec((None, tm, D_MODEL), row),
            pl.BlockSpec((None, tm, MLA_HEADS * V_DIM), row),
            pl.BlockSpec((None, tm, D_MODEL), row),
            pl.BlockSpec((None, tm, D_MODEL), row),
            pl.BlockSpec((None, M, 2 * D_MODEL), lambda b, i: (b, 0, 0)),
            pl.BlockSpec(wab.shape, const2),
            pl.BlockSpec(wmix.shape, const2),
            pl.BlockSpec((1, D_MODEL), const2),
            pl.BlockSpec(wxq.shape, const2),
            pl.BlockSpec(wxo.shape, const2),
            pl.BlockSpec((1, D_MODEL), const2),
            pl.BlockSpec(wr_hi.shape, const2),
            pl.BlockSpec(wr_lo.shape, const2),
            pl.BlockSpec((1, 2 * LANES), const2),
            pl.BlockSpec((tm, tm), const2),
            pl.BlockSpec((LANES, LANES), const2),
        ],
        out_specs=[
            pl.BlockSpec((None, tm, D_MODEL), row),
            pl.BlockSpec((None, tm, D_MODEL), row),
            pl.BlockSpec((None, tm, LANES), row),
            pl.BlockSpec((8, tm), lambda b, i: (0, b * nt + i)),
            pl.BlockSpec((None, 8, LANES), lambda b, i: (b * nt + i, 0, 0)),
        ],
        out_shape=[jax.ShapeDtypeStruct((B, S, D_MODEL), F32),
                   jax.ShapeDtypeStruct((B, S, D_MODEL), BF16),
                   jax.ShapeDtypeStruct((B, S, LANES), F32),
                   jax.ShapeDtypeStruct((8, B * S), F32),
                   jax.ShapeDtypeStruct((B * nt, 8, LANES), F32)],
        compiler_params=pltpu.CompilerParams(
            dimension_semantics=("arbitrary", "arbitrary"), vmem_limit_bytes=VMEM_LIMIT),
        name="mix_xattn",
    )(x, attn, ga, gyb, memkv, wab, wmix, xg, wxq, wxo, fg, wr_hi, wr_lo, rb, tri, upper)


def _piece_copy(src_ref, dst_ref, sem):
    return pltpu.make_async_copy(src_ref, dst_ref, sem)


def _dispatch_kernel(dst_ref, np_ref, gap_ref, fill_ref, xn_ref, infoT_ref, xs_hbm, buf_ref, zero_ref, sem_ref):
    c = pl.program_id(0)
    n = pl.num_programs(0)
    slot = c % 2

    n_tiles = xs_hbm.shape[0] // EXPERT_TILE

    def gap_copy(g):
        return _piece_copy(zero_ref.at[pl.ds(0, PIECE)],
                           xs_hbm.at[pl.ds(pl.multiple_of(gap_ref[g] * PIECE, PIECE), PIECE)], sem_ref.at[2])

    def tail_copy(t):
        return _piece_copy(zero_ref, xs_hbm.at[pl.ds(pl.multiple_of(t * EXPERT_TILE, EXPERT_TILE), EXPERT_TILE)],
                           sem_ref.at[2])

    @pl.when(c == 0)
    def _():
        zero_ref[...] = jnp.zeros_like(zero_ref)
        lax.fori_loop(0, fill_ref[0], lambda g, carry: (gap_copy(g).start(), carry)[1], 0)
        lax.fori_loop(fill_ref[1], n_tiles, lambda t, carry: (tail_copy(t).start(), carry)[1], 0)

    def copy(cc, s, q):
        return _piece_copy(buf_ref.at[s, pl.ds(pl.multiple_of(q * PIECE, PIECE), PIECE)],
                           xs_hbm.at[pl.ds(pl.multiple_of(dst_ref[cc * MAX_PIECES + q] * PIECE, PIECE), PIECE)],
                           sem_ref.at[s])

    def start_all(cc, s):
        lax.fori_loop(0, np_ref[cc], lambda q, carry: (copy(cc, s, q).start(), carry)[1], 0)

    def wait_all(cc, s):
        lax.fori_loop(0, np_ref[cc], lambda q, carry: (copy(cc, s, q).wait(), carry)[1], 0)

    @pl.when(c >= 2)
    def _():
        wait_all(c - 2, slot)

    pos1 = infoT_ref[0:1, :]
    pos2 = infoT_ref[1:2, :]
    r = lax.broadcasted_iota(jnp.int32, (CHUNK_ROWS, MOE_CHUNK), 0).astype(F32)
    onehot = jnp.where((r == pos1) | (r == pos2), 1.0, 0.0).astype(BF16)
    buf_ref[slot] = _dot(onehot, xn_ref[...]).astype(BF16)
    start_all(c, slot)

    @pl.when(c == n - 1)
    def _():
        @pl.when(c >= 1)
        def _():
            wait_all(c - 1, 1 - slot)
        wait_all(c, slot)
        lax.fori_loop(0, fill_ref[0], lambda g, carry: (gap_copy(g).wait(), carry)[1], 0)
        lax.fori_loop(fill_ref[1], n_tiles, lambda t, carry: (tail_copy(t).wait(), carry)[1], 0)


def _dispatch(xn, infoT, dst, npc, gaps, fill, rows_max):
    T = xn.shape[0]
    grid_spec = pltpu.PrefetchScalarGridSpec(
        num_scalar_prefetch=4,
        grid=(T // MOE_CHUNK,),
        in_specs=[
            pl.BlockSpec((MOE_CHUNK, D_MODEL), lambda c, *_: (c, 0)),
            pl.BlockSpec((8, MOE_CHUNK), lambda c, *_: (0, c)),
        ],
        out_specs=pl.BlockSpec(memory_space=pl.ANY),
        scratch_shapes=[pltpu.VMEM((2, CHUNK_ROWS, D_MODEL), BF16), pltpu.VMEM((EXPERT_TILE, D_MODEL), BF16),
                        pltpu.SemaphoreType.DMA((3,))],
    )
    return pl.pallas_call(
        _dispatch_kernel,
        grid_spec=grid_spec,
        out_shape=jax.ShapeDtypeStruct((rows_max, D_MODEL), BF16),
        compiler_params=pltpu.CompilerParams(
            dimension_semantics=("arbitrary",), vmem_limit_bytes=VMEM_LIMIT),
        name="moe_dispatch",
    )(dst, npc, gaps, fill, xn, infoT)


def _expert_kernel(te_ref, tv_ref, nu_ref, x_ref, wgu_ref, wd_ref, y_ref):
    i = pl.program_id(0)

    @pl.when(i < nu_ref[0])
    def _():
        row = lax.broadcasted_iota(jnp.int32, x_ref.shape, 0)
        x = jnp.where(row < tv_ref[i], x_ref[...], jnp.zeros_like(x_ref))
        gu = _dot(x, wgu_ref[...])
        gate = gu[:, :EXPERT_FF]
        hid = gate * jax.nn.sigmoid(gate) * gu[:, EXPERT_FF:]
        y_ref[...] = _dot(hid.astype(BF16), wd_ref[...]).astype(BF16)

    @pl.when(i >= nu_ref[0])
    def _():
        y_ref[...] = jnp.zeros_like(y_ref)


def _experts(xs, wgu, wd, tile_expert, tile_valid, n_used):
    rows_max = xs.shape[0]
    last = lambda i, nu: jnp.minimum(i, nu[0] - 1)
    grid_spec = pltpu.PrefetchScalarGridSpec(
        num_scalar_prefetch=3,
        grid=(rows_max // EXPERT_TILE,),
        in_specs=[
            pl.BlockSpec((EXPERT_TILE, D_MODEL), lambda i, te, tv, nu: (last(i, nu), 0)),
            pl.BlockSpec((None, D_MODEL, 2 * EXPERT_FF), lambda i, te, tv, nu: (te[last(i, nu)], 0, 0)),
            pl.BlockSpec((None, EXPERT_FF, D_MODEL), lambda i, te, tv, nu: (te[last(i, nu)], 0, 0)),
        ],
        out_specs=pl.BlockSpec((EXPERT_TILE, D_MODEL), lambda i, te, tv, nu: (i, 0)),
    )
    return pl.pallas_call(
        _expert_kernel,
        grid_spec=grid_spec,
        out_shape=jax.ShapeDtypeStruct((rows_max, D_MODEL), BF16),
        compiler_params=pltpu.CompilerParams(
            dimension_semantics=("arbitrary",), vmem_limit_bytes=VMEM_LIMIT),
        name="moe_experts",
    )(tile_expert, tile_valid, n_used, xs, wgu, wd)


def _combine_kernel(dst_ref, np_ref, h_ref, info_ref, fg_ref, ys_hbm, o_ref, buf_ref, sem_ref):
    c = pl.program_id(0)
    n = pl.num_programs(0)
    slot = c % 2

    def copy(cc, s, q):
        return _piece_copy(ys_hbm.at[pl.ds(pl.multiple_of(dst_ref[cc * MAX_PIECES + q] * PIECE, PIECE), PIECE)],
                           buf_ref.at[s, pl.ds(pl.multiple_of(q * PIECE, PIECE), PIECE)],
                           sem_ref.at[s])

    def start_all(cc, s):
        lax.fori_loop(0, np_ref[cc], lambda q, carry: (copy(cc, s, q).start(), carry)[1], 0)

    def wait_all(cc, s):
        lax.fori_loop(0, np_ref[cc], lambda q, carry: (copy(cc, s, q).wait(), carry)[1], 0)

    @pl.when(c == 0)
    def _():
        buf_ref[...] = jnp.zeros_like(buf_ref)
        start_all(0, 0)

    @pl.when(c + 1 < n)
    def _():
        start_all(c + 1, 1 - slot)

    wait_all(c, slot)
    info = info_ref[...]
    r = lax.broadcasted_iota(jnp.int32, (MOE_CHUNK, CHUNK_ROWS), 1).astype(F32)
    weights = jnp.where(r == info[:, 0:1], info[:, 2:3], 0.0) + jnp.where(r == info[:, 1:2], info[:, 3:4], 0.0)
    moe = _dot(weights.astype(BF16), buf_ref[slot])
    o_ref[...] = _rms(h_ref[...] + moe, fg_ref[...])


def _combine(h2, info, fg, ys, dst, npc):
    T = h2.shape[0]
    grid_spec = pltpu.PrefetchScalarGridSpec(
        num_scalar_prefetch=2,
        grid=(T // MOE_CHUNK,),
        in_specs=[
            pl.BlockSpec((MOE_CHUNK, D_MODEL), lambda c, dst, npc: (c, 0)),
            pl.BlockSpec((MOE_CHUNK, LANES), lambda c, dst, npc: (c, 0)),
            pl.BlockSpec((1, D_MODEL), lambda c, dst, npc: (0, 0)),
            pl.BlockSpec(memory_space=pl.ANY),
        ],
        out_specs=pl.BlockSpec((MOE_CHUNK, D_MODEL), lambda c, dst, npc: (c, 0)),
        scratch_shapes=[pltpu.VMEM((2, CHUNK_ROWS, D_MODEL), BF16), pltpu.SemaphoreType.DMA((2,))],
    )
    return pl.pallas_call(
        _combine_kernel,
        grid_spec=grid_spec,
        out_shape=jax.ShapeDtypeStruct((T, D_MODEL), F32),
        compiler_params=pltpu.CompilerParams(
            dimension_semantics=("arbitrary",), vmem_limit_bytes=VMEM_LIMIT),
        name="moe_combine",
    )(dst, npc, h2, info, fg, ys)


def _routing_tables(pieces, rows_max):
    tile_pieces = EXPERT_TILE // PIECE
    total = jnp.sum(pieces, axis=0)
    total_al = (total + tile_pieces - 1) // tile_pieces * tile_pieces
    seg_end = jnp.cumsum(total_al)
    seg_start = seg_end - total_al
    chunk_off = jnp.cumsum(pieces, axis=0) - pieces
    loc_end = jnp.cumsum(pieces, axis=1)
    loc_start = loc_end - pieces
    q = jnp.arange(MAX_PIECES, dtype=jnp.int32)
    owner = jnp.sum((q[None, :, None] >= loc_end[:, None, :]).astype(jnp.int32), axis=-1)
    owner = jnp.minimum(owner, N_EXPERTS - 1)
    take = lambda a: jnp.take_along_axis(a, owner, axis=1)
    dst = seg_start[owner] + take(chunk_off) + (q[None, :] - take(loc_start))
    n_local = loc_end[:, -1]
    dst = jnp.where(q[None, :] < n_local[:, None], dst, 0)

    t0 = jnp.arange(rows_max // EXPERT_TILE, dtype=jnp.int32) * tile_pieces
    tile_expert = jnp.minimum(jnp.sum((t0[:, None] >= seg_end[None, :]).astype(jnp.int32), axis=-1), N_EXPERTS - 1)
    tile_valid = jnp.clip((seg_start[tile_expert] + total[tile_expert] - t0) * PIECE, 0, EXPERT_TILE)
    n_used = (seg_end[-1] // tile_pieces).reshape(1)

    k = jnp.arange(tile_pieces, dtype=jnp.int32)
    is_gap = (k[None, :] < (total_al - total)[:, None]).reshape(-1)
    gap_piece = (seg_start + total)[:, None] + k[None, :]
    order = jnp.argsort(jnp.logical_not(is_gap), stable=True)
    gaps = gap_piece.reshape(-1)[order]
    fill = jnp.stack([jnp.sum(is_gap.astype(jnp.int32)), n_used[0]])
    i32 = lambda a: a.astype(jnp.int32)
    return i32(dst.reshape(-1)), i32(n_local), i32(tile_expert), i32(tile_valid), i32(n_used), i32(gaps), i32(fill)


def _rope_tables(positions):
    inv_freq = 1.0 / (ROPE_THETA ** (jnp.arange(0, ROPE_DIM, 2, dtype=F32) / ROPE_DIM))
    ang = positions.astype(F32)[..., None] * inv_freq
    cos, sin = jnp.cos(ang), jnp.sin(ang)
    half = ROPE_DIM // 2
    z = lambda n: jnp.zeros(cos.shape[:-1] + (n,), F32)
    ctab = jnp.concatenate([jnp.ones(cos.shape[:-1] + (NOPE_DIM,), F32), cos, cos, z(LANES - NOPE_DIM - ROPE_DIM)], -1)
    satab = jnp.concatenate([z(NOPE_DIM + half), sin, z(LANES - NOPE_DIM - ROPE_DIM)], -1)
    sbtab = jnp.concatenate([z(NOPE_DIM), -sin, z(half + LANES - NOPE_DIM - ROPE_DIM)], -1)
    return ctab, satab, sbtab


def _pad_heads(w, heads, width):
    k = w.shape[0]
    w = w.reshape(k, heads, width)
    w = jnp.pad(w, ((0, 0), (0, 0), (0, HEAD_PAD - width)))
    return w.reshape(k, heads * HEAD_PAD)


def _layer(l, h, mem, tables, mix_norm_g, w_in, q_norm_g, w_q_up, kv_norm_g, w_kv_up, w_attn_branch,
           pool_w, pool_scale, w_pool_branch, w_mix_out, xattn_norm_g, mem_norm_g, w_xq, w_xkv, w_xo,
           ffn_norm_g, w_router_group, b_router_group, w_router_expert, b_router_expert,
           w_exp_gate, w_exp_up, w_exp_down, out_g, tm_proj):
    B, S, _ = h.shape
    row2 = lambda v: v.reshape(1, -1).astype(F32)

    wi = w_in[l]
    kr_cols = jnp.pad(wi[:, Q_LORA + KV_LORA:Q_LORA + KV_LORA + ROPE_DIM],
                      ((0, 0), (NOPE_DIM, LANES - NOPE_DIM - ROPE_DIM)))
    win = jnp.concatenate([wi[:, :Q_LORA + KV_LORA], kr_cols, wi[:, Q_LORA + KV_LORA + ROPE_DIM:]], axis=1).astype(BF16)
    scale = math.log2(math.e) / math.sqrt(NOPE_DIM + ROPE_DIM)
    wq = _pad_heads(w_q_up[l] * scale, MLA_HEADS, NOPE_DIM + ROPE_DIM).astype(BF16)
    wkv3 = w_kv_up[l].reshape(KV_LORA, MLA_HEADS, NOPE_DIM + V_DIM)
    wkv = jnp.concatenate([
        _pad_heads(wkv3[:, :, :NOPE_DIM].reshape(KV_LORA, -1), MLA_HEADS, NOPE_DIM),
        wkv3[:, :, NOPE_DIM:].reshape(KV_LORA, -1)], axis=1).astype(BF16)

    memkv = _mem_kv(mem.reshape(-1, D_MODEL), row2(mem_norm_g[l]), w_xkv[l].astype(BF16))
    memkv = memkv.reshape(B, -1, 2 * D_MODEL)

    qT, k, vT, ga, gyb = _in_proj(
        h, *tables, row2(mix_norm_g[l]), win, row2(q_norm_g[l]), wq, row2(kv_norm_g[l]), wkv,
        pool_w[l].astype(BF16), row2(pool_scale[l]), w_pool_branch[l].astype(BF16), tm_proj)
    attn = _mla_attention(qT, k, vT)

    w_r = jnp.zeros((D_MODEL, 2 * LANES), F32)
    w_r = w_r.at[:, :N_EXPERTS].set(w_router_expert[l]).at[:, LANES:LANES + N_GROUPS].set(w_router_group[l])
    wr_hi = w_r.astype(BF16)
    wr_lo = (w_r - wr_hi.astype(F32)).astype(BF16)
    rb = jnp.zeros((1, 2 * LANES), F32)
    rb = rb.at[0, :N_EXPERTS].set(b_router_expert[l]).at[0, LANES:LANES + N_GROUPS].set(b_router_group[l])

    h2, xn, info, infoT, pieces = _mix_xattn(
        h, attn, ga, gyb, memkv, w_attn_branch[l].astype(BF16), w_mix_out[l].astype(BF16),
        row2(xattn_norm_g[l]), (w_xq[l] * (1.0 / math.sqrt(MEM_HEAD_DIM))).astype(BF16), w_xo[l].astype(BF16),
        row2(ffn_norm_g[l]), wr_hi, wr_lo, rb)

    T = B * S
    n_chunks = T // MOE_CHUNK
    tile_pieces = EXPERT_TILE // PIECE
    max_pieces = 2 * T // PIECE + n_chunks * N_EXPERTS + N_EXPERTS * tile_pieces
    rows_max = -(-max_pieces // tile_pieces) * EXPERT_TILE
    dst, n_local, tile_expert, tile_valid, n_used, gaps, fill = _routing_tables(
        pieces[:, 0, :N_EXPERTS].astype(jnp.int32), rows_max)

    wgu = jnp.concatenate([w_exp_gate[l], w_exp_up[l]], axis=-1).astype(BF16)
    xs = _dispatch(xn.reshape(T, D_MODEL), infoT, dst, n_local, gaps, fill, rows_max)
    ys = _experts(xs, wgu, w_exp_down[l].astype(BF16), tile_expert, tile_valid, n_used)
    out = _combine(h2.reshape(T, D_MODEL), info.reshape(T, LANES), row2(out_g), ys, dst, n_local)
    return out.reshape(B, S, D_MODEL)


def kernel(x, mem, positions, mix_norm_g, w_in, q_norm_g, w_q_up, kv_norm_g, w_kv_up, w_attn_branch, pool_w, pool_scale, w_pool_branch, w_mix_out, xattn_norm_g, mem_norm_g, w_xq, w_xkv, w_xo, ffn_norm_g, w_router_group, b_router_group, w_router_expert, b_router_expert, w_exp_gate, w_exp_up, w_exp_down, final_norm_g):
    depth = w_in.shape[0]
    assert depth == 1, "the combine kernel fuses the final RMSNorm, which is only valid after the last layer"
    assert x.shape[1] % Q_TILE == 0 and x.shape[1] % MOE_CHUNK == 0
    tables = _rope_tables(positions)
    return _layer(0, x, mem, tables, mix_norm_g, w_in, q_norm_g, w_q_up, kv_norm_g, w_kv_up, w_attn_branch,
                  pool_w, pool_scale, w_pool_branch, w_mix_out, xattn_norm_g, mem_norm_g, w_xq, w_xkv, w_xo,
                  ffn_norm_g, w_router_group, b_router_group, w_router_expert, b_router_expert,
                  w_exp_gate, w_exp_up, w_exp_down, final_norm_g, IN_PROJ_TILE)
```

```python
import functools
import math

import jax
import jax.numpy as jnp
from jax import lax
from jax.experimental import pallas as pl
from jax.experimental.pallas import tpu as pltpu

F32 = jnp.float32
BF16 = jnp.bfloat16

D_MODEL = 1024
CHUNK = 64
MLA_HEADS = 8
Q_LORA = 384
KV_LORA = 256
NOPE_DIM = 64
ROPE_DIM = 32
V_DIM = 64
ROPE_THETA = 10000.0
POOL_WIDTH = 512
POOL_WINDOWS = (2, 4, 8, 16)
POOL_GROUP_DIM = POOL_WIDTH // len(POOL_WINDOWS)
POOL_HALO = 16
MEM_HEADS = 4
MEM_HEAD_DIM = D_MODEL // MEM_HEADS
N_GROUPS = 4
EXPERTS_PER_GROUP = 8
N_EXPERTS = N_GROUPS * EXPERTS_PER_GROUP
EXPERT_FF = 256
EPS = 1e-6

LANES = 128
HEAD_PAD = LANES
Q_TILE = 512
KV_TILE = 256
KEY_SUB = 128
V_ROWS = V_DIM + 16
IN_PROJ_TILE = 512
MOE_CHUNK = 512
PIECE = 16
MAX_PIECES = 2 * MOE_CHUNK // PIECE + N_EXPERTS
CHUNK_ROWS = MAX_PIECES * PIECE
EXPERT_TILE = 512
EXPERT_SUB = 256
MIX_SUB = 256
QK_AHEAD = 3
PV_BEHIND = 2

_C_Q = 0
_C_KV = _C_Q + Q_LORA
_C_KR = _C_KV + KV_LORA
_C_POOL = _C_KR + LANES
_C_GA = _C_POOL + POOL_WIDTH
_C_GB = _C_GA + D_MODEL
_C_END = _C_GB + D_MODEL

VMEM_LIMIT = 56 * 1024 * 1024


def _rms(x, g):
    return x * lax.rsqrt(jnp.mean(x * x, axis=-1, keepdims=True) + EPS) * g


def _dot(a, b):
    return jnp.dot(a, b, preferred_element_type=F32)


def _dot_nt(a, b):
    return lax.dot_general(a, b, (((1,), (1,)), ((), ())), preferred_element_type=F32)


def _mem_kv_kernel(mem_ref, g_ref, w_ref, kv_ref):
    mn = _rms(mem_ref[...], g_ref[...]).astype(BF16)
    kv_ref[...] = _dot(mn, w_ref[...]).astype(BF16)


def _mem_kv(mem2d, g, w_xkv):
    rows = mem2d.shape[0]
    tm = min(512, rows)
    assert rows % tm == 0
    return pl.pallas_call(
        _mem_kv_kernel,
        grid=(rows // tm,),
        in_specs=[
            pl.BlockSpec((tm, D_MODEL), lambda i: (i, 0)),
            pl.BlockSpec((1, D_MODEL), lambda i: (0, 0)),
            pl.BlockSpec((D_MODEL, 2 * D_MODEL), lambda i: (0, 0)),
        ],
        out_specs=pl.BlockSpec((tm, 2 * D_MODEL), lambda i: (i, 0)),
        out_shape=jax.ShapeDtypeStruct((rows, 2 * D_MODEL), BF16),
        compiler_params=pltpu.CompilerParams(vmem_limit_bytes=VMEM_LIMIT),
        name="mem_kv",
    )(mem2d, g, w_xkv)


def _rope(t, c, sa, sb):
    w = t.shape[-1]
    return t * c + pltpu.roll(t, ROPE_DIM // 2, 1) * sa + pltpu.roll(t, w - ROPE_DIM // 2, 1) * sb


def _in_proj_kernel(x_ref, c_ref, sa_ref, sb_ref, g_ref, win_ref, qg_ref, wq_ref, kvg_ref, wkv_ref,
                    poolw_ref, pscale_ref, wpb_ref,
                    qT_out, k_out, vT_out, ga_out, gyb_out, hist_ref):
    tm = x_ref.shape[0]
    i = pl.program_id(1)
    hn = _rms(x_ref[...], g_ref[...]).astype(BF16)

    c1 = c_ref[...]
    sa1 = sa_ref[...]
    sb1 = sb_ref[...]
    c8 = jnp.tile(c1, (1, MLA_HEADS))
    sa8 = jnp.tile(sa1, (1, MLA_HEADS))
    sb8 = jnp.tile(sb1, (1, MLA_HEADS))

    q_lat = _dot(hn, win_ref[:, _C_Q:_C_KV])
    qn = _rms(q_lat, qg_ref[...]).astype(BF16)
    q = _rope(_dot(qn, wq_ref[...]), c8, sa8, sb8)

    kv_lat = _dot(hn, win_ref[:, _C_KV:_C_KR])
    kvn = _rms(kv_lat, kvg_ref[...]).astype(BF16)
    k_nope = _dot(kvn, wkv_ref[:, 0:MLA_HEADS * HEAD_PAD])
    kr = _rope(_dot(hn, win_ref[:, _C_KR:_C_POOL]), c1, sa1, sb1)
    k_out[...] = (k_nope + jnp.tile(kr, (1, MLA_HEADS))).astype(BF16)
    v = _dot(kvn, wkv_ref[:, MLA_HEADS * HEAD_PAD:])
    for t in range(tm // Q_TILE):
        qT_out[t] = q[t * Q_TILE:(t + 1) * Q_TILE, :].T.astype(BF16)
    vT = v.T
    ones = jnp.ones((V_ROWS - V_DIM, tm), F32)
    vT = jnp.concatenate(
        [blk for h in range(MLA_HEADS) for blk in (vT[h * V_DIM:(h + 1) * V_DIM, :], ones)], axis=0).astype(BF16)
    for t in range(tm // KV_TILE):
        vT_out[t] = vT[:, t * KV_TILE:(t + 1) * KV_TILE]

    u = _dot(hn, win_ref[:, _C_POOL:_C_GA])

    @pl.when(i == 0)
    def _():
        hist_ref[...] = jnp.zeros_like(hist_ref)

    ext = jnp.concatenate([hist_ref[...], u], axis=0)
    hist_ref[...] = u[tm - POOL_HALO:, :]
    t_idx = i * tm + lax.broadcasted_iota(jnp.int32, (tm, POOL_GROUP_DIM), 0)
    ys = []
    for g, w in enumerate(POOL_WINDOWS):
        c0 = g * POOL_GROUP_DIM
        run = ext[:, c0:c0 + POOL_GROUP_DIM]
        span = 1
        while span < w:
            run = run + pltpu.roll(run, span, 0)
            span *= 2
        cnt = jnp.minimum(t_idx + 1, w).astype(F32)
        d = run[POOL_HALO:, :] / cnt - u[:, c0:c0 + POOL_GROUP_DIM]
        ys.append(_dot(d.astype(BF16), poolw_ref[g]))
    y = (jnp.concatenate(ys, axis=1) * pscale_ref[...]).astype(BF16)
    y_b = _dot(y, wpb_ref[...])

    g_a = jax.nn.sigmoid(_dot(hn, win_ref[:, _C_GA:_C_GB]))
    ga_out[...] = g_a.astype(BF16)
    g_b = jax.nn.sigmoid(_dot(hn, win_ref[:, _C_GB:_C_END]))
    gyb_out[...] = (g_b * y_b).astype(BF16)


def _in_proj(x, ctab, satab, sbtab, g, win, qg, wq, kvg, wkv, poolw, pscale, wpb, tm):
    B, S, _ = x.shape
    row = lambda b, i: (b, i, 0)
    const2 = lambda b, i: (0, 0)
    const3 = lambda b, i: (0, 0, 0)
    slab = lambda b, i: (b, i, 0, 0)
    return pl.pallas_call(
        _in_proj_kernel,
        grid=(B, S // tm),
        in_specs=[
            pl.BlockSpec((None, tm, D_MODEL), row),
            pl.BlockSpec((None, tm, LANES), row),
            pl.BlockSpec((None, tm, LANES), row),
            pl.BlockSpec((None, tm, LANES), row),
            pl.BlockSpec((1, D_MODEL), const2),
            pl.BlockSpec(win.shape, const2),
            pl.BlockSpec((1, Q_LORA), const2),
            pl.BlockSpec(wq.shape, const2),
            pl.BlockSpec((1, KV_LORA), const2),
            pl.BlockSpec(wkv.shape, const2),
            pl.BlockSpec(poolw.shape, const3),
            pl.BlockSpec((1, POOL_WIDTH), const2),
            pl.BlockSpec(wpb.shape, const2),
        ],
        out_specs=[
            pl.BlockSpec((None, tm // Q_TILE, MLA_HEADS * HEAD_PAD, Q_TILE), slab),
            pl.BlockSpec((None, tm, MLA_HEADS * HEAD_PAD), row),
            pl.BlockSpec((None, tm // KV_TILE, MLA_HEADS * V_ROWS, KV_TILE), slab),
            pl.BlockSpec((None, tm, D_MODEL), row),
            pl.BlockSpec((None, tm, D_MODEL), row),
        ],
        out_shape=[jax.ShapeDtypeStruct((B, S // Q_TILE, MLA_HEADS * HEAD_PAD, Q_TILE), BF16),
                   jax.ShapeDtypeStruct((B, S, MLA_HEADS * HEAD_PAD), BF16),
                   jax.ShapeDtypeStruct((B, S // KV_TILE, MLA_HEADS * V_ROWS, KV_TILE), BF16),
                   jax.ShapeDtypeStruct((B, S, D_MODEL), BF16),
                   jax.ShapeDtypeStruct((B, S, D_MODEL), BF16)],
        scratch_shapes=[pltpu.VMEM((POOL_HALO, POOL_WIDTH), F32)],
        compiler_params=pltpu.CompilerParams(
            dimension_semantics=("arbitrary", "arbitrary"), vmem_limit_bytes=VMEM_LIMIT),
        name="in_proj",
    )(x, ctab, satab, sbtab, g, win, qg, wq, kvg, wkv, poolw, pscale, wpb)


def _mla_kernel(qT_ref, k_ref, vT_ref, o_ref, m_ref, acc_ref):
    i = pl.program_id(1)
    n_sub = KV_TILE // KEY_SUB
    units = [(h, c) for c in range(n_sub) for h in range(MLA_HEADS)]
    qry_c = lax.broadcasted_iota(jnp.int32, (KEY_SUB, Q_TILE), 1) // CHUNK

    def scores(j, h, c):
        hs = slice(h * HEAD_PAD, (h + 1) * HEAD_PAD)
        rows = pl.ds(pl.multiple_of(j * KV_TILE + c * KEY_SUB, KEY_SUB), KEY_SUB)
        return _dot(k_ref[rows, hs], qT_ref[hs, :])

    def fold(h, alpha, pv):
        acc_ref[h] = pv if alpha is None else alpha * acc_ref[h] + pv

    def sweep(j, diag):
        ahead = [scores(j, *u) for u in units[:QK_AHEAD]]
        pending = []
        for n, (h, c) in enumerate(units):
            s = ahead.pop(0)
            if n + QK_AHEAD < len(units):
                ahead.append(scores(j, *units[n + QK_AHEAD]))
            first = False
            if diag is not None:
                key_c0 = (diag * KV_TILE + c * KEY_SUB) // CHUNK
                key_c = key_c0 + lax.broadcasted_iota(jnp.int32, (KEY_SUB, Q_TILE), 0) // CHUNK
                s = jnp.where(key_c <= qry_c, s, -jnp.inf)
                first = diag == 0 and c == 0
            s_max = jnp.max(s, axis=0, keepdims=True)
            m_new = s_max if first else jnp.maximum(m_ref[h], s_max)
            p = jnp.exp2(s - m_new).astype(BF16)
            pv = _dot(vT_ref[j, h * V_ROWS:(h + 1) * V_ROWS, c * KEY_SUB:(c + 1) * KEY_SUB], p)
            alpha = None if first else jnp.exp2(m_ref[h] - m_new)
            m_ref[h] = m_new
            pending.append((h, alpha, pv))
            if len(pending) > PV_BEHIND:
                fold(*pending.pop(0))
        for item in pending:
            fold(*item)

    n_diag = Q_TILE // KV_TILE
    for d in range(n_diag):
        sweep(i * n_diag + d, d)

    def body(j, carry):
        sweep(j, None)
        return carry

    lax.fori_loop(0, i * n_diag, body, 0)
    oT = jnp.concatenate([acc_ref[h, :V_DIM, :] / acc_ref[h, V_DIM:V_DIM + 1, :] for h in range(MLA_HEADS)], axis=0)
    o_ref[...] = oT.T.astype(BF16)


def _mla_attention(qT, k, vT):
    B, S, W = k.shape
    return pl.pallas_call(
        _mla_kernel,
        grid=(B, S // Q_TILE),
        in_specs=[
            pl.BlockSpec((None, None, W, Q_TILE), lambda b, i: (b, i, 0, 0)),
            pl.BlockSpec((None, S, W), lambda b, i: (b, 0, 0)),
            pl.BlockSpec((None, S // KV_TILE, MLA_HEADS * V_ROWS, KV_TILE), lambda b, i: (b, 0, 0, 0)),
        ],
        out_specs=pl.BlockSpec((None, Q_TILE, MLA_HEADS * V_DIM), lambda b, i: (b, i, 0)),
        out_shape=jax.ShapeDtypeStruct((B, S, MLA_HEADS * V_DIM), BF16),
        scratch_shapes=[pltpu.VMEM((MLA_HEADS, 1, Q_TILE), F32),
                        pltpu.VMEM((MLA_HEADS, V_ROWS, Q_TILE), F32)],
        compiler_params=pltpu.CompilerParams(
            dimension_semantics=("arbitrary", "arbitrary"), vmem_limit_bytes=VMEM_LIMIT),
        name="mla_attn",
    )(qT, k, vT)


def _route(logits_t, bias_t, tri_upper, tri_lower):
    tm = logits_t.shape[1]
    neg = -jnp.inf
    rg = lax.broadcasted_iota(jnp.int32, (8, tm), 0)
    re = lax.broadcasted_iota(jnp.int32, (N_EXPERTS, tm), 0)
    top = lambda v: jnp.max(v, axis=0, keepdims=True)

    lg = jnp.where(rg < N_GROUPS, logits_t[LANES:LANES + 8, :] + bias_t[LANES:LANES + 8, :], neg)
    ge = jnp.exp(lg - top(lg))
    gp = ge / jnp.sum(ge, axis=0, keepdims=True)
    g_w = top(gp)
    g_idx = jnp.min(jnp.where(gp == g_w, rg, 8), axis=0, keepdims=True)

    sel = re // EXPERTS_PER_GROUP == g_idx
    le = jnp.where(sel, logits_t[:N_EXPERTS, :] + bias_t[:N_EXPERTS, :], neg)
    ee = jnp.exp(le - top(le))
    ep = jnp.where(sel, ee / jnp.sum(ee, axis=0, keepdims=True), -1.0)
    w1 = top(ep)
    i1 = jnp.min(jnp.where(ep == w1, re, N_EXPERTS), axis=0, keepdims=True)
    ep2 = jnp.where(re == i1, -1.0, ep)
    w2 = top(ep2)
    i2 = jnp.min(jnp.where(ep2 == w2, re, N_EXPERTS), axis=0, keepdims=True)
    den = w1 + w2
    c1 = g_w * (w1 / den)
    c2 = g_w * (w2 / den)

    oh1 = (re == i1).astype(F32)
    oh2 = (re == i2).astype(F32)
    both = oh1 + oh2
    earlier = _dot(both.astype(BF16), tri_upper)
    pieces = jnp.floor((jnp.sum(both, axis=1, keepdims=True) + (PIECE - 1)) * (1.0 / PIECE))
    start = _dot(tri_lower, jnp.broadcast_to(pieces, (N_EXPERTS, LANES)).astype(BF16))[:, 0:1] * PIECE
    pos1 = jnp.sum(oh1 * (earlier + start), axis=0, keepdims=True)
    pos2 = jnp.sum(oh2 * (earlier + start), axis=0, keepdims=True)
    info_t = jnp.concatenate([pos1, pos2, c1, c2, jnp.zeros((4, tm), F32)], axis=0)
    return info_t, pieces


def _mix_kernel(x_ref, attn_ref, ga_ref, gyb_ref, kv_ref, wab_ref, wmix_ref, xg_ref, wxq_ref, wxo_ref,
                fg_ref, wr_hi_ref, wr_lo_ref, rb_ref, tri_ref, lower_ref,
                h_out, xn_out, info_out, infoT_out, pieces_out):
    tm = x_ref.shape[0]
    blocks = [slice(b * MIX_SUB, (b + 1) * MIX_SUB) for b in range(tm // MIX_SUB)]
    y_a = [_dot(attn_ref[r, :], wab_ref[...]) for r in blocks]
    merged = [(ga_ref[r, :].astype(F32) * y + gyb_ref[r, :].astype(F32)).astype(BF16) for r, y in zip(blocks, y_a)]
    h1 = [x_ref[r, :] + _dot(m, wmix_ref[...]) for r, m in zip(blocks, merged)]

    hn = [_rms(h, xg_ref[...]).astype(BF16) for h in h1]
    q = [_dot(v, wxq_ref[...]).astype(BF16) for v in hn]
    heads = [[] for _ in blocks]
    for h in range(MEM_HEADS):
        hs = slice(h * MEM_HEAD_DIM, (h + 1) * MEM_HEAD_DIM)
        vs = slice(D_MODEL + h * MEM_HEAD_DIM, D_MODEL + (h + 1) * MEM_HEAD_DIM)
        s = [_dot_nt(v[:, hs], kv_ref[:, hs]) for v in q]
        p = [jnp.exp(v - jnp.max(v, axis=-1, keepdims=True)) for v in s]
        o = [_dot(v.astype(BF16), kv_ref[:, vs]) for v in p]
        for b in range(len(blocks)):
            heads[b].append((o[b] / jnp.sum(p[b], axis=-1, keepdims=True)).astype(BF16))
    h2 = [h + _dot(jnp.concatenate(hd, axis=1), wxo_ref[...]) for h, hd in zip(h1, heads)]

    xn = [_rms(h, fg_ref[...]) for h in h2]
    xn_hi = [v.astype(BF16) for v in xn]
    xn_lo = [(v - hi.astype(F32)).astype(BF16) for v, hi in zip(xn, xn_hi)]
    logits = [_dot(hi, wr_hi_ref[...]) + (_dot(hi, wr_lo_ref[...]) + _dot(lo, wr_hi_ref[...]))
              for hi, lo in zip(xn_hi, xn_lo)]
    for r, h, hi in zip(blocks, h2, xn_hi):
        h_out[r, :] = h
        xn_out[r, :] = hi
    logits_t = jnp.concatenate(logits, axis=0).T
    info_t, pieces = _route(logits_t, rb_ref[...], tri_ref[...], lower_ref[...])
    infoT_out[...] = info_t
    info_out[...] = jnp.concatenate([info_t, jnp.zeros((LANES - 8, tm), F32)], axis=0).T
    pieces_out[...] = jnp.broadcast_to(pieces, (N_EXPERTS, LANES))


def _mix_xattn(x, attn, ga, gyb, memkv, wab, wmix, xg, wxq, wxo, fg, wr_hi, wr_lo, rb):
    B, S, _ = x.shape
    M = memkv.shape[1]
    tm = MOE_CHUNK
    nt = S // tm
    row = lambda b, i: (b, i, 0)
    const2 = lambda b, i: (0, 0)
    tri = (lax.broadcasted_iota(jnp.int32, (tm, tm), 0) < lax.broadcasted_iota(jnp.int32, (tm, tm), 1)).astype(BF16)
    lower = (lax.broadcasted_iota(jnp.int32, (N_EXPERTS, N_EXPERTS), 1)
             < lax.broadcasted_iota(jnp.int32, (N_EXPERTS, N_EXPERTS), 0)).astype(BF16)
    rb = jnp.broadcast_to(rb.reshape(2 * LANES, 1), (2 * LANES, tm))
    return pl.pallas_call(
        _mix_kernel,
        grid=(B, S // tm),
        in_specs=[
            pl.BlockSpec((None, tm, D_MODEL), row),
            pl.BlockSpec((None, tm, MLA_HEADS * V_DIM), row),
            pl.BlockSpec((None, tm, D_MODEL), row),
            pl.BlockSpec((None, tm, D_MODEL), row),
            pl.BlockSpec((None, M, 2 * D_MODEL), lambda b, i: (b, 0, 0)),
            pl.BlockSpec(wab.shape, const2),
            pl.BlockSpec(wmix.shape, const2),
            pl.BlockSpec((1, D_MODEL), const2),
            pl.BlockSpec(wxq.shape, const2),
            pl.BlockSpec(wxo.shape, const2),
            pl.BlockSpec((1, D_MODEL), const2),
            pl.BlockSpec(wr_hi.shape, const2),
            pl.BlockSpec(wr_lo.shape, const2),
            pl.BlockSpec((2 * LANES, tm), const2),
            pl.BlockSpec((tm, tm), const2),
            pl.BlockSpec((N_EXPERTS, N_EXPERTS), const2),
        ],
        out_specs=[
            pl.BlockSpec((None, tm, D_MODEL), row),
            pl.BlockSpec((None, tm, D_MODEL), row),
            pl.BlockSpec((None, tm, LANES), row),
            pl.BlockSpec((8, tm), lambda b, i: (0, b * nt + i)),
            pl.BlockSpec((None, N_EXPERTS, LANES), lambda b, i: (b * nt + i, 0, 0)),
        ],
        out_shape=[jax.ShapeDtypeStruct((B, S, D_MODEL), F32),
                   jax.ShapeDtypeStruct((B, S, D_MODEL), BF16),
                   jax.ShapeDtypeStruct((B, S, LANES), F32),
                   jax.ShapeDtypeStruct((8, B * S), F32),
                   jax.ShapeDtypeStruct((B * nt, N_EXPERTS, LANES), F32)],
        compiler_params=pltpu.CompilerParams(
            dimension_semantics=("arbitrary", "arbitrary"), vmem_limit_bytes=VMEM_LIMIT),
        name="mix_xattn",
    )(x, attn, ga, gyb, memkv, wab, wmix, xg, wxq, wxo, fg, wr_hi, wr_lo, rb, tri, lower)


def _piece_copy(src_ref, dst_ref, sem):
    return pltpu.make_async_copy(src_ref, dst_ref, sem)


def _dispatch_kernel(dst_ref, np_ref, gap_ref, fill_ref, xn_ref, infoT_ref, xs_hbm, buf_ref, zero_ref, sem_ref):
    c = pl.program_id(0)
    n = pl.num_programs(0)
    slot = c % 2

    n_tiles = xs_hbm.shape[0] // EXPERT_TILE

    def gap_copy(g):
        return _piece_copy(zero_ref.at[pl.ds(0, PIECE)],
                           xs_hbm.at[pl.ds(pl.multiple_of(gap_ref[g] * PIECE, PIECE), PIECE)], sem_ref.at[2])

    def tail_copy(t):
        return _piece_copy(zero_ref, xs_hbm.at[pl.ds(pl.multiple_of(t * EXPERT_TILE, EXPERT_TILE), EXPERT_TILE)],
                           sem_ref.at[2])

    @pl.when(c == 0)
    def _():
        zero_ref[...] = jnp.zeros_like(zero_ref)
        lax.fori_loop(0, fill_ref[0], lambda g, carry: (gap_copy(g).start(), carry)[1], 0)
        lax.fori_loop(fill_ref[1], n_tiles, lambda t, carry: (tail_copy(t).start(), carry)[1], 0)

    def copy(cc, s, q):
        return _piece_copy(buf_ref.at[s, pl.ds(pl.multiple_of(q * PIECE, PIECE), PIECE)],
                           xs_hbm.at[pl.ds(pl.multiple_of(dst_ref[cc * MAX_PIECES + q] * PIECE, PIECE), PIECE)],
                           sem_ref.at[s])

    def start_all(cc, s):
        lax.fori_loop(0, np_ref[cc], lambda q, carry: (copy(cc, s, q).start(), carry)[1], 0)

    def wait_all(cc, s):
        lax.fori_loop(0, np_ref[cc], lambda q, carry: (copy(cc, s, q).wait(), carry)[1], 0)

    @pl.when(c >= 2)
    def _():
        wait_all(c - 2, slot)

    pos1 = infoT_ref[0:1, :]
    pos2 = infoT_ref[1:2, :]
    r = lax.broadcasted_iota(jnp.int32, (CHUNK_ROWS, MOE_CHUNK), 0).astype(F32)
    onehot = jnp.where((r == pos1) | (r == pos2), 1.0, 0.0).astype(BF16)
    buf_ref[slot] = _dot(onehot, xn_ref[...]).astype(BF16)
    start_all(c, slot)

    @pl.when(c == n - 1)
    def _():
        @pl.when(c >= 1)
        def _():
            wait_all(c - 1, 1 - slot)
        wait_all(c, slot)
        lax.fori_loop(0, fill_ref[0], lambda g, carry: (gap_copy(g).wait(), carry)[1], 0)
        lax.fori_loop(fill_ref[1], n_tiles, lambda t, carry: (tail_copy(t).wait(), carry)[1], 0)


def _dispatch(xn, infoT, dst, npc, gaps, fill, rows_max):
    T = xn.shape[0]
    grid_spec = pltpu.PrefetchScalarGridSpec(
        num_scalar_prefetch=4,
        grid=(T // MOE_CHUNK,),
        in_specs=[
            pl.BlockSpec((MOE_CHUNK, D_MODEL), lambda c, *_: (c, 0)),
            pl.BlockSpec((8, MOE_CHUNK), lambda c, *_: (0, c)),
        ],
        out_specs=pl.BlockSpec(memory_space=pl.ANY),
        scratch_shapes=[pltpu.VMEM((2, CHUNK_ROWS, D_MODEL), BF16), pltpu.VMEM((EXPERT_TILE, D_MODEL), BF16),
                        pltpu.SemaphoreType.DMA((3,))],
    )
    return pl.pallas_call(
        _dispatch_kernel,
        grid_spec=grid_spec,
        out_shape=jax.ShapeDtypeStruct((rows_max, D_MODEL), BF16),
        compiler_params=pltpu.CompilerParams(
            dimension_semantics=("arbitrary",), vmem_limit_bytes=VMEM_LIMIT),
        name="moe_dispatch",
    )(dst, npc, gaps, fill, xn, infoT)


def _expert_kernel(te_ref, tv_ref, nu_ref, x_ref, wg_ref, wu_ref, wd_ref, y_ref, wgu_bf, wd_bf):
    i = pl.program_id(0)

    @pl.when(i < nu_ref[0])
    def _():
        @pl.when((i == 0) | (te_ref[i] != te_ref[jnp.maximum(i - 1, 0)]))
        def _():
            wgu_bf[:, :EXPERT_FF] = wg_ref[...].astype(BF16)
            wgu_bf[:, EXPERT_FF:] = wu_ref[...].astype(BF16)
            wd_bf[...] = wd_ref[...].astype(BF16)

        blocks = [slice(b * EXPERT_SUB, (b + 1) * EXPERT_SUB) for b in range(EXPERT_TILE // EXPERT_SUB)]
        row = lax.broadcasted_iota(jnp.int32, (EXPERT_SUB, D_MODEL), 0)
        gus = []
        for b, rows in enumerate(blocks):
            x = x_ref[rows, :]
            x = jnp.where(row < tv_ref[i] - b * EXPERT_SUB, x, jnp.zeros_like(x))
            gus.append(_dot(x, wgu_bf[...]))
        hids = []
        for gu in gus:
            gate = gu[:, :EXPERT_FF]
            hids.append((gate * jax.nn.sigmoid(gate) * gu[:, EXPERT_FF:]).astype(BF16))
        for rows, hid in zip(blocks, hids):
            y_ref[rows, :] = _dot(hid, wd_bf[...]).astype(BF16)

    @pl.when(i >= nu_ref[0])
    def _():
        y_ref[...] = jnp.zeros_like(y_ref)


def _experts(xs, w_gate, w_up, w_down, tile_expert, tile_valid, n_used):
    rows_max = xs.shape[0]
    last = lambda i, nu: jnp.minimum(i, nu[0] - 1)
    expert = lambda i, te, tv, nu: (te[last(i, nu)], 0, 0)
    grid_spec = pltpu.PrefetchScalarGridSpec(
        num_scalar_prefetch=3,
        grid=(rows_max // EXPERT_TILE,),
        in_specs=[
            pl.BlockSpec((EXPERT_TILE, D_MODEL), lambda i, te, tv, nu: (last(i, nu), 0)),
            pl.BlockSpec((None, D_MODEL, EXPERT_FF), expert),
            pl.BlockSpec((None, D_MODEL, EXPERT_FF), expert),
            pl.BlockSpec((None, EXPERT_FF, D_MODEL), expert),
        ],
        out_specs=pl.BlockSpec((EXPERT_TILE, D_MODEL), lambda i, te, tv, nu: (i, 0)),
        scratch_shapes=[pltpu.VMEM((D_MODEL, 2 * EXPERT_FF), BF16), pltpu.VMEM((EXPERT_FF, D_MODEL), BF16)],
    )
    return pl.pallas_call(
        _expert_kernel,
        grid_spec=grid_spec,
        out_shape=jax.ShapeDtypeStruct((rows_max, D_MODEL), BF16),
        compiler_params=pltpu.CompilerParams(
            dimension_semantics=("arbitrary",), vmem_limit_bytes=VMEM_LIMIT),
        name="moe_experts",
    )(tile_expert, tile_valid, n_used, xs, w_gate, w_up, w_down)


def _combine_kernel(dst_ref, np_ref, h_ref, info_ref, fg_ref, ys_hbm, o_ref, buf_ref, sem_ref):
    c = pl.program_id(0)
    n = pl.num_programs(0)
    slot = c % 2

    def copy(cc, s, q):
        return _piece_copy(ys_hbm.at[pl.ds(pl.multiple_of(dst_ref[cc * MAX_PIECES + q] * PIECE, PIECE), PIECE)],
                           buf_ref.at[s, pl.ds(pl.multiple_of(q * PIECE, PIECE), PIECE)],
                           sem_ref.at[s])

    def start_all(cc, s):
        lax.fori_loop(0, np_ref[cc], lambda q, carry: (copy(cc, s, q).start(), carry)[1], 0)

    def wait_all(cc, s):
        lax.fori_loop(0, np_ref[cc], lambda q, carry: (copy(cc, s, q).wait(), carry)[1], 0)

    @pl.when(c == 0)
    def _():
        buf_ref[...] = jnp.zeros_like(buf_ref)
        start_all(0, 0)

    @pl.when(c + 1 < n)
    def _():
        start_all(c + 1, 1 - slot)

    wait_all(c, slot)
    info = info_ref[...]
    r = lax.broadcasted_iota(jnp.int32, (MOE_CHUNK, CHUNK_ROWS), 1).astype(F32)
    weights = jnp.where(r == info[:, 0:1], info[:, 2:3], 0.0) + jnp.where(r == info[:, 1:2], info[:, 3:4], 0.0)
    moe = _dot(weights.astype(BF16), buf_ref[slot])
    o_ref[...] = _rms(h_ref[...] + moe, fg_ref[...])


def _combine(h2, info, fg, ys, dst, npc):
    T = h2.shape[0]
    grid_spec = pltpu.PrefetchScalarGridSpec(
        num_scalar_prefetch=2,
        grid=(T // MOE_CHUNK,),
        in_specs=[
            pl.BlockSpec((MOE_CHUNK, D_MODEL), lambda c, dst, npc: (c, 0)),
            pl.BlockSpec((MOE_CHUNK, LANES), lambda c, dst, npc: (c, 0)),
            pl.BlockSpec((1, D_MODEL), lambda c, dst, npc: (0, 0)),
            pl.BlockSpec(memory_space=pl.ANY),
        ],
        out_specs=pl.BlockSpec((MOE_CHUNK, D_MODEL), lambda c, dst, npc: (c, 0)),
        scratch_shapes=[pltpu.VMEM((2, CHUNK_ROWS, D_MODEL), BF16), pltpu.SemaphoreType.DMA((2,))],
    )
    return pl.pallas_call(
        _combine_kernel,
        grid_spec=grid_spec,
        out_shape=jax.ShapeDtypeStruct((T, D_MODEL), F32),
        compiler_params=pltpu.CompilerParams(
            dimension_semantics=("arbitrary",), vmem_limit_bytes=VMEM_LIMIT),
        name="moe_combine",
    )(dst, npc, h2, info, fg, ys)


def _routing_tables(pieces, rows_max):
    tile_pieces = EXPERT_TILE // PIECE
    total = jnp.sum(pieces, axis=0)
    total_al = (total + tile_pieces - 1) // tile_pieces * tile_pieces
    seg_end = jnp.cumsum(total_al)
    seg_start = seg_end - total_al
    chunk_off = jnp.cumsum(pieces, axis=0) - pieces
    loc_end = jnp.cumsum(pieces, axis=1)
    loc_start = loc_end - pieces
    q = jnp.arange(MAX_PIECES, dtype=jnp.int32)
    owner = jnp.sum((q[None, :, None] >= loc_end[:, None, :]).astype(jnp.int32), axis=-1)
    owner = jnp.minimum(owner, N_EXPERTS - 1)
    experts = jnp.arange(N_EXPERTS, dtype=jnp.int32)
    is_owner = (owner[:, :, None] == experts).astype(jnp.int32)
    offset = seg_start[None, :] + chunk_off - loc_start
    dst = jnp.sum(is_owner * offset[:, None, :], axis=-1) + q[None, :]
    n_local = loc_end[:, -1]
    dst = jnp.where(q[None, :] < n_local[:, None], dst, 0)

    t0 = jnp.arange(rows_max // EXPERT_TILE, dtype=jnp.int32) * tile_pieces
    tile_expert = jnp.minimum(jnp.sum((t0[:, None] >= seg_end[None, :]).astype(jnp.int32), axis=-1), N_EXPERTS - 1)
    copies_end = jnp.sum((tile_expert[:, None] == experts).astype(jnp.int32) * (seg_start + total)[None, :], axis=-1)
    tile_valid = jnp.clip((copies_end - t0) * PIECE, 0, EXPERT_TILE)
    n_used = (seg_end[-1] // tile_pieces).reshape(1)

    k = jnp.arange(tile_pieces, dtype=jnp.int32)
    is_gap = (k[None, :] < (total_al - total)[:, None]).reshape(-1)
    gap_piece = (seg_start + total)[:, None] + k[None, :]
    order = jnp.argsort(jnp.logical_not(is_gap), stable=True)
    gaps = gap_piece.reshape(-1)[order]
    fill = jnp.stack([jnp.sum(is_gap.astype(jnp.int32)), n_used[0]])
    i32 = lambda a: a.astype(jnp.int32)
    return i32(dst.reshape(-1)), i32(n_local), i32(tile_expert), i32(tile_valid), i32(n_used), i32(gaps), i32(fill)


def _rope_tables(positions):
    inv_freq = 1.0 / (ROPE_THETA ** (jnp.arange(0, ROPE_DIM, 2, dtype=F32) / ROPE_DIM))
    ang = positions.astype(F32)[..., None] * inv_freq
    cos, sin = jnp.cos(ang), jnp.sin(ang)
    half = ROPE_DIM // 2
    z = lambda n: jnp.zeros(cos.shape[:-1] + (n,), F32)
    ctab = jnp.concatenate([jnp.ones(cos.shape[:-1] + (NOPE_DIM,), F32), cos, cos, z(LANES - NOPE_DIM - ROPE_DIM)], -1)
    satab = jnp.concatenate([z(NOPE_DIM + half), sin, z(LANES - NOPE_DIM - ROPE_DIM)], -1)
    sbtab = jnp.concatenate([z(NOPE_DIM), -sin, z(half + LANES - NOPE_DIM - ROPE_DIM)], -1)
    return ctab, satab, sbtab


def _pad_heads(w, heads, width):
    k = w.shape[0]
    w = w.reshape(k, heads, width)
    w = jnp.pad(w, ((0, 0), (0, 0), (0, HEAD_PAD - width)))
    return w.reshape(k, heads * HEAD_PAD)


def _layer(l, h, mem, tables, mix_norm_g, w_in, q_norm_g, w_q_up, kv_norm_g, w_kv_up, w_attn_branch,
           pool_w, pool_scale, w_pool_branch, w_mix_out, xattn_norm_g, mem_norm_g, w_xq, w_xkv, w_xo,
           ffn_norm_g, w_router_group, b_router_group, w_router_expert, b_router_expert,
           w_exp_gate, w_exp_up, w_exp_down, out_g, tm_proj):
    B, S, _ = h.shape
    row2 = lambda v: v.reshape(1, -1).astype(F32)

    wi = w_in[l]
    kr_cols = jnp.pad(wi[:, Q_LORA + KV_LORA:Q_LORA + KV_LORA + ROPE_DIM],
                      ((0, 0), (NOPE_DIM, LANES - NOPE_DIM - ROPE_DIM)))
    win = jnp.concatenate([wi[:, :Q_LORA + KV_LORA], kr_cols, wi[:, Q_LORA + KV_LORA + ROPE_DIM:]], axis=1).astype(BF16)
    scale = math.log2(math.e) / math.sqrt(NOPE_DIM + ROPE_DIM)
    wq = _pad_heads(w_q_up[l] * scale, MLA_HEADS, NOPE_DIM + ROPE_DIM).astype(BF16)
    wkv3 = w_kv_up[l].reshape(KV_LORA, MLA_HEADS, NOPE_DIM + V_DIM)
    wkv = jnp.concatenate([
        _pad_heads(wkv3[:, :, :NOPE_DIM].reshape(KV_LORA, -1), MLA_HEADS, NOPE_DIM),
        wkv3[:, :, NOPE_DIM:].reshape(KV_LORA, -1)], axis=1).astype(BF16)

    memkv = _mem_kv(mem.reshape(-1, D_MODEL), row2(mem_norm_g[l]), w_xkv[l].astype(BF16))
    memkv = memkv.reshape(B, -1, 2 * D_MODEL)

    qT, k, vT, ga, gyb = _in_proj(
        h, *tables, row2(mix_norm_g[l]), win, row2(q_norm_g[l]), wq, row2(kv_norm_g[l]), wkv,
        pool_w[l].astype(BF16), row2(pool_scale[l]), w_pool_branch[l].astype(BF16), tm_proj)
    attn = _mla_attention(qT, k, vT)

    w_r = jnp.zeros((D_MODEL, 2 * LANES), F32)
    w_r = w_r.at[:, :N_EXPERTS].set(w_router_expert[l]).at[:, LANES:LANES + N_GROUPS].set(w_router_group[l])
    wr_hi = w_r.astype(BF16)
    wr_lo = (w_r - wr_hi.astype(F32)).astype(BF16)
    rb = jnp.zeros((1, 2 * LANES), F32)
    rb = rb.at[0, :N_EXPERTS].set(b_router_expert[l]).at[0, LANES:LANES + N_GROUPS].set(b_router_group[l])

    h2, xn, info, infoT, pieces = _mix_xattn(
        h, attn, ga, gyb, memkv, w_attn_branch[l].astype(BF16), w_mix_out[l].astype(BF16),
        row2(xattn_norm_g[l]), (w_xq[l] * (1.0 / math.sqrt(MEM_HEAD_DIM))).astype(BF16), w_xo[l].astype(BF16),
        row2(ffn_norm_g[l]), wr_hi, wr_lo, rb)

    T = B * S
    n_chunks = T // MOE_CHUNK
    tile_pieces = EXPERT_TILE // PIECE
    max_pieces = 2 * T // PIECE + n_chunks * N_EXPERTS + N_EXPERTS * tile_pieces
    rows_max = -(-max_pieces // tile_pieces) * EXPERT_TILE
    dst, n_local, tile_expert, tile_valid, n_used, gaps, fill = _routing_tables(
        pieces[:, :, 0].astype(jnp.int32), rows_max)

    xs = _dispatch(xn.reshape(T, D_MODEL), infoT, dst, n_local, gaps, fill, rows_max)
    ys = _experts(xs, w_exp_gate[l], w_exp_up[l], w_exp_down[l], tile_expert, tile_valid, n_used)
    out = _combine(h2.reshape(T, D_MODEL), info.reshape(T, LANES), row2(out_g), ys, dst, n_local)
    return out.reshape(B, S, D_MODEL)


def kernel(x, mem, positions, mix_norm_g, w_in, q_norm_g, w_q_up, kv_norm_g, w_kv_up, w_attn_branch, pool_w, pool_scale, w_pool_branch, w_mix_out, xattn_norm_g, mem_norm_g, w_xq, w_xkv, w_xo, ffn_norm_g, w_router_group, b_router_group, w_router_expert, b_router_expert, w_exp_gate, w_exp_up, w_exp_down, final_norm_g):
    depth = w_in.shape[0]
    assert depth == 1, "the combine kernel fuses the final RMSNorm, which is only valid after the last layer"
    assert x.shape[1] % Q_TILE == 0 and x.shape[1] % MOE_CHUNK == 0
    tables = _rope_tables(positions)
    return _layer(0, x, mem, tables, mix_norm_g, w_in, q_norm_g, w_q_up, kv_norm_g, w_kv_up, w_attn_branch,
                  pool_w, pool_scale, w_pool_branch, w_mix_out, xattn_norm_g, mem_norm_g, w_xq, w_xkv, w_xo,
                  ffn_norm_g, w_router_group, b_router_group, w_router_expert, b_router_expert,
                  w_exp_gate, w_exp_up, w_exp_down, final_norm_g, IN_PROJ_TILE)
```

```python
import functools
import math

import jax
import jax.numpy as jnp
from jax import lax
from jax.experimental import pallas as pl
from jax.experimental.pallas import tpu as pltpu

F32 = jnp.float32
BF16 = jnp.bfloat16

D_MODEL = 1024
CHUNK = 64
MLA_HEADS = 8
Q_LORA = 384
KV_LORA = 256
NOPE_DIM = 64
ROPE_DIM = 32
V_DIM = 64
ROPE_THETA = 10000.0
POOL_WIDTH = 512
POOL_WINDOWS = (2, 4, 8, 16)
POOL_GROUP_DIM = POOL_WIDTH // len(POOL_WINDOWS)
POOL_HALO = 16
MEM_HEADS = 4
MEM_HEAD_DIM = D_MODEL // MEM_HEADS
N_GROUPS = 4
EXPERTS_PER_GROUP = 8
N_EXPERTS = N_GROUPS * EXPERTS_PER_GROUP
EXPERT_FF = 256
EPS = 1e-6

LANES = 128
HEAD_PAD = LANES
Q_TILE = 512
KV_TILE = 512
KEY_SUB = 128
V_ROWS = V_DIM + 16
IN_PROJ_TILE = 512
MOE_CHUNK = 512
PIECE = 16
MAX_PIECES = 2 * MOE_CHUNK // PIECE + N_EXPERTS
CHUNK_ROWS = MAX_PIECES * PIECE
EXPERT_TILE = 512
EXPERT_SUB = 256
MIX_SUB = 256
QK_AHEAD = 3
PV_BEHIND = 2

_C_Q = 0
_C_KV = _C_Q + Q_LORA
_C_KR = _C_KV + KV_LORA
_C_POOL = _C_KR + LANES
_C_GA = _C_POOL + POOL_WIDTH
_C_GB = _C_GA + D_MODEL
_C_END = _C_GB + D_MODEL

VMEM_LIMIT = 56 * 1024 * 1024


def _rms(x, g):
    return x * lax.rsqrt(jnp.mean(x * x, axis=-1, keepdims=True) + EPS) * g


def _dot(a, b):
    return jnp.dot(a, b, preferred_element_type=F32)


def _dot_nt(a, b):
    return lax.dot_general(a, b, (((1,), (1,)), ((), ())), preferred_element_type=F32)


def _mem_kv_kernel(mem_ref, g_ref, w_ref, kv_ref):
    mn = _rms(mem_ref[...], g_ref[...]).astype(BF16)
    kv_ref[...] = _dot(mn, w_ref[...]).astype(BF16)


def _mem_kv(mem2d, g, w_xkv):
    rows = mem2d.shape[0]
    tm = min(512, rows)
    assert rows % tm == 0
    return pl.pallas_call(
        _mem_kv_kernel,
        grid=(rows // tm,),
        in_specs=[
            pl.BlockSpec((tm, D_MODEL), lambda i: (i, 0)),
            pl.BlockSpec((1, D_MODEL), lambda i: (0, 0)),
            pl.BlockSpec((D_MODEL, 2 * D_MODEL), lambda i: (0, 0)),
        ],
        out_specs=pl.BlockSpec((tm, 2 * D_MODEL), lambda i: (i, 0)),
        out_shape=jax.ShapeDtypeStruct((rows, 2 * D_MODEL), BF16),
        compiler_params=pltpu.CompilerParams(vmem_limit_bytes=VMEM_LIMIT),
        name="mem_kv",
    )(mem2d, g, w_xkv)


def _rope(t, c, sa, sb):
    w = t.shape[-1]
    return t * c + pltpu.roll(t, ROPE_DIM // 2, 1) * sa + pltpu.roll(t, w - ROPE_DIM // 2, 1) * sb


def _in_proj_kernel(x_ref, rot_ref, g_ref, win_ref, qg_ref, wq_ref, kvg_ref, wkv_ref,
                    poolw_ref, pscale_ref, wpb_ref,
                    qT_out, k_out, vT_out, ga_out, gyb_out, hist_ref):
    tm = x_ref.shape[0]
    i = pl.program_id(1)
    hn = _rms(x_ref[...], g_ref[...]).astype(BF16)

    rot = rot_ref[...].T
    lane = lax.broadcasted_iota(jnp.int32, rot.shape, 1)
    half = ROPE_DIM // 2
    rope_lo = NOPE_DIM
    c1 = jnp.where(lane < rope_lo + ROPE_DIM, rot, 0.0)
    sa1 = jnp.where((lane >= rope_lo + half) & (lane < rope_lo + ROPE_DIM), pltpu.roll(rot, LANES - half, 1), 0.0)
    sb1 = jnp.where((lane >= rope_lo) & (lane < rope_lo + half), pltpu.roll(rot, LANES - 3 * half, 1), 0.0)
    c8 = jnp.tile(c1, (1, MLA_HEADS))
    sa8 = jnp.tile(sa1, (1, MLA_HEADS))
    sb8 = jnp.tile(sb1, (1, MLA_HEADS))

    q_lat = _dot(hn, win_ref[:, _C_Q:_C_KV])
    qn = _rms(q_lat, qg_ref[...]).astype(BF16)
    q = _rope(_dot(qn, wq_ref[...]), c8, sa8, sb8)

    kv_lat = _dot(hn, win_ref[:, _C_KV:_C_KR])
    kvn = _rms(kv_lat, kvg_ref[...]).astype(BF16)
    k_nope = _dot(kvn, wkv_ref[:, 0:MLA_HEADS * HEAD_PAD])
    kr = _rope(_dot(hn, win_ref[:, _C_KR:_C_POOL]), c1, sa1, sb1)
    k_out[...] = (k_nope + jnp.tile(kr, (1, MLA_HEADS))).astype(BF16)
    v = _dot(kvn, wkv_ref[:, MLA_HEADS * HEAD_PAD:])
    for t in range(tm // Q_TILE):
        qT_out[t] = q[t * Q_TILE:(t + 1) * Q_TILE, :].T.astype(BF16)
    vT = v.T
    ones = jnp.ones((V_ROWS - V_DIM, tm), F32)
    vT = jnp.concatenate(
        [blk for h in range(MLA_HEADS) for blk in (vT[h * V_DIM:(h + 1) * V_DIM, :], ones)], axis=0).astype(BF16)
    for t in range(tm // KV_TILE):
        vT_out[t] = vT[:, t * KV_TILE:(t + 1) * KV_TILE]

    u = _dot(hn, win_ref[:, _C_POOL:_C_GA])

    @pl.when(i == 0)
    def _():
        hist_ref[...] = jnp.zeros_like(hist_ref)

    ext = jnp.concatenate([hist_ref[...], u], axis=0)
    hist_ref[...] = u[tm - POOL_HALO:, :]
    t_idx = i * tm + lax.broadcasted_iota(jnp.int32, (tm, POOL_GROUP_DIM), 0)
    ys = []
    for g, w in enumerate(POOL_WINDOWS):
        c0 = g * POOL_GROUP_DIM
        run = ext[:, c0:c0 + POOL_GROUP_DIM]
        span = 1
        while span < w:
            run = run + pltpu.roll(run, span, 0)
            span *= 2
        cnt = jnp.minimum(t_idx + 1, w).astype(F32)
        d = run[POOL_HALO:, :] / cnt - u[:, c0:c0 + POOL_GROUP_DIM]
        ys.append(_dot(d.astype(BF16), poolw_ref[g]))
    y = (jnp.concatenate(ys, axis=1) * pscale_ref[...]).astype(BF16)
    y_b = _dot(y, wpb_ref[...])

    g_a = jax.nn.sigmoid(_dot(hn, win_ref[:, _C_GA:_C_GB]))
    ga_out[...] = g_a.astype(BF16)
    g_b = jax.nn.sigmoid(_dot(hn, win_ref[:, _C_GB:_C_END]))
    gyb_out[...] = (g_b * y_b).astype(BF16)


def _in_proj(x, rot, g, win, qg, wq, kvg, wkv, poolw, pscale, wpb, tm):
    B, S, _ = x.shape
    row = lambda b, i: (b, i, 0)
    const2 = lambda b, i: (0, 0)
    const3 = lambda b, i: (0, 0, 0)
    slab = lambda b, i: (b, i, 0, 0)
    return pl.pallas_call(
        _in_proj_kernel,
        grid=(B, S // tm),
        in_specs=[
            pl.BlockSpec((None, tm, D_MODEL), row),
            pl.BlockSpec((None, LANES, tm), lambda b, i: (b, 0, i)),
            pl.BlockSpec((1, D_MODEL), const2),
            pl.BlockSpec(win.shape, const2),
            pl.BlockSpec((1, Q_LORA), const2),
            pl.BlockSpec(wq.shape, const2),
            pl.BlockSpec((1, KV_LORA), const2),
            pl.BlockSpec(wkv.shape, const2),
            pl.BlockSpec(poolw.shape, const3),
            pl.BlockSpec((1, POOL_WIDTH), const2),
            pl.BlockSpec(wpb.shape, const2),
        ],
        out_specs=[
            pl.BlockSpec((None, tm // Q_TILE, MLA_HEADS * HEAD_PAD, Q_TILE), slab),
            pl.BlockSpec((None, tm, MLA_HEADS * HEAD_PAD), row),
            pl.BlockSpec((None, tm // KV_TILE, MLA_HEADS * V_ROWS, KV_TILE), slab),
            pl.BlockSpec((None, tm, D_MODEL), row),
            pl.BlockSpec((None, tm, D_MODEL), row),
        ],
        out_shape=[jax.ShapeDtypeStruct((B, S // Q_TILE, MLA_HEADS * HEAD_PAD, Q_TILE), BF16),
                   jax.ShapeDtypeStruct((B, S, MLA_HEADS * HEAD_PAD), BF16),
                   jax.ShapeDtypeStruct((B, S // KV_TILE, MLA_HEADS * V_ROWS, KV_TILE), BF16),
                   jax.ShapeDtypeStruct((B, S, D_MODEL), BF16),
                   jax.ShapeDtypeStruct((B, S, D_MODEL), BF16)],
        scratch_shapes=[pltpu.VMEM((POOL_HALO, POOL_WIDTH), F32)],
        compiler_params=pltpu.CompilerParams(
            dimension_semantics=("arbitrary", "arbitrary"), vmem_limit_bytes=VMEM_LIMIT),
        name="in_proj",
    )(x, rot, g, win, qg, wq, kvg, wkv, poolw, pscale, wpb)


def _mla_kernel(qT_ref, k_ref, vT_ref, o_ref, m_ref, acc_ref):
    i = pl.program_id(1)
    n_sub = KV_TILE // KEY_SUB
    units = [(h, c) for c in range(n_sub) for h in range(MLA_HEADS)]
    qry_c = lax.broadcasted_iota(jnp.int32, (KEY_SUB, Q_TILE), 1) // CHUNK

    def scores(j, h, c):
        hs = slice(h * HEAD_PAD, (h + 1) * HEAD_PAD)
        rows = pl.ds(pl.multiple_of(j * KV_TILE + c * KEY_SUB, KEY_SUB), KEY_SUB)
        return _dot(k_ref[rows, hs], qT_ref[hs, :])

    def fold(h, alpha, pv):
        acc_ref[h] = pv if alpha is None else alpha * acc_ref[h] + pv

    def sweep(j, diag):
        ahead = [scores(j, *u) for u in units[:QK_AHEAD]]
        pending = []
        for n, (h, c) in enumerate(units):
            s = ahead.pop(0)
            if n + QK_AHEAD < len(units):
                ahead.append(scores(j, *units[n + QK_AHEAD]))
            first = False
            if diag is not None:
                key_c0 = (diag * KV_TILE + c * KEY_SUB) // CHUNK
                key_c = key_c0 + lax.broadcasted_iota(jnp.int32, (KEY_SUB, Q_TILE), 0) // CHUNK
                s = jnp.where(key_c <= qry_c, s, -jnp.inf)
                first = diag == 0 and c == 0
            s_max = jnp.max(s, axis=0, keepdims=True)
            m_new = s_max if first else jnp.maximum(m_ref[h], s_max)
            p = jnp.exp2(s - m_new).astype(BF16)
            pv = _dot(vT_ref[j, h * V_ROWS:(h + 1) * V_ROWS, c * KEY_SUB:(c + 1) * KEY_SUB], p)
            alpha = None if first else jnp.exp2(m_ref[h] - m_new)
            m_ref[h] = m_new
            pending.append((h, alpha, pv))
            if len(pending) > PV_BEHIND:
                fold(*pending.pop(0))
        for item in pending:
            fold(*item)

    n_diag = Q_TILE // KV_TILE
    for d in range(n_diag):
        sweep(i * n_diag + d, d)

    def body(j, carry):
        sweep(j, None)
        return carry

    lax.fori_loop(0, i * n_diag, body, 0)
    oT = jnp.concatenate([acc_ref[h, :V_DIM, :] / acc_ref[h, V_DIM:V_DIM + 1, :] for h in range(MLA_HEADS)], axis=0)
    o_ref[...] = oT.T.astype(BF16)


def _mla_attention(qT, k, vT):
    B, S, W = k.shape
    return pl.pallas_call(
        _mla_kernel,
        grid=(B, S // Q_TILE),
        in_specs=[
            pl.BlockSpec((None, None, W, Q_TILE), lambda b, i: (b, i, 0, 0)),
            pl.BlockSpec((None, S, W), lambda b, i: (b, 0, 0)),
            pl.BlockSpec((None, S // KV_TILE, MLA_HEADS * V_ROWS, KV_TILE), lambda b, i: (b, 0, 0, 0)),
        ],
        out_specs=pl.BlockSpec((None, Q_TILE, MLA_HEADS * V_DIM), lambda b, i: (b, i, 0)),
        out_shape=jax.ShapeDtypeStruct((B, S, MLA_HEADS * V_DIM), BF16),
        scratch_shapes=[pltpu.VMEM((MLA_HEADS, 1, Q_TILE), F32),
                        pltpu.VMEM((MLA_HEADS, V_ROWS, Q_TILE), F32)],
        compiler_params=pltpu.CompilerParams(
            dimension_semantics=("arbitrary", "arbitrary"), vmem_limit_bytes=VMEM_LIMIT),
        name="mla_attn",
    )(qT, k, vT)


def _route(logits_t, bias_t, tri_upper, tri_lower):
    tm = logits_t.shape[1]
    neg = -jnp.inf
    rg = lax.broadcasted_iota(jnp.int32, (8, tm), 0)
    re = lax.broadcasted_iota(jnp.int32, (N_EXPERTS, tm), 0)
    top = lambda v: jnp.max(v, axis=0, keepdims=True)

    lg = jnp.where(rg < N_GROUPS, logits_t[LANES:LANES + 8, :] + bias_t[LANES:LANES + 8, :], neg)
    ge = jnp.exp(lg - top(lg))
    gp = ge / jnp.sum(ge, axis=0, keepdims=True)
    g_w = top(gp)
    g_idx = jnp.min(jnp.where(gp == g_w, rg, 8), axis=0, keepdims=True)

    sel = re // EXPERTS_PER_GROUP == g_idx
    le = jnp.where(sel, logits_t[:N_EXPERTS, :] + bias_t[:N_EXPERTS, :], neg)
    ee = jnp.exp(le - top(le))
    ep = jnp.where(sel, ee / jnp.sum(ee, axis=0, keepdims=True), -1.0)
    w1 = top(ep)
    i1 = jnp.min(jnp.where(ep == w1, re, N_EXPERTS), axis=0, keepdims=True)
    ep2 = jnp.where(re == i1, -1.0, ep)
    w2 = top(ep2)
    i2 = jnp.min(jnp.where(ep2 == w2, re, N_EXPERTS), axis=0, keepdims=True)
    den = w1 + w2
    c1 = g_w * (w1 / den)
    c2 = g_w * (w2 / den)

    oh1 = (re == i1).astype(F32)
    oh2 = (re == i2).astype(F32)
    both = oh1 + oh2
    earlier = _dot(both.astype(BF16), tri_upper)
    pieces = jnp.floor((jnp.sum(both, axis=1, keepdims=True) + (PIECE - 1)) * (1.0 / PIECE))
    start = _dot(tri_lower, jnp.broadcast_to(pieces, (N_EXPERTS, LANES)).astype(BF16))[:, 0:1] * PIECE
    pos1 = jnp.sum(oh1 * (earlier + start), axis=0, keepdims=True)
    pos2 = jnp.sum(oh2 * (earlier + start), axis=0, keepdims=True)
    info_t = jnp.concatenate([pos1, pos2, c1, c2, jnp.zeros((4, tm), F32)], axis=0)
    return info_t, pieces


def _mix_kernel(x_ref, attn_ref, ga_ref, gyb_ref, kv_ref, wab_ref, wmix_ref, xg_ref, wxq_ref, wxo_ref,
                fg_ref, wr_hi_ref, wr_lo_ref, rb_ref, tri_ref, lower_ref,
                h_out, xn_out, info_out, infoT_out, pieces_out):
    tm = x_ref.shape[0]
    blocks = [slice(b * MIX_SUB, (b + 1) * MIX_SUB) for b in range(tm // MIX_SUB)]
    y_a = [_dot(attn_ref[r, :], wab_ref[...]) for r in blocks]
    merged = [(ga_ref[r, :].astype(F32) * y + gyb_ref[r, :].astype(F32)).astype(BF16) for r, y in zip(blocks, y_a)]
    h1 = [x_ref[r, :] + _dot(m, wmix_ref[...]) for r, m in zip(blocks, merged)]

    hn = [_rms(h, xg_ref[...]).astype(BF16) for h in h1]
    q = [_dot(v, wxq_ref[...]).astype(BF16) for v in hn]
    heads = [[] for _ in blocks]
    for h in range(MEM_HEADS):
        hs = slice(h * MEM_HEAD_DIM, (h + 1) * MEM_HEAD_DIM)
        vs = slice(D_MODEL + h * MEM_HEAD_DIM, D_MODEL + (h + 1) * MEM_HEAD_DIM)
        s = [_dot_nt(v[:, hs], kv_ref[:, hs]) for v in q]
        p = [jnp.exp(v - jnp.max(v, axis=-1, keepdims=True)) for v in s]
        o = [_dot(v.astype(BF16), kv_ref[:, vs]) for v in p]
        for b in range(len(blocks)):
            heads[b].append((o[b] / jnp.sum(p[b], axis=-1, keepdims=True)).astype(BF16))
    h2 = [h + _dot(jnp.concatenate(hd, axis=1), wxo_ref[...]) for h, hd in zip(h1, heads)]

    xn = [_rms(h, fg_ref[...]) for h in h2]
    xn_hi = [v.astype(BF16) for v in xn]
    xn_lo = [(v - hi.astype(F32)).astype(BF16) for v, hi in zip(xn, xn_hi)]
    logits = [_dot(hi, wr_hi_ref[...]) + (_dot(hi, wr_lo_ref[...]) + _dot(lo, wr_hi_ref[...]))
              for hi, lo in zip(xn_hi, xn_lo)]
    for r, h, hi in zip(blocks, h2, xn_hi):
        h_out[r, :] = h
        xn_out[r, :] = hi
    logits_t = jnp.concatenate(logits, axis=0).T
    info_t, pieces = _route(logits_t, rb_ref[...], tri_ref[...], lower_ref[...])
    infoT_out[...] = info_t
    info_out[...] = jnp.concatenate([info_t, jnp.zeros((LANES - 8, tm), F32)], axis=0).T
    pieces_out[...] = jnp.broadcast_to(pieces, (N_EXPERTS, LANES))


def _mix_xattn(x, attn, ga, gyb, memkv, wab, wmix, xg, wxq, wxo, fg, wr_hi, wr_lo, rb):
    B, S, _ = x.shape
    M = memkv.shape[1]
    tm = MOE_CHUNK
    nt = S // tm
    row = lambda b, i: (b, i, 0)
    const2 = lambda b, i: (0, 0)
    tri = (lax.broadcasted_iota(jnp.int32, (tm, tm), 0) < lax.broadcasted_iota(jnp.int32, (tm, tm), 1)).astype(BF16)
    lower = (lax.broadcasted_iota(jnp.int32, (N_EXPERTS, N_EXPERTS), 1)
             < lax.broadcasted_iota(jnp.int32, (N_EXPERTS, N_EXPERTS), 0)).astype(BF16)
    rb = jnp.broadcast_to(rb.reshape(2 * LANES, 1), (2 * LANES, tm))
    return pl.pallas_call(
        _mix_kernel,
        grid=(B, S // tm),
        in_specs=[
            pl.BlockSpec((None, tm, D_MODEL), row),
            pl.BlockSpec((None, tm, MLA_HEADS * V_DIM), row),
            pl.BlockSpec((None, tm, D_MODEL), row),
            pl.BlockSpec((None, tm, D_MODEL), row),
            pl.BlockSpec((None, M, 2 * D_MODEL), lambda b, i: (b, 0, 0)),
            pl.BlockSpec(wab.shape, const2),
            pl.BlockSpec(wmix.shape, const2),
            pl.BlockSpec((1, D_MODEL), const2),
            pl.BlockSpec(wxq.shape, const2),
            pl.BlockSpec(wxo.shape, const2),
            pl.BlockSpec((1, D_MODEL), const2),
            pl.BlockSpec(wr_hi.shape, const2),
            pl.BlockSpec(wr_lo.shape, const2),
            pl.BlockSpec((2 * LANES, tm), const2),
            pl.BlockSpec((tm, tm), const2),
            pl.BlockSpec((N_EXPERTS, N_EXPERTS), const2),
        ],
        out_specs=[
            pl.BlockSpec((None, tm, D_MODEL), row),
            pl.BlockSpec((None, tm, D_MODEL), row),
            pl.BlockSpec((None, tm, LANES), row),
            pl.BlockSpec((8, tm), lambda b, i: (0, b * nt + i)),
            pl.BlockSpec((None, N_EXPERTS, LANES), lambda b, i: (b * nt + i, 0, 0)),
        ],
        out_shape=[jax.ShapeDtypeStruct((B, S, D_MODEL), F32),
                   jax.ShapeDtypeStruct((B, S, D_MODEL), BF16),
                   jax.ShapeDtypeStruct((B, S, LANES), F32),
                   jax.ShapeDtypeStruct((8, B * S), F32),
                   jax.ShapeDtypeStruct((B * nt, N_EXPERTS, LANES), F32)],
        compiler_params=pltpu.CompilerParams(
            dimension_semantics=("arbitrary", "arbitrary"), vmem_limit_bytes=VMEM_LIMIT),
        name="mix_xattn",
    )(x, attn, ga, gyb, memkv, wab, wmix, xg, wxq, wxo, fg, wr_hi, wr_lo, rb, tri, lower)


def _piece_copy(src_ref, dst_ref, sem):
    return pltpu.make_async_copy(src_ref, dst_ref, sem)


def _dispatch_kernel(dst_ref, np_ref, gap_ref, fill_ref, xn_ref, infoT_ref, xs_hbm, buf_ref, zero_ref, sem_ref):
    c = pl.program_id(0)
    n = pl.num_programs(0)
    slot = c % 2

    n_tiles = xs_hbm.shape[0] // EXPERT_TILE

    def gap_copy(g):
        return _piece_copy(zero_ref.at[pl.ds(0, PIECE)],
                           xs_hbm.at[pl.ds(pl.multiple_of(gap_ref[g] * PIECE, PIECE), PIECE)], sem_ref.at[2])

    def tail_copy(t):
        return _piece_copy(zero_ref, xs_hbm.at[pl.ds(pl.multiple_of(t * EXPERT_TILE, EXPERT_TILE), EXPERT_TILE)],
                           sem_ref.at[2])

    @pl.when(c == 0)
    def _():
        zero_ref[...] = jnp.zeros_like(zero_ref)
        lax.fori_loop(0, fill_ref[0], lambda g, carry: (gap_copy(g).start(), carry)[1], 0)
        lax.fori_loop(fill_ref[1], n_tiles, lambda t, carry: (tail_copy(t).start(), carry)[1], 0)

    def copy(cc, s, q):
        return _piece_copy(buf_ref.at[s, pl.ds(pl.multiple_of(q * PIECE, PIECE), PIECE)],
                           xs_hbm.at[pl.ds(pl.multiple_of(dst_ref[cc * MAX_PIECES + q] * PIECE, PIECE), PIECE)],
                           sem_ref.at[s])

    def start_all(cc, s):
        lax.fori_loop(0, np_ref[cc], lambda q, carry: (copy(cc, s, q).start(), carry)[1], 0)

    def wait_all(cc, s):
        lax.fori_loop(0, np_ref[cc], lambda q, carry: (copy(cc, s, q).wait(), carry)[1], 0)

    @pl.when(c >= 2)
    def _():
        wait_all(c - 2, slot)

    pos1 = infoT_ref[0:1, :]
    pos2 = infoT_ref[1:2, :]
    r = lax.broadcasted_iota(jnp.int32, (CHUNK_ROWS, MOE_CHUNK), 0).astype(F32)
    onehot = jnp.where((r == pos1) | (r == pos2), 1.0, 0.0).astype(BF16)
    buf_ref[slot] = _dot(onehot, xn_ref[...]).astype(BF16)
    start_all(c, slot)

    @pl.when(c == n - 1)
    def _():
        @pl.when(c >= 1)
        def _():
            wait_all(c - 1, 1 - slot)
        wait_all(c, slot)
        lax.fori_loop(0, fill_ref[0], lambda g, carry: (gap_copy(g).wait(), carry)[1], 0)
        lax.fori_loop(fill_ref[1], n_tiles, lambda t, carry: (tail_copy(t).wait(), carry)[1], 0)


def _dispatch(xn, infoT, dst, npc, gaps, fill, rows_max):
    T = xn.shape[0]
    grid_spec = pltpu.PrefetchScalarGridSpec(
        num_scalar_prefetch=4,
        grid=(T // MOE_CHUNK,),
        in_specs=[
            pl.BlockSpec((MOE_CHUNK, D_MODEL), lambda c, *_: (c, 0)),
            pl.BlockSpec((8, MOE_CHUNK), lambda c, *_: (0, c)),
        ],
        out_specs=pl.BlockSpec(memory_space=pl.ANY),
        scratch_shapes=[pltpu.VMEM((2, CHUNK_ROWS, D_MODEL), BF16), pltpu.VMEM((EXPERT_TILE, D_MODEL), BF16),
                        pltpu.SemaphoreType.DMA((3,))],
    )
    return pl.pallas_call(
        _dispatch_kernel,
        grid_spec=grid_spec,
        out_shape=jax.ShapeDtypeStruct((rows_max, D_MODEL), BF16),
        compiler_params=pltpu.CompilerParams(
            dimension_semantics=("arbitrary",), vmem_limit_bytes=VMEM_LIMIT),
        name="moe_dispatch",
    )(dst, npc, gaps, fill, xn, infoT)


def _expert_kernel(te_ref, tv_ref, nu_ref, x_ref, wg_ref, wu_ref, wd_ref, y_ref, wgu_bf, wd_bf):
    i = pl.program_id(0)

    @pl.when(i < nu_ref[0])
    def _():
        @pl.when((i == 0) | (te_ref[i] != te_ref[jnp.maximum(i - 1, 0)]))
        def _():
            wgu_bf[:, :EXPERT_FF] = wg_ref[...].astype(BF16)
            wgu_bf[:, EXPERT_FF:] = wu_ref[...].astype(BF16)
            wd_bf[...] = wd_ref[...].astype(BF16)

        blocks = [slice(b * EXPERT_SUB, (b + 1) * EXPERT_SUB) for b in range(EXPERT_TILE // EXPERT_SUB)]
        row = lax.broadcasted_iota(jnp.int32, (EXPERT_SUB, D_MODEL), 0)
        gus = []
        for b, rows in enumerate(blocks):
            x = x_ref[rows, :]
            x = jnp.where(row < tv_ref[i] - b * EXPERT_SUB, x, jnp.zeros_like(x))
            gus.append(_dot(x, wgu_bf[...]))
        hids = []
        for gu in gus:
            gate = gu[:, :EXPERT_FF]
            hids.append((gate * jax.nn.sigmoid(gate) * gu[:, EXPERT_FF:]).astype(BF16))
        for rows, hid in zip(blocks, hids):
            y_ref[rows, :] = _dot(hid, wd_bf[...]).astype(BF16)

    @pl.when(i >= nu_ref[0])
    def _():
        y_ref[...] = jnp.zeros_like(y_ref)


def _experts(xs, w_gate, w_up, w_down, tile_expert, tile_valid, n_used):
    rows_max = xs.shape[0]
    last = lambda i, nu: jnp.minimum(i, nu[0] - 1)
    expert = lambda i, te, tv, nu: (te[last(i, nu)], 0, 0)
    grid_spec = pltpu.PrefetchScalarGridSpec(
        num_scalar_prefetch=3,
        grid=(rows_max // EXPERT_TILE,),
        in_specs=[
            pl.BlockSpec((EXPERT_TILE, D_MODEL), lambda i, te, tv, nu: (last(i, nu), 0)),
            pl.BlockSpec((None, D_MODEL, EXPERT_FF), expert),
            pl.BlockSpec((None, D_MODEL, EXPERT_FF), expert),
            pl.BlockSpec((None, EXPERT_FF, D_MODEL), expert),
        ],
        out_specs=pl.BlockSpec((EXPERT_TILE, D_MODEL), lambda i, te, tv, nu: (i, 0)),
        scratch_shapes=[pltpu.VMEM((D_MODEL, 2 * EXPERT_FF), BF16), pltpu.VMEM((EXPERT_FF, D_MODEL), BF16)],
    )
    return pl.pallas_call(
        _expert_kernel,
        grid_spec=grid_spec,
        out_shape=jax.ShapeDtypeStruct((rows_max, D_MODEL), BF16),
        compiler_params=pltpu.CompilerParams(
            dimension_semantics=("arbitrary",), vmem_limit_bytes=VMEM_LIMIT),
        name="moe_experts",
    )(tile_expert, tile_valid, n_used, xs, w_gate, w_up, w_down)


def _combine_kernel(dst_ref, np_ref, h_ref, info_ref, fg_ref, ys_hbm, o_ref, buf_ref, sem_ref):
    c = pl.program_id(0)
    n = pl.num_programs(0)
    slot = c % 2

    def copy(cc, s, q):
        return _piece_copy(ys_hbm.at[pl.ds(pl.multiple_of(dst_ref[cc * MAX_PIECES + q] * PIECE, PIECE), PIECE)],
                           buf_ref.at[s, pl.ds(pl.multiple_of(q * PIECE, PIECE), PIECE)],
                           sem_ref.at[s])

    def start_all(cc, s):
        lax.fori_loop(0, np_ref[cc], lambda q, carry: (copy(cc, s, q).start(), carry)[1], 0)

    def wait_all(cc, s):
        lax.fori_loop(0, np_ref[cc], lambda q, carry: (copy(cc, s, q).wait(), carry)[1], 0)

    @pl.when(c == 0)
    def _():
        buf_ref[...] = jnp.zeros_like(buf_ref)
        start_all(0, 0)

    @pl.when(c + 1 < n)
    def _():
        start_all(c + 1, 1 - slot)

    wait_all(c, slot)
    info = info_ref[...]
    r = lax.broadcasted_iota(jnp.int32, (MOE_CHUNK, CHUNK_ROWS), 1).astype(F32)
    weights = jnp.where(r == info[:, 0:1], info[:, 2:3], 0.0) + jnp.where(r == info[:, 1:2], info[:, 3:4], 0.0)
    moe = _dot(weights.astype(BF16), buf_ref[slot])
    o_ref[...] = _rms(h_ref[...] + moe, fg_ref[...])


def _combine(h2, info, fg, ys, dst, npc):
    T = h2.shape[0]
    grid_spec = pltpu.PrefetchScalarGridSpec(
        num_scalar_prefetch=2,
        grid=(T // MOE_CHUNK,),
        in_specs=[
            pl.BlockSpec((MOE_CHUNK, D_MODEL), lambda c, dst, npc: (c, 0)),
            pl.BlockSpec((MOE_CHUNK, LANES), lambda c, dst, npc: (c, 0)),
            pl.BlockSpec((1, D_MODEL), lambda c, dst, npc: (0, 0)),
            pl.BlockSpec(memory_space=pl.ANY),
        ],
        out_specs=pl.BlockSpec((MOE_CHUNK, D_MODEL), lambda c, dst, npc: (c, 0)),
        scratch_shapes=[pltpu.VMEM((2, CHUNK_ROWS, D_MODEL), BF16), pltpu.SemaphoreType.DMA((2,))],
    )
    return pl.pallas_call(
        _combine_kernel,
        grid_spec=grid_spec,
        out_shape=jax.ShapeDtypeStruct((T, D_MODEL), F32),
        compiler_params=pltpu.CompilerParams(
            dimension_semantics=("arbitrary",), vmem_limit_bytes=VMEM_LIMIT),
        name="moe_combine",
    )(dst, npc, h2, info, fg, ys)


def _routing_tables(pieces, rows_max):
    tile_pieces = EXPERT_TILE // PIECE
    total = jnp.sum(pieces, axis=0)
    total_al = (total + tile_pieces - 1) // tile_pieces * tile_pieces
    seg_end = jnp.cumsum(total_al)
    seg_start = seg_end - total_al
    chunk_off = jnp.cumsum(pieces, axis=0) - pieces
    loc_end = jnp.cumsum(pieces, axis=1)
    loc_start = loc_end - pieces
    q = jnp.arange(MAX_PIECES, dtype=jnp.int32)
    owner = jnp.sum((q[None, :, None] >= loc_end[:, None, :]).astype(jnp.int32), axis=-1)
    owner = jnp.minimum(owner, N_EXPERTS - 1)
    experts = jnp.arange(N_EXPERTS, dtype=jnp.int32)
    is_owner = (owner[:, :, None] == experts).astype(jnp.int32)
    offset = seg_start[None, :] + chunk_off - loc_start
    dst = jnp.sum(is_owner * offset[:, None, :], axis=-1) + q[None, :]
    n_local = loc_end[:, -1]
    dst = jnp.where(q[None, :] < n_local[:, None], dst, 0)

    t0 = jnp.arange(rows_max // EXPERT_TILE, dtype=jnp.int32) * tile_pieces
    tile_expert = jnp.minimum(jnp.sum((t0[:, None] >= seg_end[None, :]).astype(jnp.int32), axis=-1), N_EXPERTS - 1)
    copies_end = jnp.sum((tile_expert[:, None] == experts).astype(jnp.int32) * (seg_start + total)[None, :], axis=-1)
    tile_valid = jnp.clip((copies_end - t0) * PIECE, 0, EXPERT_TILE)
    n_used = (seg_end[-1] // tile_pieces).reshape(1)

    k = jnp.arange(tile_pieces, dtype=jnp.int32)
    is_gap = (k[None, :] < (total_al - total)[:, None]).reshape(-1)
    gap_piece = (seg_start + total)[:, None] + k[None, :]
    order = jnp.argsort(jnp.logical_not(is_gap), stable=True)
    gaps = gap_piece.reshape(-1)[order]
    fill = jnp.stack([jnp.sum(is_gap.astype(jnp.int32)), n_used[0]])
    i32 = lambda a: a.astype(jnp.int32)
    return i32(dst.reshape(-1)), i32(n_local), i32(tile_expert), i32(tile_valid), i32(n_used), i32(gaps), i32(fill)


def _rope_tables(positions):
    inv_freq = 1.0 / (ROPE_THETA ** (jnp.arange(0, ROPE_DIM, 2, dtype=F32) / ROPE_DIM))
    ang = positions.astype(F32)[:, None, :] * inv_freq[None, :, None]
    cos, sin = jnp.cos(ang), jnp.sin(ang)
    ones = jnp.ones((ang.shape[0], NOPE_DIM, ang.shape[2]), F32)
    return jnp.concatenate([ones, cos, cos, sin, -sin], axis=1)


def _pad_heads(w, heads, width):
    k = w.shape[0]
    w = w.reshape(k, heads, width)
    w = jnp.pad(w, ((0, 0), (0, 0), (0, HEAD_PAD - width)))
    return w.reshape(k, heads * HEAD_PAD)


def _layer(l, h, mem, tables, mix_norm_g, w_in, q_norm_g, w_q_up, kv_norm_g, w_kv_up, w_attn_branch,
           pool_w, pool_scale, w_pool_branch, w_mix_out, xattn_norm_g, mem_norm_g, w_xq, w_xkv, w_xo,
           ffn_norm_g, w_router_group, b_router_group, w_router_expert, b_router_expert,
           w_exp_gate, w_exp_up, w_exp_down, out_g, tm_proj):
    B, S, _ = h.shape
    row2 = lambda v: v.reshape(1, -1).astype(F32)

    wi = w_in[l]
    kr_cols = jnp.pad(wi[:, Q_LORA + KV_LORA:Q_LORA + KV_LORA + ROPE_DIM],
                      ((0, 0), (NOPE_DIM, LANES - NOPE_DIM - ROPE_DIM)))
    win = jnp.concatenate([wi[:, :Q_LORA + KV_LORA], kr_cols, wi[:, Q_LORA + KV_LORA + ROPE_DIM:]], axis=1).astype(BF16)
    scale = math.log2(math.e) / math.sqrt(NOPE_DIM + ROPE_DIM)
    wq = _pad_heads(w_q_up[l] * scale, MLA_HEADS, NOPE_DIM + ROPE_DIM).astype(BF16)
    wkv3 = w_kv_up[l].reshape(KV_LORA, MLA_HEADS, NOPE_DIM + V_DIM)
    wkv = jnp.concatenate([
        _pad_heads(wkv3[:, :, :NOPE_DIM].reshape(KV_LORA, -1), MLA_HEADS, NOPE_DIM),
        wkv3[:, :, NOPE_DIM:].reshape(KV_LORA, -1)], axis=1).astype(BF16)

    memkv = _mem_kv(mem.reshape(-1, D_MODEL), row2(mem_norm_g[l]), w_xkv[l].astype(BF16))
    memkv = memkv.reshape(B, -1, 2 * D_MODEL)

    qT, k, vT, ga, gyb = _in_proj(
        h, tables, row2(mix_norm_g[l]), win, row2(q_norm_g[l]), wq, row2(kv_norm_g[l]), wkv,
        pool_w[l].astype(BF16), row2(pool_scale[l]), w_pool_branch[l].astype(BF16), tm_proj)
    attn = _mla_attention(qT, k, vT)

    w_r = jnp.zeros((D_MODEL, 2 * LANES), F32)
    w_r = w_r.at[:, :N_EXPERTS].set(w_router_expert[l]).at[:, LANES:LANES + N_GROUPS].set(w_router_group[l])
    wr_hi = w_r.astype(BF16)
    wr_lo = (w_r - wr_hi.astype(F32)).astype(BF16)
    rb = jnp.zeros((1, 2 * LANES), F32)
    rb = rb.at[0, :N_EXPERTS].set(b_router_expert[l]).at[0, LANES:LANES + N_GROUPS].set(b_router_group[l])

    h2, xn, info, infoT, pieces = _mix_xattn(
        h, attn, ga, gyb, memkv, w_attn_branch[l].astype(BF16), w_mix_out[l].astype(BF16),
        row2(xattn_norm_g[l]), (w_xq[l] * (1.0 / math.sqrt(MEM_HEAD_DIM))).astype(BF16), w_xo[l].astype(BF16),
        row2(ffn_norm_g[l]), wr_hi, wr_lo, rb)

    T = B * S
    n_chunks = T // MOE_CHUNK
    tile_pieces = EXPERT_TILE // PIECE
    max_pieces = 2 * T // PIECE + n_chunks * N_EXPERTS + N_EXPERTS * tile_pieces
    rows_max = -(-max_pieces // tile_pieces) * EXPERT_TILE
    dst, n_local, tile_expert, tile_valid, n_used, gaps, fill = _routing_tables(
        pieces[:, :, 0].astype(jnp.int32), rows_max)

    xs = _dispatch(xn.reshape(T, D_MODEL), infoT, dst, n_local, gaps, fill, rows_max)
    ys = _experts(xs, w_exp_gate[l], w_exp_up[l], w_exp_down[l], tile_expert, tile_valid, n_used)
    out = _combine(h2.reshape(T, D_MODEL), info.reshape(T, LANES), row2(out_g), ys, dst, n_local)
    return out.reshape(B, S, D_MODEL)


def kernel(x, mem, positions, mix_norm_g, w_in, q_norm_g, w_q_up, kv_norm_g, w_kv_up, w_attn_branch, pool_w, pool_scale, w_pool_branch, w_mix_out, xattn_norm_g, mem_norm_g, w_xq, w_xkv, w_xo, ffn_norm_g, w_router_group, b_router_group, w_router_expert, b_router_expert, w_exp_gate, w_exp_up, w_exp_down, final_norm_g):
    depth = w_in.shape[0]
    assert depth == 1, "the combine kernel fuses the final RMSNorm, which is only valid after the last layer"
    assert x.shape[1] % Q_TILE == 0 and x.shape[1] % MOE_CHUNK == 0
    tables = _rope_tables(positions)
    return _layer(0, x, mem, tables, mix_norm_g, w_in, q_norm_g, w_q_up, kv_norm_g, w_kv_up, w_attn_branch,
                  pool_w, pool_scale, w_pool_branch, w_mix_out, xattn_norm_g, mem_norm_g, w_xq, w_xkv, w_xo,
                  ffn_norm_g, w_router_group, b_router_group, w_router_expert, b_router_expert,
                  w_exp_gate, w_exp_up, w_exp_down, final_norm_g, IN_PROJ_TILE)
```

```python
import functools
import math

import jax
import jax.numpy as jnp
from jax import lax
from jax.experimental import pallas as pl
from jax.experimental.pallas import tpu as pltpu

F32 = jnp.float32
BF16 = jnp.bfloat16

D_MODEL = 1024
CHUNK = 64
MLA_HEADS = 8
Q_LORA = 384
KV_LORA = 256
NOPE_DIM = 64
ROPE_DIM = 32
V_DIM = 64
ROPE_THETA = 10000.0
POOL_WIDTH = 512
POOL_WINDOWS = (2, 4, 8, 16)
POOL_GROUP_DIM = POOL_WIDTH // len(POOL_WINDOWS)
POOL_HALO = 16
MEM_HEADS = 4
MEM_HEAD_DIM = D_MODEL // MEM_HEADS
N_GROUPS = 4
EXPERTS_PER_GROUP = 8
N_EXPERTS = N_GROUPS * EXPERTS_PER_GROUP
EXPERT_FF = 256
EPS = 1e-6

LANES = 128
HEAD_PAD = LANES
Q_TILE = 512
KV_TILE = 512
KEY_SUB = 128
V_ROWS = V_DIM + 16
IN_PROJ_TILE = 512
MOE_CHUNK = 512
PIECE = 16
MAX_PIECES = 2 * MOE_CHUNK // PIECE + N_EXPERTS
CHUNK_ROWS = MAX_PIECES * PIECE
EXPERT_TILE = 512
EXPERT_SUB = 256
PROJ_SUB = 256
MIX_SUB = 256
QK_AHEAD = 3
PV_BEHIND = 2

_C_Q = 0
_C_KV = _C_Q + Q_LORA
_C_KR = _C_KV + KV_LORA
_C_POOL = _C_KR + LANES
_C_GA = _C_POOL + POOL_WIDTH
_C_GB = _C_GA + D_MODEL
_C_END = _C_GB + D_MODEL

VMEM_LIMIT = 56 * 1024 * 1024


def _rms(x, g):
    return x * lax.rsqrt(jnp.mean(x * x, axis=-1, keepdims=True) + EPS) * g


def _dot(a, b):
    return jnp.dot(a, b, preferred_element_type=F32)


def _dot_nt(a, b):
    return lax.dot_general(a, b, (((1,), (1,)), ((), ())), preferred_element_type=F32)


def _mem_kv_kernel(mem_ref, g_ref, w_ref, kv_ref):
    mn = _rms(mem_ref[...], g_ref[...]).astype(BF16)
    kv_ref[...] = _dot(mn, w_ref[...]).astype(BF16)


def _mem_kv(mem2d, g, w_xkv):
    rows = mem2d.shape[0]
    tm = min(512, rows)
    assert rows % tm == 0
    return pl.pallas_call(
        _mem_kv_kernel,
        grid=(rows // tm,),
        in_specs=[
            pl.BlockSpec((tm, D_MODEL), lambda i: (i, 0)),
            pl.BlockSpec((1, D_MODEL), lambda i: (0, 0)),
            pl.BlockSpec((D_MODEL, 2 * D_MODEL), lambda i: (0, 0)),
        ],
        out_specs=pl.BlockSpec((tm, 2 * D_MODEL), lambda i: (i, 0)),
        out_shape=jax.ShapeDtypeStruct((rows, 2 * D_MODEL), BF16),
        compiler_params=pltpu.CompilerParams(vmem_limit_bytes=VMEM_LIMIT),
        name="mem_kv",
    )(mem2d, g, w_xkv)


def _rope(t, c, sa, sb):
    w = t.shape[-1]
    return t * c + pltpu.roll(t, ROPE_DIM // 2, 1) * sa + pltpu.roll(t, w - ROPE_DIM // 2, 1) * sb


def _in_proj_kernel(x_ref, rot_ref, g_ref, win_ref, qg_ref, wq_ref, kvg_ref, wkv_ref,
                    poolw_ref, pscale_ref, wpb_ref,
                    qT_out, k_out, vT_out, ga_out, gyb_out, hist_ref):
    tm = x_ref.shape[0]
    i = pl.program_id(1)
    blocks = [slice(b * PROJ_SUB, (b + 1) * PROJ_SUB) for b in range(tm // PROJ_SUB)]
    rows = lambda parts: jnp.concatenate(parts, axis=0)
    hn = [_rms(x_ref[r, :], g_ref[...]).astype(BF16) for r in blocks]
    proj = lambda lo, hi: [_dot(v, win_ref[:, lo:hi]) for v in hn]

    cos8 = rot_ref[0:LANES, :].T
    sin8 = rot_ref[LANES:, :].T
    lane = lax.broadcasted_iota(jnp.int32, cos8.shape, 1)
    half = ROPE_DIM // 2
    rope_lo = NOPE_DIM
    c1 = jnp.where(lane < rope_lo + ROPE_DIM, cos8, 0.0)
    sa1 = jnp.where((lane >= rope_lo + half) & (lane < rope_lo + ROPE_DIM), sin8, 0.0)
    sb1 = jnp.where((lane >= rope_lo) & (lane < rope_lo + half), -sin8, 0.0)

    q_lat = proj(_C_Q, _C_KV)
    kv_lat = proj(_C_KV, _C_KR)
    qn = [_rms(v, qg_ref[...]).astype(BF16) for v in q_lat]
    kvn = [_rms(v, kvg_ref[...]).astype(BF16) for v in kv_lat]
    q = rows([_dot(v, wq_ref[...]) for v in qn])
    n_nope = MLA_HEADS * NOPE_DIM
    x1 = q[:, n_nope:n_nope + LANES]
    x2 = q[:, n_nope + LANES:]
    qT = jnp.concatenate([q[:, :n_nope], x1 * cos8 - x2 * sin8, x2 * cos8 + x1 * sin8], axis=1).T
    pad = jnp.zeros((HEAD_PAD - NOPE_DIM - ROPE_DIM, tm), F32)
    qT = jnp.concatenate(
        [blk for h in range(MLA_HEADS) for blk in (
            qT[h * NOPE_DIM:(h + 1) * NOPE_DIM, :],
            qT[n_nope + h * half:n_nope + (h + 1) * half, :],
            qT[n_nope + LANES + h * half:n_nope + LANES + (h + 1) * half, :], pad)], axis=0).astype(BF16)

    k_nope = rows([_dot(v, wkv_ref[:, 0:MLA_HEADS * HEAD_PAD]) for v in kvn])
    kr = _rope(rows(proj(_C_KR, _C_POOL)), c1, sa1, sb1)
    k_out[...] = (k_nope + jnp.tile(kr, (1, MLA_HEADS))).astype(BF16)
    v = rows([_dot(t, wkv_ref[:, MLA_HEADS * HEAD_PAD:]) for t in kvn])
    for t in range(tm // Q_TILE):
        qT_out[t] = qT[:, t * Q_TILE:(t + 1) * Q_TILE]
    vT = v.T
    ones = jnp.ones((V_ROWS - V_DIM, tm), F32)
    vT = jnp.concatenate(
        [blk for h in range(MLA_HEADS) for blk in (vT[h * V_DIM:(h + 1) * V_DIM, :], ones)], axis=0).astype(BF16)
    for t in range(tm // KV_TILE):
        vT_out[t] = vT[:, t * KV_TILE:(t + 1) * KV_TILE]

    u = rows(proj(_C_POOL, _C_GA))

    @pl.when(i == 0)
    def _():
        hist_ref[...] = jnp.zeros_like(hist_ref)

    ext = jnp.concatenate([hist_ref[...], u], axis=0)
    hist_ref[...] = u[tm - POOL_HALO:, :]
    t_idx = i * tm + lax.broadcasted_iota(jnp.int32, (tm, POOL_GROUP_DIM), 0)
    ys = []
    for g, w in enumerate(POOL_WINDOWS):
        c0 = g * POOL_GROUP_DIM
        run = ext[:, c0:c0 + POOL_GROUP_DIM]
        span = 1
        while span < w:
            run = run + pltpu.roll(run, span, 0)
            span *= 2
        cnt = jnp.minimum(t_idx + 1, w).astype(F32)
        d = run[POOL_HALO:, :] / cnt - u[:, c0:c0 + POOL_GROUP_DIM]
        ys.append(_dot(d.astype(BF16), poolw_ref[g]))
    y = (jnp.concatenate(ys, axis=1) * pscale_ref[...]).astype(BF16)
    y_b = [_dot(y[r, :], wpb_ref[...]) for r in blocks]

    for r, t in zip(blocks, proj(_C_GA, _C_GB)):
        ga_out[r, :] = jax.nn.sigmoid(t).astype(BF16)
    for r, t, yb in zip(blocks, proj(_C_GB, _C_END), y_b):
        gyb_out[r, :] = (jax.nn.sigmoid(t) * yb).astype(BF16)


def _in_proj(x, rot, g, win, qg, wq, kvg, wkv, poolw, pscale, wpb, tm):
    B, S, _ = x.shape
    row = lambda b, i: (b, i, 0)
    const2 = lambda b, i: (0, 0)
    const3 = lambda b, i: (0, 0, 0)
    slab = lambda b, i: (b, i, 0, 0)
    return pl.pallas_call(
        _in_proj_kernel,
        grid=(B, S // tm),
        in_specs=[
            pl.BlockSpec((None, tm, D_MODEL), row),
            pl.BlockSpec((None, 2 * LANES, tm), lambda b, i: (b, 0, i)),
            pl.BlockSpec((1, D_MODEL), const2),
            pl.BlockSpec(win.shape, const2),
            pl.BlockSpec((1, Q_LORA), const2),
            pl.BlockSpec(wq.shape, const2),
            pl.BlockSpec((1, KV_LORA), const2),
            pl.BlockSpec(wkv.shape, const2),
            pl.BlockSpec(poolw.shape, const3),
            pl.BlockSpec((1, POOL_WIDTH), const2),
            pl.BlockSpec(wpb.shape, const2),
        ],
        out_specs=[
            pl.BlockSpec((None, tm // Q_TILE, MLA_HEADS * HEAD_PAD, Q_TILE), slab),
            pl.BlockSpec((None, tm, MLA_HEADS * HEAD_PAD), row),
            pl.BlockSpec((None, tm // KV_TILE, MLA_HEADS * V_ROWS, KV_TILE), slab),
            pl.BlockSpec((None, tm, D_MODEL), row),
            pl.BlockSpec((None, tm, D_MODEL), row),
        ],
        out_shape=[jax.ShapeDtypeStruct((B, S // Q_TILE, MLA_HEADS * HEAD_PAD, Q_TILE), BF16),
                   jax.ShapeDtypeStruct((B, S, MLA_HEADS * HEAD_PAD), BF16),
                   jax.ShapeDtypeStruct((B, S // KV_TILE, MLA_HEADS * V_ROWS, KV_TILE), BF16),
                   jax.ShapeDtypeStruct((B, S, D_MODEL), BF16),
                   jax.ShapeDtypeStruct((B, S, D_MODEL), BF16)],
        scratch_shapes=[pltpu.VMEM((POOL_HALO, POOL_WIDTH), F32)],
        compiler_params=pltpu.CompilerParams(
            dimension_semantics=("arbitrary", "arbitrary"), vmem_limit_bytes=VMEM_LIMIT),
        name="in_proj",
    )(x, rot, g, win, qg, wq, kvg, wkv, poolw, pscale, wpb)


def _mla_kernel(qT_ref, k_ref, vT_ref, o_ref, m_ref, acc_ref):
    i = pl.program_id(1)
    n_sub = KV_TILE // KEY_SUB
    units = [(h, c) for c in range(n_sub) for h in range(MLA_HEADS)]
    qry_c = lax.broadcasted_iota(jnp.int32, (KEY_SUB, Q_TILE), 1) // CHUNK

    def scores(j, h, c):
        hs = slice(h * HEAD_PAD, (h + 1) * HEAD_PAD)
        rows = pl.ds(pl.multiple_of(j * KV_TILE + c * KEY_SUB, KEY_SUB), KEY_SUB)
        return _dot(k_ref[rows, hs], qT_ref[hs, :])

    def fold(h, alpha, pv):
        acc_ref[h] = pv if alpha is None else alpha * acc_ref[h] + pv

    def sweep(j, diag):
        ahead = [scores(j, *u) for u in units[:QK_AHEAD]]
        pending = []
        for n, (h, c) in enumerate(units):
            s = ahead.pop(0)
            if n + QK_AHEAD < len(units):
                ahead.append(scores(j, *units[n + QK_AHEAD]))
            first = False
            if diag is not None:
                key_c0 = (diag * KV_TILE + c * KEY_SUB) // CHUNK
                key_c = key_c0 + lax.broadcasted_iota(jnp.int32, (KEY_SUB, Q_TILE), 0) // CHUNK
                s = jnp.where(key_c <= qry_c, s, -jnp.inf)
                first = diag == 0 and c == 0
            s_max = jnp.max(s, axis=0, keepdims=True)
            m_new = s_max if first else jnp.maximum(m_ref[h], s_max)
            p = jnp.exp2(s - m_new).astype(BF16)
            pv = _dot(vT_ref[j, h * V_ROWS:(h + 1) * V_ROWS, c * KEY_SUB:(c + 1) * KEY_SUB], p)
            alpha = None if first else jnp.exp2(m_ref[h] - m_new)
            m_ref[h] = m_new
            pending.append((h, alpha, pv))
            if len(pending) > PV_BEHIND:
                fold(*pending.pop(0))
        for item in pending:
            fold(*item)

    n_diag = Q_TILE // KV_TILE
    for d in range(n_diag):
        sweep(i * n_diag + d, d)

    def body(j, carry):
        sweep(j, None)
        return carry

    lax.fori_loop(0, i * n_diag, body, 0)
    oT = jnp.concatenate([acc_ref[h, :V_DIM, :] / acc_ref[h, V_DIM:V_DIM + 1, :] for h in range(MLA_HEADS)], axis=0)
    o_ref[...] = oT.T.astype(BF16)


def _mla_attention(qT, k, vT):
    B, S, W = k.shape
    return pl.pallas_call(
        _mla_kernel,
        grid=(B, S // Q_TILE),
        in_specs=[
            pl.BlockSpec((None, None, W, Q_TILE), lambda b, i: (b, i, 0, 0)),
            pl.BlockSpec((None, S, W), lambda b, i: (b, 0, 0)),
            pl.BlockSpec((None, S // KV_TILE, MLA_HEADS * V_ROWS, KV_TILE), lambda b, i: (b, 0, 0, 0)),
        ],
        out_specs=pl.BlockSpec((None, Q_TILE, MLA_HEADS * V_DIM), lambda b, i: (b, i, 0)),
        out_shape=jax.ShapeDtypeStruct((B, S, MLA_HEADS * V_DIM), BF16),
        scratch_shapes=[pltpu.VMEM((MLA_HEADS, 1, Q_TILE), F32),
                        pltpu.VMEM((MLA_HEADS, V_ROWS, Q_TILE), F32)],
        compiler_params=pltpu.CompilerParams(
            dimension_semantics=("arbitrary", "arbitrary"), vmem_limit_bytes=VMEM_LIMIT),
        name="mla_attn",
    )(qT, k, vT)


def _route(logits_t, bias_t, tri_upper, tri_lower):
    tm = logits_t.shape[1]
    neg = -jnp.inf
    rg = lax.broadcasted_iota(jnp.int32, (8, tm), 0)
    re = lax.broadcasted_iota(jnp.int32, (N_EXPERTS, tm), 0)
    top = lambda v: jnp.max(v, axis=0, keepdims=True)

    lg = jnp.where(rg < N_GROUPS, logits_t[LANES:LANES + 8, :] + bias_t[LANES:LANES + 8, :], neg)
    ge = jnp.exp(lg - top(lg))
    gp = ge / jnp.sum(ge, axis=0, keepdims=True)
    g_w = top(gp)
    g_idx = jnp.min(jnp.where(gp == g_w, rg, 8), axis=0, keepdims=True)

    sel = re // EXPERTS_PER_GROUP == g_idx
    le = jnp.where(sel, logits_t[:N_EXPERTS, :] + bias_t[:N_EXPERTS, :], neg)
    ee = jnp.exp(le - top(le))
    ep = jnp.where(sel, ee / jnp.sum(ee, axis=0, keepdims=True), -1.0)
    w1 = top(ep)
    i1 = jnp.min(jnp.where(ep == w1, re, N_EXPERTS), axis=0, keepdims=True)
    ep2 = jnp.where(re == i1, -1.0, ep)
    w2 = top(ep2)
    i2 = jnp.min(jnp.where(ep2 == w2, re, N_EXPERTS), axis=0, keepdims=True)
    den = w1 + w2
    c1 = g_w * (w1 / den)
    c2 = g_w * (w2 / den)

    oh1 = (re == i1).astype(F32)
    oh2 = (re == i2).astype(F32)
    both = oh1 + oh2
    earlier = _dot(both.astype(BF16), tri_upper)
    pieces = jnp.floor((jnp.sum(both, axis=1, keepdims=True) + (PIECE - 1)) * (1.0 / PIECE))
    start = _dot(tri_lower, jnp.broadcast_to(pieces, (N_EXPERTS, LANES)).astype(BF16))[:, 0:1] * PIECE
    pos1 = jnp.sum(oh1 * (earlier + start), axis=0, keepdims=True)
    pos2 = jnp.sum(oh2 * (earlier + start), axis=0, keepdims=True)
    info_t = jnp.concatenate([pos1, pos2, c1, c2, jnp.zeros((4, tm), F32)], axis=0)
    return info_t, pieces


def _mix_kernel(x_ref, attn_ref, ga_ref, gyb_ref, kv_ref, wab_ref, wmix_ref, xg_ref, wxq_ref, wxo_ref,
                fg_ref, wr_hi_ref, wr_lo_ref, rb_ref, tri_ref, lower_ref,
                h_out, xn_out, info_out, infoT_out, pieces_out):
    tm = x_ref.shape[0]
    blocks = [slice(b * MIX_SUB, (b + 1) * MIX_SUB) for b in range(tm // MIX_SUB)]
    y_a = [_dot(attn_ref[r, :], wab_ref[...]) for r in blocks]
    merged = [(ga_ref[r, :].astype(F32) * y + gyb_ref[r, :].astype(F32)).astype(BF16) for r, y in zip(blocks, y_a)]
    h1 = [x_ref[r, :] + _dot(m, wmix_ref[...]) for r, m in zip(blocks, merged)]

    hn = [_rms(h, xg_ref[...]).astype(BF16) for h in h1]
    q = [_dot(v, wxq_ref[...]).astype(BF16) for v in hn]
    heads = [[] for _ in blocks]
    for h in range(MEM_HEADS):
        hs = slice(h * MEM_HEAD_DIM, (h + 1) * MEM_HEAD_DIM)
        vs = slice(D_MODEL + h * MEM_HEAD_DIM, D_MODEL + (h + 1) * MEM_HEAD_DIM)
        s = [_dot_nt(v[:, hs], kv_ref[:, hs]) for v in q]
        p = [jnp.exp(v - jnp.max(v, axis=-1, keepdims=True)) for v in s]
        o = [_dot(v.astype(BF16), kv_ref[:, vs]) for v in p]
        for b in range(len(blocks)):
            heads[b].append((o[b] / jnp.sum(p[b], axis=-1, keepdims=True)).astype(BF16))
    h2 = [h + _dot(jnp.concatenate(hd, axis=1), wxo_ref[...]) for h, hd in zip(h1, heads)]

    xn = [_rms(h, fg_ref[...]) for h in h2]
    xn_hi = [v.astype(BF16) for v in xn]
    xn_lo = [(v - hi.astype(F32)).astype(BF16) for v, hi in zip(xn, xn_hi)]
    logits = [_dot(hi, wr_hi_ref[...]) + (_dot(hi, wr_lo_ref[...]) + _dot(lo, wr_hi_ref[...]))
              for hi, lo in zip(xn_hi, xn_lo)]
    for r, h, hi in zip(blocks, h2, xn_hi):
        h_out[r, :] = h
        xn_out[r, :] = hi
    logits_t = jnp.concatenate(logits, axis=0).T
    info_t, pieces = _route(logits_t, rb_ref[...], tri_ref[...], lower_ref[...])
    infoT_out[...] = info_t
    info_out[...] = jnp.concatenate([info_t, jnp.zeros((LANES - 8, tm), F32)], axis=0).T
    pieces_out[...] = jnp.broadcast_to(pieces, (N_EXPERTS, LANES))


def _mix_xattn(x, attn, ga, gyb, memkv, wab, wmix, xg, wxq, wxo, fg, wr_hi, wr_lo, rb):
    B, S, _ = x.shape
    M = memkv.shape[1]
    tm = MOE_CHUNK
    nt = S // tm
    row = lambda b, i: (b, i, 0)
    const2 = lambda b, i: (0, 0)
    tri = (lax.broadcasted_iota(jnp.int32, (tm, tm), 0) < lax.broadcasted_iota(jnp.int32, (tm, tm), 1)).astype(BF16)
    lower = (lax.broadcasted_iota(jnp.int32, (N_EXPERTS, N_EXPERTS), 1)
             < lax.broadcasted_iota(jnp.int32, (N_EXPERTS, N_EXPERTS), 0)).astype(BF16)
    rb = jnp.broadcast_to(rb.reshape(2 * LANES, 1), (2 * LANES, tm))
    return pl.pallas_call(
        _mix_kernel,
        grid=(B, S // tm),
        in_specs=[
            pl.BlockSpec((None, tm, D_MODEL), row),
            pl.BlockSpec((None, tm, MLA_HEADS * V_DIM), row),
            pl.BlockSpec((None, tm, D_MODEL), row),
            pl.BlockSpec((None, tm, D_MODEL), row),
            pl.BlockSpec((None, M, 2 * D_MODEL), lambda b, i: (b, 0, 0)),
            pl.BlockSpec(wab.shape, const2),
            pl.BlockSpec(wmix.shape, const2),
            pl.BlockSpec((1, D_MODEL), const2),
            pl.BlockSpec(wxq.shape, const2),
            pl.BlockSpec(wxo.shape, const2),
            pl.BlockSpec((1, D_MODEL), const2),
            pl.BlockSpec(wr_hi.shape, const2),
            pl.BlockSpec(wr_lo.shape, const2),
            pl.BlockSpec((2 * LANES, tm), const2),
            pl.BlockSpec((tm, tm), const2),
            pl.BlockSpec((N_EXPERTS, N_EXPERTS), const2),
        ],
        out_specs=[
            pl.BlockSpec((None, tm, D_MODEL), row),
            pl.BlockSpec((None, tm, D_MODEL), row),
            pl.BlockSpec((None, tm, LANES), row),
            pl.BlockSpec((8, tm), lambda b, i: (0, b * nt + i)),
            pl.BlockSpec((None, N_EXPERTS, LANES), lambda b, i: (b * nt + i, 0, 0)),
        ],
        out_shape=[jax.ShapeDtypeStruct((B, S, D_MODEL), F32),
                   jax.ShapeDtypeStruct((B, S, D_MODEL), BF16),
                   jax.ShapeDtypeStruct((B, S, LANES), F32),
                   jax.ShapeDtypeStruct((8, B * S), F32),
                   jax.ShapeDtypeStruct((B * nt, N_EXPERTS, LANES), F32)],
        compiler_params=pltpu.CompilerParams(
            dimension_semantics=("arbitrary", "arbitrary"), vmem_limit_bytes=VMEM_LIMIT),
        name="mix_xattn",
    )(x, attn, ga, gyb, memkv, wab, wmix, xg, wxq, wxo, fg, wr_hi, wr_lo, rb, tri, lower)


def _piece_copy(src_ref, dst_ref, sem):
    return pltpu.make_async_copy(src_ref, dst_ref, sem)


def _dispatch_kernel(dst_ref, np_ref, gap_ref, fill_ref, xn_ref, infoT_ref, xs_hbm, buf_ref, zero_ref, sem_ref):
    c = pl.program_id(0)
    n = pl.num_programs(0)
    slot = c % 2

    n_tiles = xs_hbm.shape[0] // EXPERT_TILE

    def gap_copy(g):
        return _piece_copy(zero_ref.at[pl.ds(0, PIECE)],
                           xs_hbm.at[pl.ds(pl.multiple_of(gap_ref[g] * PIECE, PIECE), PIECE)], sem_ref.at[2])

    def tail_copy(t):
        return _piece_copy(zero_ref, xs_hbm.at[pl.ds(pl.multiple_of(t * EXPERT_TILE, EXPERT_TILE), EXPERT_TILE)],
                           sem_ref.at[2])

    @pl.when(c == 0)
    def _():
        zero_ref[...] = jnp.zeros_like(zero_ref)
        lax.fori_loop(0, fill_ref[0], lambda g, carry: (gap_copy(g).start(), carry)[1], 0)
        lax.fori_loop(fill_ref[1], n_tiles, lambda t, carry: (tail_copy(t).start(), carry)[1], 0)

    def copy(cc, s, q):
        return _piece_copy(buf_ref.at[s, pl.ds(pl.multiple_of(q * PIECE, PIECE), PIECE)],
                           xs_hbm.at[pl.ds(pl.multiple_of(dst_ref[cc * MAX_PIECES + q] * PIECE, PIECE), PIECE)],
                           sem_ref.at[s])

    def start_all(cc, s):
        lax.fori_loop(0, np_ref[cc], lambda q, carry: (copy(cc, s, q).start(), carry)[1], 0)

    def wait_all(cc, s):
        lax.fori_loop(0, np_ref[cc], lambda q, carry: (copy(cc, s, q).wait(), carry)[1], 0)

    @pl.when(c >= 2)
    def _():
        wait_all(c - 2, slot)

    pos1 = infoT_ref[0:1, :]
    pos2 = infoT_ref[1:2, :]
    r = lax.broadcasted_iota(jnp.int32, (CHUNK_ROWS, MOE_CHUNK), 0).astype(F32)
    onehot = jnp.where((r == pos1) | (r == pos2), 1.0, 0.0).astype(BF16)
    buf_ref[slot] = _dot(onehot, xn_ref[...]).astype(BF16)
    start_all(c, slot)

    @pl.when(c == n - 1)
    def _():
        @pl.when(c >= 1)
        def _():
            wait_all(c - 1, 1 - slot)
        wait_all(c, slot)
        lax.fori_loop(0, fill_ref[0], lambda g, carry: (gap_copy(g).wait(), carry)[1], 0)
        lax.fori_loop(fill_ref[1], n_tiles, lambda t, carry: (tail_copy(t).wait(), carry)[1], 0)


def _dispatch(xn, infoT, dst, npc, gaps, fill, rows_max):
    T = xn.shape[0]
    grid_spec = pltpu.PrefetchScalarGridSpec(
        num_scalar_prefetch=4,
        grid=(T // MOE_CHUNK,),
        in_specs=[
            pl.BlockSpec((MOE_CHUNK, D_MODEL), lambda c, *_: (c, 0)),
            pl.BlockSpec((8, MOE_CHUNK), lambda c, *_: (0, c)),
        ],
        out_specs=pl.BlockSpec(memory_space=pl.ANY),
        scratch_shapes=[pltpu.VMEM((2, CHUNK_ROWS, D_MODEL), BF16), pltpu.VMEM((EXPERT_TILE, D_MODEL), BF16),
                        pltpu.SemaphoreType.DMA((3,))],
    )
    return pl.pallas_call(
        _dispatch_kernel,
        grid_spec=grid_spec,
        out_shape=jax.ShapeDtypeStruct((rows_max, D_MODEL), BF16),
        compiler_params=pltpu.CompilerParams(
            dimension_semantics=("arbitrary",), vmem_limit_bytes=VMEM_LIMIT),
        name="moe_dispatch",
    )(dst, npc, gaps, fill, xn, infoT)


def _expert_kernel(te_ref, tv_ref, nu_ref, x_ref, wg_ref, wu_ref, wd_ref, y_ref, wgu_bf, wd_bf):
    i = pl.program_id(0)

    @pl.when(i < nu_ref[0])
    def _():
        @pl.when((i == 0) | (te_ref[i] != te_ref[jnp.maximum(i - 1, 0)]))
        def _():
            wgu_bf[:, :EXPERT_FF] = wg_ref[...].astype(BF16)
            wgu_bf[:, EXPERT_FF:] = wu_ref[...].astype(BF16)
            wd_bf[...] = wd_ref[...].astype(BF16)

        blocks = [slice(b * EXPERT_SUB, (b + 1) * EXPERT_SUB) for b in range(EXPERT_TILE // EXPERT_SUB)]
        row = lax.broadcasted_iota(jnp.int32, (EXPERT_SUB, D_MODEL), 0)
        gus = []
        for b, rows in enumerate(blocks):
            x = x_ref[rows, :]
            x = jnp.where(row < tv_ref[i] - b * EXPERT_SUB, x, jnp.zeros_like(x))
            gus.append(_dot(x, wgu_bf[...]))
        hids = []
        for gu in gus:
            gate = gu[:, :EXPERT_FF]
            hids.append((gate * jax.nn.sigmoid(gate) * gu[:, EXPERT_FF:]).astype(BF16))
        for rows, hid in zip(blocks, hids):
            y_ref[rows, :] = _dot(hid, wd_bf[...]).astype(BF16)

    @pl.when(i >= nu_ref[0])
    def _():
        y_ref[...] = jnp.zeros_like(y_ref)


def _experts(xs, w_gate, w_up, w_down, tile_expert, tile_valid, n_used):
    rows_max = xs.shape[0]
    last = lambda i, nu: jnp.minimum(i, nu[0] - 1)
    expert = lambda i, te, tv, nu: (te[last(i, nu)], 0, 0)
    grid_spec = pltpu.PrefetchScalarGridSpec(
        num_scalar_prefetch=3,
        grid=(rows_max // EXPERT_TILE,),
        in_specs=[
            pl.BlockSpec((EXPERT_TILE, D_MODEL), lambda i, te, tv, nu: (last(i, nu), 0)),
            pl.BlockSpec((None, D_MODEL, EXPERT_FF), expert),
            pl.BlockSpec((None, D_MODEL, EXPERT_FF), expert),
            pl.BlockSpec((None, EXPERT_FF, D_MODEL), expert),
        ],
        out_specs=pl.BlockSpec((EXPERT_TILE, D_MODEL), lambda i, te, tv, nu: (i, 0)),
        scratch_shapes=[pltpu.VMEM((D_MODEL, 2 * EXPERT_FF), BF16), pltpu.VMEM((EXPERT_FF, D_MODEL), BF16)],
    )
    return pl.pallas_call(
        _expert_kernel,
        grid_spec=grid_spec,
        out_shape=jax.ShapeDtypeStruct((rows_max, D_MODEL), BF16),
        compiler_params=pltpu.CompilerParams(
            dimension_semantics=("arbitrary",), vmem_limit_bytes=VMEM_LIMIT),
        name="moe_experts",
    )(tile_expert, tile_valid, n_used, xs, w_gate, w_up, w_down)


def _combine_kernel(dst_ref, np_ref, h_ref, info_ref, fg_ref, ys_hbm, o_ref, buf_ref, sem_ref):
    c = pl.program_id(0)
    n = pl.num_programs(0)
    slot = c % 2

    def copy(cc, s, q):
        return _piece_copy(ys_hbm.at[pl.ds(pl.multiple_of(dst_ref[cc * MAX_PIECES + q] * PIECE, PIECE), PIECE)],
                           buf_ref.at[s, pl.ds(pl.multiple_of(q * PIECE, PIECE), PIECE)],
                           sem_ref.at[s])

    def start_all(cc, s):
        lax.fori_loop(0, np_ref[cc], lambda q, carry: (copy(cc, s, q).start(), carry)[1], 0)

    def wait_all(cc, s):
        lax.fori_loop(0, np_ref[cc], lambda q, carry: (copy(cc, s, q).wait(), carry)[1], 0)

    @pl.when(c == 0)
    def _():
        buf_ref[...] = jnp.zeros_like(buf_ref)
        start_all(0, 0)

    @pl.when(c + 1 < n)
    def _():
        start_all(c + 1, 1 - slot)

    wait_all(c, slot)
    info = info_ref[...]
    r = lax.broadcasted_iota(jnp.int32, (MOE_CHUNK, CHUNK_ROWS), 1).astype(F32)
    weights = jnp.where(r == info[:, 0:1], info[:, 2:3], 0.0) + jnp.where(r == info[:, 1:2], info[:, 3:4], 0.0)
    moe = _dot(weights.astype(BF16), buf_ref[slot])
    o_ref[...] = _rms(h_ref[...] + moe, fg_ref[...])


def _combine(h2, info, fg, ys, dst, npc):
    T = h2.shape[0]
    grid_spec = pltpu.PrefetchScalarGridSpec(
        num_scalar_prefetch=2,
        grid=(T // MOE_CHUNK,),
        in_specs=[
            pl.BlockSpec((MOE_CHUNK, D_MODEL), lambda c, dst, npc: (c, 0)),
            pl.BlockSpec((MOE_CHUNK, LANES), lambda c, dst, npc: (c, 0)),
            pl.BlockSpec((1, D_MODEL), lambda c, dst, npc: (0, 0)),
            pl.BlockSpec(memory_space=pl.ANY),
        ],
        out_specs=pl.BlockSpec((MOE_CHUNK, D_MODEL), lambda c, dst, npc: (c, 0)),
        scratch_shapes=[pltpu.VMEM((2, CHUNK_ROWS, D_MODEL), BF16), pltpu.SemaphoreType.DMA((2,))],
    )
    return pl.pallas_call(
        _combine_kernel,
        grid_spec=grid_spec,
        out_shape=jax.ShapeDtypeStruct((T, D_MODEL), F32),
        compiler_params=pltpu.CompilerParams(
            dimension_semantics=("arbitrary",), vmem_limit_bytes=VMEM_LIMIT),
        name="moe_combine",
    )(dst, npc, h2, info, fg, ys)


def _routing_tables(pieces, rows_max):
    tile_pieces = EXPERT_TILE // PIECE
    total = jnp.sum(pieces, axis=0)
    total_al = (total + tile_pieces - 1) // tile_pieces * tile_pieces
    seg_end = jnp.cumsum(total_al)
    seg_start = seg_end - total_al
    chunk_off = jnp.cumsum(pieces, axis=0) - pieces
    loc_end = jnp.cumsum(pieces, axis=1)
    loc_start = loc_end - pieces
    q = jnp.arange(MAX_PIECES, dtype=jnp.int32)
    owner = jnp.sum((q[None, :, None] >= loc_end[:, None, :]).astype(jnp.int32), axis=-1)
    owner = jnp.minimum(owner, N_EXPERTS - 1)
    experts = jnp.arange(N_EXPERTS, dtype=jnp.int32)
    is_owner = (owner[:, :, None] == experts).astype(jnp.int32)
    offset = seg_start[None, :] + chunk_off - loc_start
    dst = jnp.sum(is_owner * offset[:, None, :], axis=-1) + q[None, :]
    n_local = loc_end[:, -1]
    dst = jnp.where(q[None, :] < n_local[:, None], dst, 0)

    t0 = jnp.arange(rows_max // EXPERT_TILE, dtype=jnp.int32) * tile_pieces
    tile_expert = jnp.minimum(jnp.sum((t0[:, None] >= seg_end[None, :]).astype(jnp.int32), axis=-1), N_EXPERTS - 1)
    copies_end = jnp.sum((tile_expert[:, None] == experts).astype(jnp.int32) * (seg_start + total)[None, :], axis=-1)
    tile_valid = jnp.clip((copies_end - t0) * PIECE, 0, EXPERT_TILE)
    n_used = (seg_end[-1] // tile_pieces).reshape(1)

    k = jnp.arange(tile_pieces, dtype=jnp.int32)
    is_gap = (k[None, :] < (total_al - total)[:, None]).reshape(-1)
    gap_piece = (seg_start + total)[:, None] + k[None, :]
    order = jnp.argsort(jnp.logical_not(is_gap), stable=True)
    gaps = gap_piece.reshape(-1)[order]
    fill = jnp.stack([jnp.sum(is_gap.astype(jnp.int32)), n_used[0]])
    i32 = lambda a: a.astype(jnp.int32)
    return i32(dst.reshape(-1)), i32(n_local), i32(tile_expert), i32(tile_valid), i32(n_used), i32(gaps), i32(fill)


def _rope_tables(positions):
    inv_freq = 1.0 / (ROPE_THETA ** (jnp.arange(0, ROPE_DIM, 2, dtype=F32) / ROPE_DIM))
    ang = positions.astype(F32)[:, None, :] * inv_freq[None, :, None]
    cos, sin = jnp.cos(ang), jnp.sin(ang)
    return jnp.concatenate([cos] * MLA_HEADS + [sin] * MLA_HEADS, axis=1)


def _pad_heads(w, heads, width):
    k = w.shape[0]
    w = w.reshape(k, heads, width)
    w = jnp.pad(w, ((0, 0), (0, 0), (0, HEAD_PAD - width)))
    return w.reshape(k, heads * HEAD_PAD)


def _layer(l, h, mem, tables, mix_norm_g, w_in, q_norm_g, w_q_up, kv_norm_g, w_kv_up, w_attn_branch,
           pool_w, pool_scale, w_pool_branch, w_mix_out, xattn_norm_g, mem_norm_g, w_xq, w_xkv, w_xo,
           ffn_norm_g, w_router_group, b_router_group, w_router_expert, b_router_expert,
           w_exp_gate, w_exp_up, w_exp_down, out_g, tm_proj):
    B, S, _ = h.shape
    row2 = lambda v: v.reshape(1, -1).astype(F32)

    wi = w_in[l]
    kr_cols = jnp.pad(wi[:, Q_LORA + KV_LORA:Q_LORA + KV_LORA + ROPE_DIM],
                      ((0, 0), (NOPE_DIM, LANES - NOPE_DIM - ROPE_DIM)))
    win = jnp.concatenate([wi[:, :Q_LORA + KV_LORA], kr_cols, wi[:, Q_LORA + KV_LORA + ROPE_DIM:]], axis=1).astype(BF16)
    scale = math.log2(math.e) / math.sqrt(NOPE_DIM + ROPE_DIM)
    wq3 = (w_q_up[l] * scale).reshape(Q_LORA, MLA_HEADS, NOPE_DIM + ROPE_DIM)
    half = ROPE_DIM // 2
    wq = jnp.concatenate([wq3[:, :, :NOPE_DIM].reshape(Q_LORA, -1),
                          wq3[:, :, NOPE_DIM:NOPE_DIM + half].reshape(Q_LORA, -1),
                          wq3[:, :, NOPE_DIM + half:].reshape(Q_LORA, -1)], axis=1).astype(BF16)
    wkv3 = w_kv_up[l].reshape(KV_LORA, MLA_HEADS, NOPE_DIM + V_DIM)
    wkv = jnp.concatenate([
        _pad_heads(wkv3[:, :, :NOPE_DIM].reshape(KV_LORA, -1), MLA_HEADS, NOPE_DIM),
        wkv3[:, :, NOPE_DIM:].reshape(KV_LORA, -1)], axis=1).astype(BF16)

    memkv = _mem_kv(mem.reshape(-1, D_MODEL), row2(mem_norm_g[l]), w_xkv[l].astype(BF16))
    memkv = memkv.reshape(B, -1, 2 * D_MODEL)

    qT, k, vT, ga, gyb = _in_proj(
        h, tables, row2(mix_norm_g[l]), win, row2(q_norm_g[l]), wq, row2(kv_norm_g[l]), wkv,
        pool_w[l].astype(BF16), row2(pool_scale[l]), w_pool_branch[l].astype(BF16), tm_proj)
    attn = _mla_attention(qT, k, vT)

    w_r = jnp.zeros((D_MODEL, 2 * LANES), F32)
    w_r = w_r.at[:, :N_EXPERTS].set(w_router_expert[l]).at[:, LANES:LANES + N_GROUPS].set(w_router_group[l])
    wr_hi = w_r.astype(BF16)
    wr_lo = (w_r - wr_hi.astype(F32)).astype(BF16)
    rb = jnp.zeros((1, 2 * LANES), F32)
    rb = rb.at[0, :N_EXPERTS].set(b_router_expert[l]).at[0, LANES:LANES + N_GROUPS].set(b_router_group[l])

    h2, xn, info, infoT, pieces = _mix_xattn(
        h, attn, ga, gyb, memkv, w_attn_branch[l].astype(BF16), w_mix_out[l].astype(BF16),
        row2(xattn_norm_g[l]), (w_xq[l] * (1.0 / math.sqrt(MEM_HEAD_DIM))).astype(BF16), w_xo[l].astype(BF16),
        row2(ffn_norm_g[l]), wr_hi, wr_lo, rb)

    T = B * S
    n_chunks = T // MOE_CHUNK
    tile_pieces = EXPERT_TILE // PIECE
    max_pieces = 2 * T // PIECE + n_chunks * N_EXPERTS + N_EXPERTS * tile_pieces
    rows_max = -(-max_pieces // tile_pieces) * EXPERT_TILE
    dst, n_local, tile_expert, tile_valid, n_used, gaps, fill = _routing_tables(
        pieces[:, :, 0].astype(jnp.int32), rows_max)

    xs = _dispatch(xn.reshape(T, D_MODEL), infoT, dst, n_local, gaps, fill, rows_max)
    ys = _experts(xs, w_exp_gate[l], w_exp_up[l], w_exp_down[l], tile_expert, tile_valid, n_used)
    out = _combine(h2.reshape(T, D_MODEL), info.reshape(T, LANES), row2(out_g), ys, dst, n_local)
    return out.reshape(B, S, D_MODEL)


def kernel(x, mem, positions, mix_norm_g, w_in, q_norm_g, w_q_up, kv_norm_g, w_kv_up, w_attn_branch, pool_w, pool_scale, w_pool_branch, w_mix_out, xattn_norm_g, mem_norm_g, w_xq, w_xkv, w_xo, ffn_norm_g, w_router_group, b_router_group, w_router_expert, b_router_expert, w_exp_gate, w_exp_up, w_exp_down, final_norm_g):
    depth = w_in.shape[0]
    assert depth == 1, "the combine kernel fuses the final RMSNorm, which is only valid after the last layer"
    assert x.shape[1] % Q_TILE == 0 and x.shape[1] % MOE_CHUNK == 0
    tables = _rope_tables(positions)
    return _layer(0, x, mem, tables, mix_norm_g, w_in, q_norm_g, w_q_up, kv_norm_g, w_kv_up, w_attn_branch,
                  pool_w, pool_scale, w_pool_branch, w_mix_out, xattn_norm_g, mem_norm_g, w_xq, w_xkv, w_xo,
                  ffn_norm_g, w_router_group, b_router_group, w_router_expert, b_router_expert,
                  w_exp_gate, w_exp_up, w_exp_down, final_norm_g, IN_PROJ_TILE)
```

```python
import functools
import math

import jax
import jax.numpy as jnp
from jax import lax
from jax.experimental import pallas as pl
from jax.experimental.pallas import tpu as pltpu

F32 = jnp.float32
BF16 = jnp.bfloat16

D_MODEL = 1024
CHUNK = 64
MLA_HEADS = 8
Q_LORA = 384
KV_LORA = 256
NOPE_DIM = 64
ROPE_DIM = 32
V_DIM = 64
ROPE_THETA = 10000.0
POOL_WIDTH = 512
POOL_WINDOWS = (2, 4, 8, 16)
POOL_GROUP_DIM = POOL_WIDTH // len(POOL_WINDOWS)
POOL_HALO = 16
MEM_HEADS = 4
MEM_HEAD_DIM = D_MODEL // MEM_HEADS
N_GROUPS = 4
EXPERTS_PER_GROUP = 8
N_EXPERTS = N_GROUPS * EXPERTS_PER_GROUP
EXPERT_FF = 256
EPS = 1e-6

LANES = 128
HEAD_PAD = LANES
Q_TILE = 512
KV_TILE = 512
KEY_SUB = 128
V_ROWS = V_DIM + 16
IN_PROJ_TILE = 512
MOE_CHUNK = 512
PIECE = 16
MAX_PIECES = 2 * MOE_CHUNK // PIECE + N_EXPERTS
CHUNK_ROWS = MAX_PIECES * PIECE
EXPERT_TILE = 512
EXPERT_SUB = 256
PROJ_SUB = 256
MIX_SUB = 256
QK_AHEAD = 3
PV_BEHIND = 2

_C_Q = 0
_C_KV = _C_Q + Q_LORA
_C_KR = _C_KV + KV_LORA
_C_POOL = _C_KR + LANES
_C_GA = _C_POOL + POOL_WIDTH
_C_GB = _C_GA + D_MODEL
_C_END = _C_GB + D_MODEL

VMEM_LIMIT = 56 * 1024 * 1024


def _rms(x, g):
    return x * lax.rsqrt(jnp.mean(x * x, axis=-1, keepdims=True) + EPS) * g


def _dot(a, b):
    return jnp.dot(a, b, preferred_element_type=F32)


def _dot_nt(a, b):
    return lax.dot_general(a, b, (((1,), (1,)), ((), ())), preferred_element_type=F32)


def _mem_kv_kernel(mem_ref, g_ref, w_ref, kv_ref):
    mn = _rms(mem_ref[...], g_ref[...]).astype(BF16)
    kv_ref[...] = _dot(mn, w_ref[...]).astype(BF16)


def _mem_kv(mem2d, g, w_xkv):
    rows = mem2d.shape[0]
    tm = min(512, rows)
    assert rows % tm == 0
    return pl.pallas_call(
        _mem_kv_kernel,
        grid=(rows // tm,),
        in_specs=[
            pl.BlockSpec((tm, D_MODEL), lambda i: (i, 0)),
            pl.BlockSpec((1, D_MODEL), lambda i: (0, 0)),
            pl.BlockSpec((D_MODEL, 2 * D_MODEL), lambda i: (0, 0)),
        ],
        out_specs=pl.BlockSpec((tm, 2 * D_MODEL), lambda i: (i, 0)),
        out_shape=jax.ShapeDtypeStruct((rows, 2 * D_MODEL), BF16),
        compiler_params=pltpu.CompilerParams(vmem_limit_bytes=VMEM_LIMIT),
        name="mem_kv",
    )(mem2d, g, w_xkv)


def _rope(t, c, sa, sb):
    w = t.shape[-1]
    return t * c + pltpu.roll(t, ROPE_DIM // 2, 1) * sa + pltpu.roll(t, w - ROPE_DIM // 2, 1) * sb


def _in_proj_kernel(x_ref, rot_ref, g_ref, win_ref, qg_ref, wq_ref, kvg_ref, wkv_ref,
                    poolw_ref, pscale_ref, wpb_ref,
                    qT_out, k_out, vT_out, ga_out, gyb_out, hist_ref):
    tm = x_ref.shape[0]
    i = pl.program_id(1)
    blocks = [slice(b * PROJ_SUB, (b + 1) * PROJ_SUB) for b in range(tm // PROJ_SUB)]
    rows = lambda parts: jnp.concatenate(parts, axis=0)
    hn = [_rms(x_ref[r, :], g_ref[...]).astype(BF16) for r in blocks]
    proj = lambda lo, hi: [_dot(v, win_ref[:, lo:hi]) for v in hn]

    half = ROPE_DIM // 2
    cos8 = jnp.concatenate([rot_ref[0:half, :]] * MLA_HEADS, axis=0).T
    sin8 = jnp.concatenate([rot_ref[half:, :]] * MLA_HEADS, axis=0).T
    lane = lax.broadcasted_iota(jnp.int32, cos8.shape, 1)
    rope_lo = NOPE_DIM
    c1 = jnp.where(lane < rope_lo + ROPE_DIM, cos8, 0.0)
    sa1 = jnp.where((lane >= rope_lo + half) & (lane < rope_lo + ROPE_DIM), sin8, 0.0)
    sb1 = jnp.where((lane >= rope_lo) & (lane < rope_lo + half), -sin8, 0.0)

    q_lat = proj(_C_Q, _C_KV)
    kv_lat = proj(_C_KV, _C_KR)
    qn = [_rms(v, qg_ref[...]).astype(BF16) for v in q_lat]
    kvn = [_rms(v, kvg_ref[...]).astype(BF16) for v in kv_lat]
    q = rows([_dot(v, wq_ref[...]) for v in qn])
    n_nope = MLA_HEADS * NOPE_DIM
    x1 = q[:, n_nope:n_nope + LANES]
    x2 = q[:, n_nope + LANES:]
    qT = jnp.concatenate([q[:, :n_nope], x1 * cos8 - x2 * sin8, x2 * cos8 + x1 * sin8], axis=1).T
    pad = jnp.zeros((HEAD_PAD - NOPE_DIM - ROPE_DIM, tm), F32)
    qT = jnp.concatenate(
        [blk for h in range(MLA_HEADS) for blk in (
            qT[h * NOPE_DIM:(h + 1) * NOPE_DIM, :],
            qT[n_nope + h * half:n_nope + (h + 1) * half, :],
            qT[n_nope + LANES + h * half:n_nope + LANES + (h + 1) * half, :], pad)], axis=0).astype(BF16)

    k_nope = rows([_dot(v, wkv_ref[:, 0:MLA_HEADS * HEAD_PAD]) for v in kvn])
    kr = _rope(rows(proj(_C_KR, _C_POOL)), c1, sa1, sb1)
    k_out[...] = (k_nope + jnp.tile(kr, (1, MLA_HEADS))).astype(BF16)
    v = rows([_dot(t, wkv_ref[:, MLA_HEADS * HEAD_PAD:]) for t in kvn])
    for t in range(tm // Q_TILE):
        qT_out[t] = qT[:, t * Q_TILE:(t + 1) * Q_TILE]
    vT = v.T
    ones = jnp.ones((V_ROWS - V_DIM, tm), F32)
    vT = jnp.concatenate(
        [blk for h in range(MLA_HEADS) for blk in (vT[h * V_DIM:(h + 1) * V_DIM, :], ones)], axis=0).astype(BF16)
    for t in range(tm // KV_TILE):
        vT_out[t] = vT[:, t * KV_TILE:(t + 1) * KV_TILE]

    u = rows(proj(_C_POOL, _C_GA))

    @pl.when(i == 0)
    def _():
        hist_ref[...] = jnp.zeros_like(hist_ref)

    ext = jnp.concatenate([hist_ref[...], u], axis=0)
    hist_ref[...] = u[tm - POOL_HALO:, :]
    t_idx = i * tm + lax.broadcasted_iota(jnp.int32, (tm, POOL_GROUP_DIM), 0)
    ys = []
    for g, w in enumerate(POOL_WINDOWS):
        c0 = g * POOL_GROUP_DIM
        run = ext[:, c0:c0 + POOL_GROUP_DIM]
        span = 1
        while span < w:
            run = run + pltpu.roll(run, span, 0)
            span *= 2
        cnt = jnp.minimum(t_idx + 1, w).astype(F32)
        d = run[POOL_HALO:, :] / cnt - u[:, c0:c0 + POOL_GROUP_DIM]
        ys.append(_dot(d.astype(BF16), poolw_ref[g]))
    y = (jnp.concatenate(ys, axis=1) * pscale_ref[...]).astype(BF16)
    y_b = [_dot(y[r, :], wpb_ref[...]) for r in blocks]

    for r, t in zip(blocks, proj(_C_GA, _C_GB)):
        ga_out[r, :] = jax.nn.sigmoid(t).astype(BF16)
    for r, t, yb in zip(blocks, proj(_C_GB, _C_END), y_b):
        gyb_out[r, :] = (jax.nn.sigmoid(t) * yb).astype(BF16)


def _in_proj(x, rot, g, win, qg, wq, kvg, wkv, poolw, pscale, wpb, tm):
    B, S, _ = x.shape
    row = lambda b, i: (b, i, 0)
    const2 = lambda b, i: (0, 0)
    const3 = lambda b, i: (0, 0, 0)
    slab = lambda b, i: (b, i, 0, 0)
    return pl.pallas_call(
        _in_proj_kernel,
        grid=(B, S // tm),
        in_specs=[
            pl.BlockSpec((None, tm, D_MODEL), row),
            pl.BlockSpec((None, ROPE_DIM, tm), lambda b, i: (b, 0, i)),
            pl.BlockSpec((1, D_MODEL), const2),
            pl.BlockSpec(win.shape, const2),
            pl.BlockSpec((1, Q_LORA), const2),
            pl.BlockSpec(wq.shape, const2),
            pl.BlockSpec((1, KV_LORA), const2),
            pl.BlockSpec(wkv.shape, const2),
            pl.BlockSpec(poolw.shape, const3),
            pl.BlockSpec((1, POOL_WIDTH), const2),
            pl.BlockSpec(wpb.shape, const2),
        ],
        out_specs=[
            pl.BlockSpec((None, tm // Q_TILE, MLA_HEADS * HEAD_PAD, Q_TILE), slab),
            pl.BlockSpec((None, tm, MLA_HEADS * HEAD_PAD), row),
            pl.BlockSpec((None, tm // KV_TILE, MLA_HEADS * V_ROWS, KV_TILE), slab),
            pl.BlockSpec((None, tm, D_MODEL), row),
            pl.BlockSpec((None, tm, D_MODEL), row),
        ],
        out_shape=[jax.ShapeDtypeStruct((B, S // Q_TILE, MLA_HEADS * HEAD_PAD, Q_TILE), BF16),
                   jax.ShapeDtypeStruct((B, S, MLA_HEADS * HEAD_PAD), BF16),
                   jax.ShapeDtypeStruct((B, S // KV_TILE, MLA_HEADS * V_ROWS, KV_TILE), BF16),
                   jax.ShapeDtypeStruct((B, S, D_MODEL), BF16),
                   jax.ShapeDtypeStruct((B, S, D_MODEL), BF16)],
        scratch_shapes=[pltpu.VMEM((POOL_HALO, POOL_WIDTH), F32)],
        compiler_params=pltpu.CompilerParams(
            dimension_semantics=("arbitrary", "arbitrary"), vmem_limit_bytes=VMEM_LIMIT),
        name="in_proj",
    )(x, rot, g, win, qg, wq, kvg, wkv, poolw, pscale, wpb)


def _mla_kernel(qT_ref, k_ref, vT_ref, o_ref, m_ref, acc_ref):
    i = pl.program_id(1)
    n_sub = KV_TILE // KEY_SUB
    units = [(h, c) for c in range(n_sub) for h in range(MLA_HEADS)]
    qry_c = lax.broadcasted_iota(jnp.int32, (KEY_SUB, Q_TILE), 1) // CHUNK

    def scores(j, h, c):
        hs = slice(h * HEAD_PAD, (h + 1) * HEAD_PAD)
        rows = pl.ds(pl.multiple_of(j * KV_TILE + c * KEY_SUB, KEY_SUB), KEY_SUB)
        return _dot(k_ref[rows, hs], qT_ref[hs, :])

    def fold(h, alpha, pv):
        acc_ref[h] = pv if alpha is None else alpha * acc_ref[h] + pv

    def sweep(j, diag):
        ahead = [scores(j, *u) for u in units[:QK_AHEAD]]
        pending = []
        for n, (h, c) in enumerate(units):
            s = ahead.pop(0)
            if n + QK_AHEAD < len(units):
                ahead.append(scores(j, *units[n + QK_AHEAD]))
            first = False
            if diag is not None:
                key_c0 = (diag * KV_TILE + c * KEY_SUB) // CHUNK
                key_c = key_c0 + lax.broadcasted_iota(jnp.int32, (KEY_SUB, Q_TILE), 0) // CHUNK
                s = jnp.where(key_c <= qry_c, s, -jnp.inf)
                first = diag == 0 and c == 0
            s_max = jnp.max(s, axis=0, keepdims=True)
            m_new = s_max if first else jnp.maximum(m_ref[h], s_max)
            p = jnp.exp2(s - m_new).astype(BF16)
            pv = _dot(vT_ref[j, h * V_ROWS:(h + 1) * V_ROWS, c * KEY_SUB:(c + 1) * KEY_SUB], p)
            alpha = None if first else jnp.exp2(m_ref[h] - m_new)
            m_ref[h] = m_new
            pending.append((h, alpha, pv))
            if len(pending) > PV_BEHIND:
                fold(*pending.pop(0))
        for item in pending:
            fold(*item)

    n_diag = Q_TILE // KV_TILE
    for d in range(n_diag):
        sweep(i * n_diag + d, d)

    def body(j, carry):
        sweep(j, None)
        return carry

    lax.fori_loop(0, i * n_diag, body, 0)
    oT = jnp.concatenate([acc_ref[h, :V_DIM, :] / acc_ref[h, V_DIM:V_DIM + 1, :] for h in range(MLA_HEADS)], axis=0)
    o_ref[...] = oT.T.astype(BF16)


def _mla_attention(qT, k, vT):
    B, S, W = k.shape
    return pl.pallas_call(
        _mla_kernel,
        grid=(B, S // Q_TILE),
        in_specs=[
            pl.BlockSpec((None, None, W, Q_TILE), lambda b, i: (b, i, 0, 0)),
            pl.BlockSpec((None, S, W), lambda b, i: (b, 0, 0)),
            pl.BlockSpec((None, S // KV_TILE, MLA_HEADS * V_ROWS, KV_TILE), lambda b, i: (b, 0, 0, 0)),
        ],
        out_specs=pl.BlockSpec((None, Q_TILE, MLA_HEADS * V_DIM), lambda b, i: (b, i, 0)),
        out_shape=jax.ShapeDtypeStruct((B, S, MLA_HEADS * V_DIM), BF16),
        scratch_shapes=[pltpu.VMEM((MLA_HEADS, 1, Q_TILE), F32),
                        pltpu.VMEM((MLA_HEADS, V_ROWS, Q_TILE), F32)],
        compiler_params=pltpu.CompilerParams(
            dimension_semantics=("arbitrary", "arbitrary"), vmem_limit_bytes=VMEM_LIMIT),
        name="mla_attn",
    )(qT, k, vT)


def _route(logits_t, bias_t, tri_upper, tri_lower):
    tm = logits_t.shape[1]
    neg = -jnp.inf
    rg = lax.broadcasted_iota(jnp.int32, (8, tm), 0)
    re = lax.broadcasted_iota(jnp.int32, (N_EXPERTS, tm), 0)
    top = lambda v: jnp.max(v, axis=0, keepdims=True)

    lg = jnp.where(rg < N_GROUPS, logits_t[LANES:LANES + 8, :] + bias_t[LANES:LANES + 8, :], neg)
    ge = jnp.exp(lg - top(lg))
    gp = ge / jnp.sum(ge, axis=0, keepdims=True)
    g_w = top(gp)
    g_idx = jnp.min(jnp.where(gp == g_w, rg, 8), axis=0, keepdims=True)

    sel = re // EXPERTS_PER_GROUP == g_idx
    le = jnp.where(sel, logits_t[:N_EXPERTS, :] + bias_t[:N_EXPERTS, :], neg)
    ee = jnp.exp(le - top(le))
    ep = jnp.where(sel, ee / jnp.sum(ee, axis=0, keepdims=True), -1.0)
    w1 = top(ep)
    i1 = jnp.min(jnp.where(ep == w1, re, N_EXPERTS), axis=0, keepdims=True)
    ep2 = jnp.where(re == i1, -1.0, ep)
    w2 = top(ep2)
    i2 = jnp.min(jnp.where(ep2 == w2, re, N_EXPERTS), axis=0, keepdims=True)
    den = w1 + w2
    c1 = g_w * (w1 / den)
    c2 = g_w * (w2 / den)

    oh1 = (re == i1).astype(F32)
    oh2 = (re == i2).astype(F32)
    both = oh1 + oh2
    earlier = _dot(both.astype(BF16), tri_upper)
    pieces = jnp.floor((jnp.sum(both, axis=1, keepdims=True) + (PIECE - 1)) * (1.0 / PIECE))
    start = _dot(tri_lower, jnp.broadcast_to(pieces, (N_EXPERTS, LANES)).astype(BF16))[:, 0:1] * PIECE
    pos1 = jnp.sum(oh1 * (earlier + start), axis=0, keepdims=True)
    pos2 = jnp.sum(oh2 * (earlier + start), axis=0, keepdims=True)
    info_t = jnp.concatenate([pos1, pos2, c1, c2, jnp.zeros((4, tm), F32)], axis=0)
    return info_t, pieces


def _mix_kernel(x_ref, attn_ref, ga_ref, gyb_ref, kv_ref, wab_ref, wmix_ref, xg_ref, wxq_ref, wxo_ref,
                fg_ref, wr_hi_ref, wr_lo_ref, rb_ref, tri_ref, lower_ref,
                h_out, xn_out, info_out, infoT_out, pieces_out, logits_ref):
    tm = x_ref.shape[0]

    @pl.when(pl.program_id(0) == 0)
    def _():
        logits_ref[...] = jnp.zeros_like(logits_ref)

    info_t, pieces = _route(logits_ref[...].T, rb_ref[...], tri_ref[...], lower_ref[...])
    infoT_out[...] = info_t
    info_out[...] = jnp.concatenate([info_t, jnp.zeros((LANES - 8, tm), F32)], axis=0).T
    pieces_out[...] = jnp.broadcast_to(pieces, (N_EXPERTS, LANES))

    blocks = [slice(b * MIX_SUB, (b + 1) * MIX_SUB) for b in range(tm // MIX_SUB)]
    y_a = [_dot(attn_ref[r, :], wab_ref[...]) for r in blocks]
    merged = [(ga_ref[r, :].astype(F32) * y + gyb_ref[r, :].astype(F32)).astype(BF16) for r, y in zip(blocks, y_a)]
    h1 = [x_ref[r, :] + _dot(m, wmix_ref[...]) for r, m in zip(blocks, merged)]

    hn = [_rms(h, xg_ref[...]).astype(BF16) for h in h1]
    q = [_dot(v, wxq_ref[...]).astype(BF16) for v in hn]
    pairs = [(b, h) for h in range(MEM_HEADS) for b in range(len(blocks))]
    hs = lambda h: slice(h * MEM_HEAD_DIM, (h + 1) * MEM_HEAD_DIM)
    vs = lambda h: slice(D_MODEL + h * MEM_HEAD_DIM, D_MODEL + (h + 1) * MEM_HEAD_DIM)
    s = [_dot_nt(q[b][:, hs(h)], kv_ref[:, hs(h)]) for b, h in pairs]
    p = [jnp.exp(v - jnp.max(v, axis=-1, keepdims=True)) for v in s]
    o = [_dot(v.astype(BF16), kv_ref[:, vs(h)]) for v, (b, h) in zip(p, pairs)]
    heads = [[None] * MEM_HEADS for _ in blocks]
    for (b, h), ov, pv in zip(pairs, o, p):
        heads[b][h] = (ov / jnp.sum(pv, axis=-1, keepdims=True)).astype(BF16)
    h2 = [h + _dot(jnp.concatenate(hd, axis=1), wxo_ref[...]) for h, hd in zip(h1, heads)]

    xn = [_rms(h, fg_ref[...]) for h in h2]
    xn_hi = [v.astype(BF16) for v in xn]
    xn_lo = [(v - hi.astype(F32)).astype(BF16) for v, hi in zip(xn, xn_hi)]
    logits = [_dot(hi, wr_hi_ref[...]) + (_dot(hi, wr_lo_ref[...]) + _dot(lo, wr_hi_ref[...]))
              for hi, lo in zip(xn_hi, xn_lo)]
    for r, h, hi in zip(blocks, h2, xn_hi):
        h_out[r, :] = h
        xn_out[r, :] = hi
    for r, v in zip(blocks, logits):
        logits_ref[r, :] = v


def _mix_xattn(x, attn, ga, gyb, memkv, wab, wmix, xg, wxq, wxo, fg, wr_hi, wr_lo, rb):
    B, S, _ = x.shape
    M = memkv.shape[1]
    tm = MOE_CHUNK
    nt = S // tm
    n_tiles = B * nt
    cur = lambda t: jnp.minimum(t, n_tiles - 1)
    prev = lambda t: jnp.maximum(t - 1, 0)
    row = lambda t: (cur(t) // nt, cur(t) % nt, 0)
    const2 = lambda t: (0, 0)
    tri = (lax.broadcasted_iota(jnp.int32, (tm, tm), 0) < lax.broadcasted_iota(jnp.int32, (tm, tm), 1)).astype(BF16)
    lower = (lax.broadcasted_iota(jnp.int32, (N_EXPERTS, N_EXPERTS), 1)
             < lax.broadcasted_iota(jnp.int32, (N_EXPERTS, N_EXPERTS), 0)).astype(BF16)
    rb = jnp.broadcast_to(rb.reshape(2 * LANES, 1), (2 * LANES, tm))
    return pl.pallas_call(
        _mix_kernel,
        grid=(n_tiles + 1,),
        in_specs=[
            pl.BlockSpec((None, tm, D_MODEL), row),
            pl.BlockSpec((None, tm, MLA_HEADS * V_DIM), row),
            pl.BlockSpec((None, tm, D_MODEL), row),
            pl.BlockSpec((None, tm, D_MODEL), row),
            pl.BlockSpec((None, M, 2 * D_MODEL), lambda t: (cur(t) // nt, 0, 0)),
            pl.BlockSpec(wab.shape, const2),
            pl.BlockSpec(wmix.shape, const2),
            pl.BlockSpec((1, D_MODEL), const2),
            pl.BlockSpec(wxq.shape, const2),
            pl.BlockSpec(wxo.shape, const2),
            pl.BlockSpec((1, D_MODEL), const2),
            pl.BlockSpec(wr_hi.shape, const2),
            pl.BlockSpec(wr_lo.shape, const2),
            pl.BlockSpec((2 * LANES, tm), const2),
            pl.BlockSpec((tm, tm), const2),
            pl.BlockSpec((N_EXPERTS, N_EXPERTS), const2),
        ],
        out_specs=[
            pl.BlockSpec((None, tm, D_MODEL), row),
            pl.BlockSpec((None, tm, D_MODEL), row),
            pl.BlockSpec((None, tm, LANES), lambda t: (prev(t) // nt, prev(t) % nt, 0)),
            pl.BlockSpec((8, tm), lambda t: (0, prev(t))),
            pl.BlockSpec((None, N_EXPERTS, LANES), lambda t: (prev(t), 0, 0)),
        ],
        out_shape=[jax.ShapeDtypeStruct((B, S, D_MODEL), F32),
                   jax.ShapeDtypeStruct((B, S, D_MODEL), BF16),
                   jax.ShapeDtypeStruct((B, S, LANES), F32),
                   jax.ShapeDtypeStruct((8, B * S), F32),
                   jax.ShapeDtypeStruct((B * nt, N_EXPERTS, LANES), F32)],
        scratch_shapes=[pltpu.VMEM((tm, 2 * LANES), F32)],
        compiler_params=pltpu.CompilerParams(
            dimension_semantics=("arbitrary",), vmem_limit_bytes=VMEM_LIMIT),
        name="mix_xattn",
    )(x, attn, ga, gyb, memkv, wab, wmix, xg, wxq, wxo, fg, wr_hi, wr_lo, rb, tri, lower)


def _piece_copy(src_ref, dst_ref, sem):
    return pltpu.make_async_copy(src_ref, dst_ref, sem)


def _dispatch_kernel(dst_ref, np_ref, gap_ref, fill_ref, xn_ref, infoT_ref, xs_hbm, buf_ref, zero_ref, sem_ref):
    c = pl.program_id(0)
    n = pl.num_programs(0)
    slot = c % 2

    n_tiles = xs_hbm.shape[0] // EXPERT_TILE

    def gap_copy(g):
        return _piece_copy(zero_ref.at[pl.ds(0, PIECE)],
                           xs_hbm.at[pl.ds(pl.multiple_of(gap_ref[g] * PIECE, PIECE), PIECE)], sem_ref.at[2])

    def tail_copy(t):
        return _piece_copy(zero_ref, xs_hbm.at[pl.ds(pl.multiple_of(t * EXPERT_TILE, EXPERT_TILE), EXPERT_TILE)],
                           sem_ref.at[2])

    @pl.when(c == 0)
    def _():
        zero_ref[...] = jnp.zeros_like(zero_ref)
        lax.fori_loop(0, fill_ref[0], lambda g, carry: (gap_copy(g).start(), carry)[1], 0)
        lax.fori_loop(fill_ref[1], n_tiles, lambda t, carry: (tail_copy(t).start(), carry)[1], 0)

    def copy(cc, s, q):
        return _piece_copy(buf_ref.at[s, pl.ds(pl.multiple_of(q * PIECE, PIECE), PIECE)],
                           xs_hbm.at[pl.ds(pl.multiple_of(dst_ref[cc * MAX_PIECES + q] * PIECE, PIECE), PIECE)],
                           sem_ref.at[s])

    def start_all(cc, s):
        lax.fori_loop(0, np_ref[cc], lambda q, carry: (copy(cc, s, q).start(), carry)[1], 0)

    def wait_all(cc, s):
        lax.fori_loop(0, np_ref[cc], lambda q, carry: (copy(cc, s, q).wait(), carry)[1], 0)

    @pl.when(c >= 2)
    def _():
        wait_all(c - 2, slot)

    pos1 = infoT_ref[0:1, :]
    pos2 = infoT_ref[1:2, :]
    r = lax.broadcasted_iota(jnp.int32, (CHUNK_ROWS, MOE_CHUNK), 0).astype(F32)
    onehot = jnp.where((r == pos1) | (r == pos2), 1.0, 0.0).astype(BF16)
    buf_ref[slot] = _dot(onehot, xn_ref[...]).astype(BF16)
    start_all(c, slot)

    @pl.when(c == n - 1)
    def _():
        @pl.when(c >= 1)
        def _():
            wait_all(c - 1, 1 - slot)
        wait_all(c, slot)
        lax.fori_loop(0, fill_ref[0], lambda g, carry: (gap_copy(g).wait(), carry)[1], 0)
        lax.fori_loop(fill_ref[1], n_tiles, lambda t, carry: (tail_copy(t).wait(), carry)[1], 0)


def _dispatch(xn, infoT, dst, npc, gaps, fill, rows_max):
    T = xn.shape[0]
    grid_spec = pltpu.PrefetchScalarGridSpec(
        num_scalar_prefetch=4,
        grid=(T // MOE_CHUNK,),
        in_specs=[
            pl.BlockSpec((MOE_CHUNK, D_MODEL), lambda c, *_: (c, 0)),
            pl.BlockSpec((8, MOE_CHUNK), lambda c, *_: (0, c)),
        ],
        out_specs=pl.BlockSpec(memory_space=pl.ANY),
        scratch_shapes=[pltpu.VMEM((2, CHUNK_ROWS, D_MODEL), BF16), pltpu.VMEM((EXPERT_TILE, D_MODEL), BF16),
                        pltpu.SemaphoreType.DMA((3,))],
    )
    return pl.pallas_call(
        _dispatch_kernel,
        grid_spec=grid_spec,
        out_shape=jax.ShapeDtypeStruct((rows_max, D_MODEL), BF16),
        compiler_params=pltpu.CompilerParams(
            dimension_semantics=("arbitrary",), vmem_limit_bytes=VMEM_LIMIT),
        name="moe_dispatch",
    )(dst, npc, gaps, fill, xn, infoT)


def _expert_kernel(te_ref, tv_ref, nu_ref, x_ref, wg_ref, wu_ref, wd_ref, y_ref, wgu_bf, wd_bf):
    i = pl.program_id(0)

    @pl.when(i < nu_ref[0])
    def _():
        @pl.when((i == 0) | (te_ref[i] != te_ref[jnp.maximum(i - 1, 0)]))
        def _():
            wgu_bf[:, :EXPERT_FF] = wg_ref[...].astype(BF16)
            wgu_bf[:, EXPERT_FF:] = wu_ref[...].astype(BF16)
            wd_bf[...] = wd_ref[...].astype(BF16)

        blocks = [slice(b * EXPERT_SUB, (b + 1) * EXPERT_SUB) for b in range(EXPERT_TILE // EXPERT_SUB)]
        row = lax.broadcasted_iota(jnp.int32, (EXPERT_SUB, D_MODEL), 0)
        gus = []
        for b, rows in enumerate(blocks):
            x = x_ref[rows, :]
            x = jnp.where(row < tv_ref[i] - b * EXPERT_SUB, x, jnp.zeros_like(x))
            gus.append(_dot(x, wgu_bf[...]))
        hids = []
        for gu in gus:
            gate = gu[:, :EXPERT_FF]
            hids.append((gate * jax.nn.sigmoid(gate) * gu[:, EXPERT_FF:]).astype(BF16))
        for rows, hid in zip(blocks, hids):
            y_ref[rows, :] = _dot(hid, wd_bf[...]).astype(BF16)

    @pl.when(i >= nu_ref[0])
    def _():
        y_ref[...] = jnp.zeros_like(y_ref)


def _experts(xs, w_gate, w_up, w_down, tile_expert, tile_valid, n_used):
    rows_max = xs.shape[0]
    last = lambda i, nu: jnp.minimum(i, nu[0] - 1)
    expert = lambda i, te, tv, nu: (te[last(i, nu)], 0, 0)
    grid_spec = pltpu.PrefetchScalarGridSpec(
        num_scalar_prefetch=3,
        grid=(rows_max // EXPERT_TILE,),
        in_specs=[
            pl.BlockSpec((EXPERT_TILE, D_MODEL), lambda i, te, tv, nu: (last(i, nu), 0)),
            pl.BlockSpec((None, D_MODEL, EXPERT_FF), expert),
            pl.BlockSpec((None, D_MODEL, EXPERT_FF), expert),
            pl.BlockSpec((None, EXPERT_FF, D_MODEL), expert),
        ],
        out_specs=pl.BlockSpec((EXPERT_TILE, D_MODEL), lambda i, te, tv, nu: (i, 0)),
        scratch_shapes=[pltpu.VMEM((D_MODEL, 2 * EXPERT_FF), BF16), pltpu.VMEM((EXPERT_FF, D_MODEL), BF16)],
    )
    return pl.pallas_call(
        _expert_kernel,
        grid_spec=grid_spec,
        out_shape=jax.ShapeDtypeStruct((rows_max, D_MODEL), BF16),
        compiler_params=pltpu.CompilerParams(
            dimension_semantics=("arbitrary",), vmem_limit_bytes=VMEM_LIMIT),
        name="moe_experts",
    )(tile_expert, tile_valid, n_used, xs, w_gate, w_up, w_down)


def _combine_kernel(dst_ref, np_ref, h_ref, info_ref, fg_ref, ys_hbm, o_ref, buf_ref, sem_ref):
    c = pl.program_id(0)
    n = pl.num_programs(0)
    slot = c % 2

    def copy(cc, s, q):
        return _piece_copy(ys_hbm.at[pl.ds(pl.multiple_of(dst_ref[cc * MAX_PIECES + q] * PIECE, PIECE), PIECE)],
                           buf_ref.at[s, pl.ds(pl.multiple_of(q * PIECE, PIECE), PIECE)],
                           sem_ref.at[s])

    def start_all(cc, s):
        lax.fori_loop(0, np_ref[cc], lambda q, carry: (copy(cc, s, q).start(), carry)[1], 0)

    def wait_all(cc, s):
        lax.fori_loop(0, np_ref[cc], lambda q, carry: (copy(cc, s, q).wait(), carry)[1], 0)

    @pl.when(c == 0)
    def _():
        buf_ref[...] = jnp.zeros_like(buf_ref)
        start_all(0, 0)

    @pl.when(c + 1 < n)
    def _():
        start_all(c + 1, 1 - slot)

    wait_all(c, slot)
    info = info_ref[...]
    r = lax.broadcasted_iota(jnp.int32, (MOE_CHUNK, CHUNK_ROWS), 1).astype(F32)
    weights = jnp.where(r == info[:, 0:1], info[:, 2:3], 0.0) + jnp.where(r == info[:, 1:2], info[:, 3:4], 0.0)
    moe = _dot(weights.astype(BF16), buf_ref[slot])
    o_ref[...] = _rms(h_ref[...] + moe, fg_ref[...])


def _combine(h2, info, fg, ys, dst, npc):
    T = h2.shape[0]
    grid_spec = pltpu.PrefetchScalarGridSpec(
        num_scalar_prefetch=2,
        grid=(T // MOE_CHUNK,),
        in_specs=[
            pl.BlockSpec((MOE_CHUNK, D_MODEL), lambda c, dst, npc: (c, 0)),
            pl.BlockSpec((MOE_CHUNK, LANES), lambda c, dst, npc: (c, 0)),
            pl.BlockSpec((1, D_MODEL), lambda c, dst, npc: (0, 0)),
            pl.BlockSpec(memory_space=pl.ANY),
        ],
        out_specs=pl.BlockSpec((MOE_CHUNK, D_MODEL), lambda c, dst, npc: (c, 0)),
        scratch_shapes=[pltpu.VMEM((2, CHUNK_ROWS, D_MODEL), BF16), pltpu.SemaphoreType.DMA((2,))],
    )
    return pl.pallas_call(
        _combine_kernel,
        grid_spec=grid_spec,
        out_shape=jax.ShapeDtypeStruct((T, D_MODEL), F32),
        compiler_params=pltpu.CompilerParams(
            dimension_semantics=("arbitrary",), vmem_limit_bytes=VMEM_LIMIT),
        name="moe_combine",
    )(dst, npc, h2, info, fg, ys)


def _routing_tables(pieces, rows_max):
    tile_pieces = EXPERT_TILE // PIECE
    total = jnp.sum(pieces, axis=0)
    total_al = (total + tile_pieces - 1) // tile_pieces * tile_pieces
    seg_end = jnp.cumsum(total_al)
    seg_start = seg_end - total_al
    chunk_off = jnp.cumsum(pieces, axis=0) - pieces
    loc_end = jnp.cumsum(pieces, axis=1)
    loc_start = loc_end - pieces
    q = jnp.arange(MAX_PIECES, dtype=jnp.int32)
    owner = jnp.sum((q[None, :, None] >= loc_end[:, None, :]).astype(jnp.int32), axis=-1)
    owner = jnp.minimum(owner, N_EXPERTS - 1)
    experts = jnp.arange(N_EXPERTS, dtype=jnp.int32)
    is_owner = (owner[:, :, None] == experts).astype(jnp.int32)
    offset = seg_start[None, :] + chunk_off - loc_start
    dst = jnp.sum(is_owner * offset[:, None, :], axis=-1) + q[None, :]
    n_local = loc_end[:, -1]
    dst = jnp.where(q[None, :] < n_local[:, None], dst, 0)

    t0 = jnp.arange(rows_max // EXPERT_TILE, dtype=jnp.int32) * tile_pieces
    tile_expert = jnp.minimum(jnp.sum((t0[:, None] >= seg_end[None, :]).astype(jnp.int32), axis=-1), N_EXPERTS - 1)
    copies_end = jnp.sum((tile_expert[:, None] == experts).astype(jnp.int32) * (seg_start + total)[None, :], axis=-1)
    tile_valid = jnp.clip((copies_end - t0) * PIECE, 0, EXPERT_TILE)
    n_used = (seg_end[-1] // tile_pieces).reshape(1)

    k = jnp.arange(tile_pieces, dtype=jnp.int32)
    is_gap = (k[None, :] < (total_al - total)[:, None]).reshape(-1)
    gap_piece = (seg_start + total)[:, None] + k[None, :]
    order = jnp.argsort(jnp.logical_not(is_gap), stable=True)
    gaps = gap_piece.reshape(-1)[order]
    fill = jnp.stack([jnp.sum(is_gap.astype(jnp.int32)), n_used[0]])
    i32 = lambda a: a.astype(jnp.int32)
    return i32(dst.reshape(-1)), i32(n_local), i32(tile_expert), i32(tile_valid), i32(n_used), i32(gaps), i32(fill)


def _rope_tables(positions):
    inv_freq = 1.0 / (ROPE_THETA ** (jnp.arange(0, ROPE_DIM, 2, dtype=F32) / ROPE_DIM))
    ang = positions.astype(F32)[:, None, :] * inv_freq[None, :, None]
    cos, sin = jnp.cos(ang), jnp.sin(ang)
    return jnp.concatenate([cos, sin], axis=1)


def _pad_heads(w, heads, width):
    k = w.shape[0]
    w = w.reshape(k, heads, width)
    w = jnp.pad(w, ((0, 0), (0, 0), (0, HEAD_PAD - width)))
    return w.reshape(k, heads * HEAD_PAD)


def _layer(l, h, mem, tables, mix_norm_g, w_in, q_norm_g, w_q_up, kv_norm_g, w_kv_up, w_attn_branch,
           pool_w, pool_scale, w_pool_branch, w_mix_out, xattn_norm_g, mem_norm_g, w_xq, w_xkv, w_xo,
           ffn_norm_g, w_router_group, b_router_group, w_router_expert, b_router_expert,
           w_exp_gate, w_exp_up, w_exp_down, out_g, tm_proj):
    B, S, _ = h.shape
    row2 = lambda v: v.reshape(1, -1).astype(F32)

    wi = w_in[l]
    kr_cols = jnp.pad(wi[:, Q_LORA + KV_LORA:Q_LORA + KV_LORA + ROPE_DIM],
                      ((0, 0), (NOPE_DIM, LANES - NOPE_DIM - ROPE_DIM)))
    win = jnp.concatenate([wi[:, :Q_LORA + KV_LORA], kr_cols, wi[:, Q_LORA + KV_LORA + ROPE_DIM:]], axis=1).astype(BF16)
    scale = math.log2(math.e) / math.sqrt(NOPE_DIM + ROPE_DIM)
    wq3 = (w_q_up[l] * scale).reshape(Q_LORA, MLA_HEADS, NOPE_DIM + ROPE_DIM)
    half = ROPE_DIM // 2
    wq = jnp.concatenate([wq3[:, :, :NOPE_DIM].reshape(Q_LORA, -1),
                          wq3[:, :, NOPE_DIM:NOPE_DIM + half].reshape(Q_LORA, -1),
                          wq3[:, :, NOPE_DIM + half:].reshape(Q_LORA, -1)], axis=1).astype(BF16)
    wkv3 = w_kv_up[l].reshape(KV_LORA, MLA_HEADS, NOPE_DIM + V_DIM)
    wkv = jnp.concatenate([
        _pad_heads(wkv3[:, :, :NOPE_DIM].reshape(KV_LORA, -1), MLA_HEADS, NOPE_DIM),
        wkv3[:, :, NOPE_DIM:].reshape(KV_LORA, -1)], axis=1).astype(BF16)

    memkv = _mem_kv(mem.reshape(-1, D_MODEL), row2(mem_norm_g[l]), w_xkv[l].astype(BF16))
    memkv = memkv.reshape(B, -1, 2 * D_MODEL)

    qT, k, vT, ga, gyb = _in_proj(
        h, tables, row2(mix_norm_g[l]), win, row2(q_norm_g[l]), wq, row2(kv_norm_g[l]), wkv,
        pool_w[l].astype(BF16), row2(pool_scale[l]), w_pool_branch[l].astype(BF16), tm_proj)
    attn = _mla_attention(qT, k, vT)

    w_r = jnp.zeros((D_MODEL, 2 * LANES), F32)
    w_r = w_r.at[:, :N_EXPERTS].set(w_router_expert[l]).at[:, LANES:LANES + N_GROUPS].set(w_router_group[l])
    wr_hi = w_r.astype(BF16)
    wr_lo = (w_r - wr_hi.astype(F32)).astype(BF16)
    rb = jnp.zeros((1, 2 * LANES), F32)
    rb = rb.at[0, :N_EXPERTS].set(b_router_expert[l]).at[0, LANES:LANES + N_GROUPS].set(b_router_group[l])

    h2, xn, info, infoT, pieces = _mix_xattn(
        h, attn, ga, gyb, memkv, w_attn_branch[l].astype(BF16), w_mix_out[l].astype(BF16),
        row2(xattn_norm_g[l]), (w_xq[l] * (1.0 / math.sqrt(MEM_HEAD_DIM))).astype(BF16), w_xo[l].astype(BF16),
        row2(ffn_norm_g[l]), wr_hi, wr_lo, rb)

    T = B * S
    n_chunks = T // MOE_CHUNK
    tile_pieces = EXPERT_TILE // PIECE
    max_pieces = 2 * T // PIECE + n_chunks * N_EXPERTS + N_EXPERTS * tile_pieces
    rows_max = -(-max_pieces // tile_pieces) * EXPERT_TILE
    dst, n_local, tile_expert, tile_valid, n_used, gaps, fill = _routing_tables(
        pieces[:, :, 0].astype(jnp.int32), rows_max)

    xs = _dispatch(xn.reshape(T, D_MODEL), infoT, dst, n_local, gaps, fill, rows_max)
    ys = _experts(xs, w_exp_gate[l], w_exp_up[l], w_exp_down[l], tile_expert, tile_valid, n_used)
    out = _combine(h2.reshape(T, D_MODEL), info.reshape(T, LANES), row2(out_g), ys, dst, n_local)
    return out.reshape(B, S, D_MODEL)


def kernel(x, mem, positions, mix_norm_g, w_in, q_norm_g, w_q_up, kv_norm_g, w_kv_up, w_attn_branch, pool_w, pool_scale, w_pool_branch, w_mix_out, xattn_norm_g, mem_norm_g, w_xq, w_xkv, w_xo, ffn_norm_g, w_router_group, b_router_group, w_router_expert, b_router_expert, w_exp_gate, w_exp_up, w_exp_down, final_norm_g):
    depth = w_in.shape[0]
    assert depth == 1, "the combine kernel fuses the final RMSNorm, which is only valid after the last layer"
    assert x.shape[1] % Q_TILE == 0 and x.shape[1] % MOE_CHUNK == 0
    tables = _rope_tables(positions)
    return _layer(0, x, mem, tables, mix_norm_g, w_in, q_norm_g, w_q_up, kv_norm_g, w_kv_up, w_attn_branch,
                  pool_w, pool_scale, w_pool_branch, w_mix_out, xattn_norm_g, mem_norm_g, w_xq, w_xkv, w_xo,
                  ffn_norm_g, w_router_group, b_router_group, w_router_expert, b_router_expert,
                  w_exp_gate, w_exp_up, w_exp_down, final_norm_g, IN_PROJ_TILE)
```

```python
import functools
import math

import jax
import jax.numpy as jnp
from jax import lax
from jax.experimental import pallas as pl
from jax.experimental.pallas import tpu as pltpu

F32 = jnp.float32
BF16 = jnp.bfloat16

D_MODEL = 1024
CHUNK = 64
MLA_HEADS = 8
Q_LORA = 384
KV_LORA = 256
NOPE_DIM = 64
ROPE_DIM = 32
V_DIM = 64
ROPE_THETA = 10000.0
POOL_WIDTH = 512
POOL_WINDOWS = (2, 4, 8, 16)
POOL_GROUP_DIM = POOL_WIDTH // len(POOL_WINDOWS)
POOL_HALO = 16
MEM_HEADS = 4
MEM_HEAD_DIM = D_MODEL // MEM_HEADS
N_GROUPS = 4
EXPERTS_PER_GROUP = 8
N_EXPERTS = N_GROUPS * EXPERTS_PER_GROUP
EXPERT_FF = 256
EPS = 1e-6

LANES = 128
HEAD_PAD = LANES
Q_TILE = 512
KV_TILE = 512
KEY_SUB = 256
SCORE_SUB = 256
V_ROWS = V_DIM + 16
IN_PROJ_TILE = 512
MOE_CHUNK = 512
PIECE = 16
MAX_PIECES = 2 * MOE_CHUNK // PIECE + N_EXPERTS
CHUNK_ROWS = MAX_PIECES * PIECE
EXPERT_TILE = 1024
EXPERT_SUB = 256
PROJ_SUB = 256
MIX_SUB = 256
QK_AHEAD = 2
PV_BEHIND = 1

_C_Q = 0
_C_KV = _C_Q + Q_LORA
_C_KR = _C_KV + KV_LORA
_C_POOL = _C_KR + LANES
_C_GA = _C_POOL + POOL_WIDTH
_C_GB = _C_GA + D_MODEL
_C_END = _C_GB + D_MODEL

VMEM_LIMIT = 56 * 1024 * 1024


def _rms(x, g):
    return x * lax.rsqrt(jnp.mean(x * x, axis=-1, keepdims=True) + EPS) * g


def _dot(a, b):
    return jnp.dot(a, b, preferred_element_type=F32)


def _dot_nt(a, b):
    return lax.dot_general(a, b, (((1,), (1,)), ((), ())), preferred_element_type=F32)


def _mem_kv_kernel(mem_ref, g_ref, w_ref, kv_ref):
    mn = _rms(mem_ref[...], g_ref[...]).astype(BF16)
    kv_ref[...] = _dot(mn, w_ref[...]).astype(BF16)


def _mem_kv(mem2d, g, w_xkv):
    rows = mem2d.shape[0]
    tm = min(512, rows)
    assert rows % tm == 0
    return pl.pallas_call(
        _mem_kv_kernel,
        grid=(rows // tm,),
        in_specs=[
            pl.BlockSpec((tm, D_MODEL), lambda i: (i, 0)),
            pl.BlockSpec((1, D_MODEL), lambda i: (0, 0)),
            pl.BlockSpec((D_MODEL, 2 * D_MODEL), lambda i: (0, 0)),
        ],
        out_specs=pl.BlockSpec((tm, 2 * D_MODEL), lambda i: (i, 0)),
        out_shape=jax.ShapeDtypeStruct((rows, 2 * D_MODEL), BF16),
        compiler_params=pltpu.CompilerParams(vmem_limit_bytes=VMEM_LIMIT),
        name="mem_kv",
    )(mem2d, g, w_xkv)


def _rope(t, c, sa, sb):
    w = t.shape[-1]
    return t * c + pltpu.roll(t, ROPE_DIM // 2, 1) * sa + pltpu.roll(t, w - ROPE_DIM // 2, 1) * sb


def _in_proj_kernel(x_ref, rot_ref, g_ref, win_ref, qg_ref, wq_ref, kvg_ref, wkv_ref,
                    poolw_ref, pscale_ref, wpb_ref,
                    qT_out, k_out, vT_out, ga_out, gyb_out, hist_ref):
    tm = x_ref.shape[0]
    i = pl.program_id(1)
    blocks = [slice(b * PROJ_SUB, (b + 1) * PROJ_SUB) for b in range(tm // PROJ_SUB)]
    rows = lambda parts: jnp.concatenate(parts, axis=0)
    hn = [_rms(x_ref[r, :], g_ref[...]).astype(BF16) for r in blocks]
    proj = lambda lo, hi: [_dot(v, win_ref[:, lo:hi]) for v in hn]

    half = ROPE_DIM // 2
    cos8 = jnp.concatenate([rot_ref[0:half, :]] * MLA_HEADS, axis=0).T
    sin8 = jnp.concatenate([rot_ref[half:, :]] * MLA_HEADS, axis=0).T
    lane = lax.broadcasted_iota(jnp.int32, cos8.shape, 1)
    rope_lo = NOPE_DIM
    c1 = jnp.where(lane < rope_lo + ROPE_DIM, cos8, 0.0)
    sa1 = jnp.where((lane >= rope_lo + half) & (lane < rope_lo + ROPE_DIM), sin8, 0.0)
    sb1 = jnp.where((lane >= rope_lo) & (lane < rope_lo + half), -sin8, 0.0)

    q_lat = proj(_C_Q, _C_KV)
    kv_lat = proj(_C_KV, _C_KR)
    qn = [_rms(v, qg_ref[...]).astype(BF16) for v in q_lat]
    kvn = [_rms(v, kvg_ref[...]).astype(BF16) for v in kv_lat]
    q = rows([_dot(v, wq_ref[...]) for v in qn])
    n_nope = MLA_HEADS * NOPE_DIM
    x1 = q[:, n_nope:n_nope + LANES]
    x2 = q[:, n_nope + LANES:]
    qT = jnp.concatenate([q[:, :n_nope], x1 * cos8 - x2 * sin8, x2 * cos8 + x1 * sin8], axis=1).T
    pad = jnp.zeros((HEAD_PAD - NOPE_DIM - ROPE_DIM, tm), F32)
    qT = jnp.concatenate(
        [blk for h in range(MLA_HEADS) for blk in (
            qT[h * NOPE_DIM:(h + 1) * NOPE_DIM, :],
            qT[n_nope + h * half:n_nope + (h + 1) * half, :],
            qT[n_nope + LANES + h * half:n_nope + LANES + (h + 1) * half, :], pad)], axis=0).astype(BF16)

    k_nope = rows([_dot(v, wkv_ref[:, 0:MLA_HEADS * HEAD_PAD]) for v in kvn])
    kr = _rope(rows(proj(_C_KR, _C_POOL)), c1, sa1, sb1)
    k_out[...] = (k_nope + jnp.tile(kr, (1, MLA_HEADS))).astype(BF16)
    v = rows([_dot(t, wkv_ref[:, MLA_HEADS * HEAD_PAD:]) for t in kvn])
    for t in range(tm // Q_TILE):
        qT_out[t] = qT[:, t * Q_TILE:(t + 1) * Q_TILE]
    vT = v.T
    ones = jnp.ones((V_ROWS - V_DIM, tm), F32)
    vT = jnp.concatenate(
        [blk for h in range(MLA_HEADS) for blk in (vT[h * V_DIM:(h + 1) * V_DIM, :], ones)], axis=0).astype(BF16)
    for t in range(tm // KV_TILE):
        vT_out[t] = vT[:, t * KV_TILE:(t + 1) * KV_TILE]

    u = rows(proj(_C_POOL, _C_GA))

    @pl.when(i == 0)
    def _():
        hist_ref[...] = jnp.zeros_like(hist_ref)

    ext = jnp.concatenate([hist_ref[...], u], axis=0)
    hist_ref[...] = u[tm - POOL_HALO:, :]
    t_idx = i * tm + lax.broadcasted_iota(jnp.int32, (tm, POOL_GROUP_DIM), 0)
    ys = []
    for g, w in enumerate(POOL_WINDOWS):
        c0 = g * POOL_GROUP_DIM
        run = ext[:, c0:c0 + POOL_GROUP_DIM]
        span = 1
        while span < w:
            run = run + pltpu.roll(run, span, 0)
            span *= 2
        cnt = jnp.minimum(t_idx + 1, w).astype(F32)
        d = run[POOL_HALO:, :] / cnt - u[:, c0:c0 + POOL_GROUP_DIM]
        ys.append(_dot(d.astype(BF16), poolw_ref[g]))
    y = (jnp.concatenate(ys, axis=1) * pscale_ref[...]).astype(BF16)
    y_b = [_dot(y[r, :], wpb_ref[...]) for r in blocks]

    for r, t in zip(blocks, proj(_C_GA, _C_GB)):
        ga_out[r, :] = jax.nn.sigmoid(t).astype(BF16)
    for r, t, yb in zip(blocks, proj(_C_GB, _C_END), y_b):
        gyb_out[r, :] = (jax.nn.sigmoid(t) * yb).astype(BF16)


def _in_proj(x, rot, g, win, qg, wq, kvg, wkv, poolw, pscale, wpb, tm):
    B, S, _ = x.shape
    row = lambda b, i: (b, i, 0)
    const2 = lambda b, i: (0, 0)
    const3 = lambda b, i: (0, 0, 0)
    slab = lambda b, i: (b, i, 0, 0)
    return pl.pallas_call(
        _in_proj_kernel,
        grid=(B, S // tm),
        in_specs=[
            pl.BlockSpec((None, tm, D_MODEL), row),
            pl.BlockSpec((None, ROPE_DIM, tm), lambda b, i: (b, 0, i)),
            pl.BlockSpec((1, D_MODEL), const2),
            pl.BlockSpec(win.shape, const2),
            pl.BlockSpec((1, Q_LORA), const2),
            pl.BlockSpec(wq.shape, const2),
            pl.BlockSpec((1, KV_LORA), const2),
            pl.BlockSpec(wkv.shape, const2),
            pl.BlockSpec(poolw.shape, const3),
            pl.BlockSpec((1, POOL_WIDTH), const2),
            pl.BlockSpec(wpb.shape, const2),
        ],
        out_specs=[
            pl.BlockSpec((None, tm // Q_TILE, MLA_HEADS * HEAD_PAD, Q_TILE), slab),
            pl.BlockSpec((None, tm, MLA_HEADS * HEAD_PAD), row),
            pl.BlockSpec((None, tm // KV_TILE, MLA_HEADS * V_ROWS, KV_TILE), slab),
            pl.BlockSpec((None, tm, D_MODEL), row),
            pl.BlockSpec((None, tm, D_MODEL), row),
        ],
        out_shape=[jax.ShapeDtypeStruct((B, S // Q_TILE, MLA_HEADS * HEAD_PAD, Q_TILE), BF16),
                   jax.ShapeDtypeStruct((B, S, MLA_HEADS * HEAD_PAD), BF16),
                   jax.ShapeDtypeStruct((B, S // KV_TILE, MLA_HEADS * V_ROWS, KV_TILE), BF16),
                   jax.ShapeDtypeStruct((B, S, D_MODEL), BF16),
                   jax.ShapeDtypeStruct((B, S, D_MODEL), BF16)],
        scratch_shapes=[pltpu.VMEM((POOL_HALO, POOL_WIDTH), F32)],
        compiler_params=pltpu.CompilerParams(
            dimension_semantics=("arbitrary", "arbitrary"), vmem_limit_bytes=VMEM_LIMIT),
        name="in_proj",
    )(x, rot, g, win, qg, wq, kvg, wkv, poolw, pscale, wpb)


def _mla_kernel(qT_ref, k_ref, vT_ref, o_ref, m_ref, acc_ref):
    i = pl.program_id(1)
    n_sub = KV_TILE // KEY_SUB
    units = [(h, c) for c in range(n_sub) for h in range(MLA_HEADS)]
    qry_c = lax.broadcasted_iota(jnp.int32, (SCORE_SUB, Q_TILE), 1) // CHUNK
    parts = range(KEY_SUB // SCORE_SUB)

    def scores(j, h, c):
        hs = slice(h * HEAD_PAD, (h + 1) * HEAD_PAD)
        out = []
        for a in parts:
            rows = pl.ds(pl.multiple_of(j * KV_TILE + c * KEY_SUB + a * SCORE_SUB, SCORE_SUB), SCORE_SUB)
            out.append(_dot(k_ref[rows, hs], qT_ref[hs, :]))
        return out

    def fold(h, alpha, pv):
        acc_ref[h] = pv if alpha is None else alpha * acc_ref[h] + pv

    def sweep(j, diag):
        ahead = [scores(j, *u) for u in units[:QK_AHEAD]]
        pending = []
        for n, (h, c) in enumerate(units):
            s = ahead.pop(0)
            if n + QK_AHEAD < len(units):
                ahead.append(scores(j, *units[n + QK_AHEAD]))
            first = False
            if diag is not None:
                for a in parts:
                    key_c0 = (diag * KV_TILE + c * KEY_SUB + a * SCORE_SUB) // CHUNK
                    key_c = key_c0 + lax.broadcasted_iota(jnp.int32, (SCORE_SUB, Q_TILE), 0) // CHUNK
                    s[a] = jnp.where(key_c <= qry_c, s[a], -jnp.inf)
                first = diag == 0 and c == 0
            s_max = functools.reduce(jnp.maximum, [jnp.max(v, axis=0, keepdims=True) for v in s])
            m_new = s_max if first else jnp.maximum(m_ref[h], s_max)
            p = jnp.concatenate([jnp.exp2(v - m_new).astype(BF16) for v in s], axis=0)
            pv = _dot(vT_ref[j, h * V_ROWS:(h + 1) * V_ROWS, c * KEY_SUB:(c + 1) * KEY_SUB], p)
            alpha = None if first else jnp.exp2(m_ref[h] - m_new)
            m_ref[h] = m_new
            pending.append((h, alpha, pv))
            if len(pending) > PV_BEHIND:
                fold(*pending.pop(0))
        for item in pending:
            fold(*item)

    n_diag = Q_TILE // KV_TILE
    for d in range(n_diag):
        sweep(i * n_diag + d, d)

    def body(j, carry):
        sweep(j, None)
        return carry

    lax.fori_loop(0, i * n_diag, body, 0)
    oT = jnp.concatenate([acc_ref[h, :V_DIM, :] / acc_ref[h, V_DIM:V_DIM + 1, :] for h in range(MLA_HEADS)], axis=0)
    o_ref[...] = oT.T.astype(BF16)


def _mla_attention(qT, k, vT):
    B, S, W = k.shape
    return pl.pallas_call(
        _mla_kernel,
        grid=(B, S // Q_TILE),
        in_specs=[
            pl.BlockSpec((None, None, W, Q_TILE), lambda b, i: (b, i, 0, 0)),
            pl.BlockSpec((None, S, W), lambda b, i: (b, 0, 0)),
            pl.BlockSpec((None, S // KV_TILE, MLA_HEADS * V_ROWS, KV_TILE), lambda b, i: (b, 0, 0, 0)),
        ],
        out_specs=pl.BlockSpec((None, Q_TILE, MLA_HEADS * V_DIM), lambda b, i: (b, i, 0)),
        out_shape=jax.ShapeDtypeStruct((B, S, MLA_HEADS * V_DIM), BF16),
        scratch_shapes=[pltpu.VMEM((MLA_HEADS, 1, Q_TILE), F32),
                        pltpu.VMEM((MLA_HEADS, V_ROWS, Q_TILE), F32)],
        compiler_params=pltpu.CompilerParams(
            dimension_semantics=("arbitrary", "arbitrary"), vmem_limit_bytes=VMEM_LIMIT),
        name="mla_attn",
    )(qT, k, vT)


def _route(logits_t, bias_t, tri_upper, tri_lower):
    tm = logits_t.shape[1]
    neg = -jnp.inf
    rg = lax.broadcasted_iota(jnp.int32, (8, tm), 0)
    re = lax.broadcasted_iota(jnp.int32, (N_EXPERTS, tm), 0)
    top = lambda v: jnp.max(v, axis=0, keepdims=True)

    lg = jnp.where(rg < N_GROUPS, logits_t[LANES:LANES + 8, :] + bias_t[LANES:LANES + 8, :], neg)
    ge = jnp.exp(lg - top(lg))
    gp = ge / jnp.sum(ge, axis=0, keepdims=True)
    g_w = top(gp)
    g_idx = jnp.min(jnp.where(gp == g_w, rg, 8), axis=0, keepdims=True)

    sel = re // EXPERTS_PER_GROUP == g_idx
    le = jnp.where(sel, logits_t[:N_EXPERTS, :] + bias_t[:N_EXPERTS, :], neg)
    ee = jnp.exp(le - top(le))
    ep = jnp.where(sel, ee / jnp.sum(ee, axis=0, keepdims=True), -1.0)
    w1 = top(ep)
    i1 = jnp.min(jnp.where(ep == w1, re, N_EXPERTS), axis=0, keepdims=True)
    ep2 = jnp.where(re == i1, -1.0, ep)
    w2 = top(ep2)
    i2 = jnp.min(jnp.where(ep2 == w2, re, N_EXPERTS), axis=0, keepdims=True)
    den = w1 + w2
    c1 = g_w * (w1 / den)
    c2 = g_w * (w2 / den)

    oh1 = (re == i1).astype(F32)
    oh2 = (re == i2).astype(F32)
    both = oh1 + oh2
    earlier = _dot(both.astype(BF16), tri_upper)
    pieces = jnp.floor((jnp.sum(both, axis=1, keepdims=True) + (PIECE - 1)) * (1.0 / PIECE))
    start = _dot(tri_lower, jnp.broadcast_to(pieces, (N_EXPERTS, LANES)).astype(BF16))[:, 0:1] * PIECE
    pos1 = jnp.sum(oh1 * (earlier + start), axis=0, keepdims=True)
    pos2 = jnp.sum(oh2 * (earlier + start), axis=0, keepdims=True)
    info_t = jnp.concatenate([pos1, pos2, c1, c2, jnp.zeros((4, tm), F32)], axis=0)
    return info_t, pieces


def _mix_kernel(x_ref, attn_ref, ga_ref, gyb_ref, kv_ref, wab_ref, wmix_ref, xg_ref, wxq_ref, wxo_ref,
                fg_ref, wr_hi_ref, wr_lo_ref, rb_ref, tri_ref, lower_ref,
                h_out, xn_out, info_out, infoT_out, pieces_out, logits_ref):
    tm = x_ref.shape[0]

    @pl.when(pl.program_id(0) == 0)
    def _():
        logits_ref[...] = jnp.zeros_like(logits_ref)

    info_t, pieces = _route(logits_ref[...].T, rb_ref[...], tri_ref[...], lower_ref[...])
    infoT_out[...] = info_t
    info_out[...] = jnp.concatenate([info_t, jnp.zeros((LANES - 8, tm), F32)], axis=0).T
    pieces_out[...] = jnp.broadcast_to(pieces, (N_EXPERTS, LANES))

    blocks = [slice(b * MIX_SUB, (b + 1) * MIX_SUB) for b in range(tm // MIX_SUB)]
    y_a = [_dot(attn_ref[r, :], wab_ref[...]) for r in blocks]
    merged = [(ga_ref[r, :].astype(F32) * y + gyb_ref[r, :].astype(F32)).astype(BF16) for r, y in zip(blocks, y_a)]
    h1 = [x_ref[r, :] + _dot(m, wmix_ref[...]) for r, m in zip(blocks, merged)]

    hn = [_rms(h, xg_ref[...]).astype(BF16) for h in h1]
    q = [_dot(v, wxq_ref[...]).astype(BF16) for v in hn]
    pairs = [(b, h) for h in range(MEM_HEADS) for b in range(len(blocks))]
    hs = lambda h: slice(h * MEM_HEAD_DIM, (h + 1) * MEM_HEAD_DIM)
    vs = lambda h: slice(D_MODEL + h * MEM_HEAD_DIM, D_MODEL + (h + 1) * MEM_HEAD_DIM)
    s = [_dot_nt(q[b][:, hs(h)], kv_ref[:, hs(h)]) for b, h in pairs]
    p = [jnp.exp(v - jnp.max(v, axis=-1, keepdims=True)) for v in s]
    o = [_dot(v.astype(BF16), kv_ref[:, vs(h)]) for v, (b, h) in zip(p, pairs)]
    heads = [[None] * MEM_HEADS for _ in blocks]
    for (b, h), ov, pv in zip(pairs, o, p):
        heads[b][h] = (ov / jnp.sum(pv, axis=-1, keepdims=True)).astype(BF16)
    h2 = [h + _dot(jnp.concatenate(hd, axis=1), wxo_ref[...]) for h, hd in zip(h1, heads)]

    xn = [_rms(h, fg_ref[...]) for h in h2]
    xn_hi = [v.astype(BF16) for v in xn]
    xn_lo = [(v - hi.astype(F32)).astype(BF16) for v, hi in zip(xn, xn_hi)]
    logits = [_dot(hi, wr_hi_ref[...]) + (_dot(hi, wr_lo_ref[...]) + _dot(lo, wr_hi_ref[...]))
              for hi, lo in zip(xn_hi, xn_lo)]
    for r, h, hi in zip(blocks, h2, xn_hi):
        h_out[r, :] = h
        xn_out[r, :] = hi
    for r, v in zip(blocks, logits):
        logits_ref[r, :] = v


def _mix_xattn(x, attn, ga, gyb, memkv, wab, wmix, xg, wxq, wxo, fg, wr_hi, wr_lo, rb):
    B, S, _ = x.shape
    M = memkv.shape[1]
    tm = MOE_CHUNK
    nt = S // tm
    n_tiles = B * nt
    cur = lambda t: jnp.minimum(t, n_tiles - 1)
    prev = lambda t: jnp.maximum(t - 1, 0)
    row = lambda t: (cur(t) // nt, cur(t) % nt, 0)
    const2 = lambda t: (0, 0)
    tri = (lax.broadcasted_iota(jnp.int32, (tm, tm), 0) < lax.broadcasted_iota(jnp.int32, (tm, tm), 1)).astype(BF16)
    lower = (lax.broadcasted_iota(jnp.int32, (N_EXPERTS, N_EXPERTS), 1)
             < lax.broadcasted_iota(jnp.int32, (N_EXPERTS, N_EXPERTS), 0)).astype(BF16)
    rb = jnp.broadcast_to(rb.reshape(2 * LANES, 1), (2 * LANES, tm))
    return pl.pallas_call(
        _mix_kernel,
        grid=(n_tiles + 1,),
        in_specs=[
            pl.BlockSpec((None, tm, D_MODEL), row),
            pl.BlockSpec((None, tm, MLA_HEADS * V_DIM), row),
            pl.BlockSpec((None, tm, D_MODEL), row),
            pl.BlockSpec((None, tm, D_MODEL), row),
            pl.BlockSpec((None, M, 2 * D_MODEL), lambda t: (cur(t) // nt, 0, 0)),
            pl.BlockSpec(wab.shape, const2),
            pl.BlockSpec(wmix.shape, const2),
            pl.BlockSpec((1, D_MODEL), const2),
            pl.BlockSpec(wxq.shape, const2),
            pl.BlockSpec(wxo.shape, const2),
            pl.BlockSpec((1, D_MODEL), const2),
            pl.BlockSpec(wr_hi.shape, const2),
            pl.BlockSpec(wr_lo.shape, const2),
            pl.BlockSpec((2 * LANES, tm), const2),
            pl.BlockSpec((tm, tm), const2),
            pl.BlockSpec((N_EXPERTS, N_EXPERTS), const2),
        ],
        out_specs=[
            pl.BlockSpec((None, tm, D_MODEL), row),
            pl.BlockSpec((None, tm, D_MODEL), row),
            pl.BlockSpec((None, tm, LANES), lambda t: (prev(t) // nt, prev(t) % nt, 0)),
            pl.BlockSpec((8, tm), lambda t: (0, prev(t))),
            pl.BlockSpec((None, N_EXPERTS, LANES), lambda t: (prev(t), 0, 0)),
        ],
        out_shape=[jax.ShapeDtypeStruct((B, S, D_MODEL), F32),
                   jax.ShapeDtypeStruct((B, S, D_MODEL), BF16),
                   jax.ShapeDtypeStruct((B, S, LANES), F32),
                   jax.ShapeDtypeStruct((8, B * S), F32),
                   jax.ShapeDtypeStruct((B * nt, N_EXPERTS, LANES), F32)],
        scratch_shapes=[pltpu.VMEM((tm, 2 * LANES), F32)],
        compiler_params=pltpu.CompilerParams(
            dimension_semantics=("arbitrary",), vmem_limit_bytes=VMEM_LIMIT),
        name="mix_xattn",
    )(x, attn, ga, gyb, memkv, wab, wmix, xg, wxq, wxo, fg, wr_hi, wr_lo, rb, tri, lower)


def _piece_copy(src_ref, dst_ref, sem):
    return pltpu.make_async_copy(src_ref, dst_ref, sem)


def _dispatch_kernel(dst_ref, np_ref, gap_ref, fill_ref, xn_ref, infoT_ref, xs_hbm, buf_ref, zero_ref, sem_ref):
    c = pl.program_id(0)
    n = pl.num_programs(0)
    slot = c % 2

    tile_pieces = EXPERT_TILE // PIECE
    n_tiles = xs_hbm.shape[0] // tile_pieces

    def gap_copy(g):
        return _piece_copy(zero_ref.at[0], xs_hbm.at[gap_ref[g]], sem_ref.at[2])

    def tail_copy(t):
        return _piece_copy(zero_ref, xs_hbm.at[pl.ds(t * tile_pieces, tile_pieces)], sem_ref.at[2])

    @pl.when(c == 0)
    def _():
        zero_ref[...] = jnp.zeros_like(zero_ref)
        lax.fori_loop(0, fill_ref[0], lambda g, carry: (gap_copy(g).start(), carry)[1], 0)
        lax.fori_loop(fill_ref[1], n_tiles, lambda t, carry: (tail_copy(t).start(), carry)[1], 0)

    def copy(cc, s, q):
        return _piece_copy(buf_ref.at[s, q], xs_hbm.at[dst_ref[cc * MAX_PIECES + q]], sem_ref.at[s])

    def start_all(cc, s):
        lax.fori_loop(0, np_ref[cc], lambda q, carry: (copy(cc, s, q).start(), carry)[1], 0)

    def wait_all(cc, s):
        lax.fori_loop(0, np_ref[cc], lambda q, carry: (copy(cc, s, q).wait(), carry)[1], 0)

    @pl.when(c >= 2)
    def _():
        wait_all(c - 2, slot)

    pos1 = infoT_ref[0:1, :]
    pos2 = infoT_ref[1:2, :]
    r = lax.broadcasted_iota(jnp.int32, (CHUNK_ROWS, MOE_CHUNK), 0).astype(F32)
    onehot = jnp.where((r == pos1) | (r == pos2), 1.0, 0.0).astype(BF16)
    buf_ref[slot] = _dot(onehot, xn_ref[...]).astype(BF16).reshape(MAX_PIECES, PIECE, D_MODEL)
    start_all(c, slot)

    @pl.when(c == n - 1)
    def _():
        @pl.when(c >= 1)
        def _():
            wait_all(c - 1, 1 - slot)
        wait_all(c, slot)
        lax.fori_loop(0, fill_ref[0], lambda g, carry: (gap_copy(g).wait(), carry)[1], 0)
        lax.fori_loop(fill_ref[1], n_tiles, lambda t, carry: (tail_copy(t).wait(), carry)[1], 0)


def _dispatch(xn, infoT, dst, npc, gaps, fill, rows_max):
    T = xn.shape[0]
    grid_spec = pltpu.PrefetchScalarGridSpec(
        num_scalar_prefetch=4,
        grid=(T // MOE_CHUNK,),
        in_specs=[
            pl.BlockSpec((MOE_CHUNK, D_MODEL), lambda c, *_: (c, 0)),
            pl.BlockSpec((8, MOE_CHUNK), lambda c, *_: (0, c)),
        ],
        out_specs=pl.BlockSpec(memory_space=pl.ANY),
        scratch_shapes=[pltpu.VMEM((2, MAX_PIECES, PIECE, D_MODEL), BF16),
                        pltpu.VMEM((EXPERT_TILE // PIECE, PIECE, D_MODEL), BF16),
                        pltpu.SemaphoreType.DMA((3,))],
    )
    xs = pl.pallas_call(
        _dispatch_kernel,
        grid_spec=grid_spec,
        out_shape=jax.ShapeDtypeStruct((rows_max // PIECE, PIECE, D_MODEL), BF16),
        compiler_params=pltpu.CompilerParams(
            dimension_semantics=("arbitrary",), vmem_limit_bytes=VMEM_LIMIT),
        name="moe_dispatch",
    )(dst, npc, gaps, fill, xn, infoT)
    return xs.reshape(rows_max, D_MODEL)


def _expert_kernel(te_ref, tv_ref, nu_ref, x_ref, wg_ref, wu_ref, wd_ref, y_ref, wgu_bf, wd_bf):
    i = pl.program_id(0)

    @pl.when(i < nu_ref[0])
    def _():
        @pl.when((i == 0) | (te_ref[i] != te_ref[jnp.maximum(i - 1, 0)]))
        def _():
            wgu_bf[:, :EXPERT_FF] = wg_ref[...].astype(BF16)
            wgu_bf[:, EXPERT_FF:] = wu_ref[...].astype(BF16)
            wd_bf[...] = wd_ref[...].astype(BF16)

        blocks = [slice(b * EXPERT_SUB, (b + 1) * EXPERT_SUB) for b in range(EXPERT_TILE // EXPERT_SUB)]
        row = lax.broadcasted_iota(jnp.int32, (EXPERT_SUB, D_MODEL), 0)
        gus = []
        for b, rows in enumerate(blocks):
            x = x_ref[rows, :]
            x = jnp.where(row < tv_ref[i] - b * EXPERT_SUB, x, jnp.zeros_like(x))
            gus.append(_dot(x, wgu_bf[...]))
        hids = []
        for gu in gus:
            gate = gu[:, :EXPERT_FF]
            hids.append((gate * jax.nn.sigmoid(gate) * gu[:, EXPERT_FF:]).astype(BF16))
        for rows, hid in zip(blocks, hids):
            y_ref[rows, :] = _dot(hid, wd_bf[...]).astype(BF16)

    @pl.when(i >= nu_ref[0])
    def _():
        y_ref[...] = jnp.zeros_like(y_ref)


def _experts(xs, w_gate, w_up, w_down, tile_expert, tile_valid, n_used):
    rows_max = xs.shape[0]
    last = lambda i, nu: jnp.minimum(i, nu[0] - 1)
    expert = lambda i, te, tv, nu: (te[last(i, nu)], 0, 0)
    grid_spec = pltpu.PrefetchScalarGridSpec(
        num_scalar_prefetch=3,
        grid=(rows_max // EXPERT_TILE,),
        in_specs=[
            pl.BlockSpec((EXPERT_TILE, D_MODEL), lambda i, te, tv, nu: (last(i, nu), 0)),
            pl.BlockSpec((None, D_MODEL, EXPERT_FF), expert),
            pl.BlockSpec((None, D_MODEL, EXPERT_FF), expert),
            pl.BlockSpec((None, EXPERT_FF, D_MODEL), expert),
        ],
        out_specs=pl.BlockSpec((EXPERT_TILE, D_MODEL), lambda i, te, tv, nu: (i, 0)),
        scratch_shapes=[pltpu.VMEM((D_MODEL, 2 * EXPERT_FF), BF16), pltpu.VMEM((EXPERT_FF, D_MODEL), BF16)],
    )
    return pl.pallas_call(
        _expert_kernel,
        grid_spec=grid_spec,
        out_shape=jax.ShapeDtypeStruct((rows_max, D_MODEL), BF16),
        compiler_params=pltpu.CompilerParams(
            dimension_semantics=("arbitrary",), vmem_limit_bytes=VMEM_LIMIT),
        name="moe_experts",
    )(tile_expert, tile_valid, n_used, xs, w_gate, w_up, w_down)


def _combine_kernel(dst_ref, np_ref, h_ref, info_ref, fg_ref, ys_hbm, o_ref, buf_ref, sem_ref):
    c = pl.program_id(0)
    n = pl.num_programs(0)
    slot = c % 2

    def copy(cc, s, q):
        return _piece_copy(ys_hbm.at[dst_ref[cc * MAX_PIECES + q]], buf_ref.at[s, q], sem_ref.at[s])

    def start_all(cc, s):
        lax.fori_loop(0, np_ref[cc], lambda q, carry: (copy(cc, s, q).start(), carry)[1], 0)

    def wait_all(cc, s):
        lax.fori_loop(0, np_ref[cc], lambda q, carry: (copy(cc, s, q).wait(), carry)[1], 0)

    @pl.when(c == 0)
    def _():
        buf_ref[...] = jnp.zeros_like(buf_ref)
        start_all(0, 0)

    @pl.when(c + 1 < n)
    def _():
        start_all(c + 1, 1 - slot)

    wait_all(c, slot)
    info = info_ref[...]
    r = lax.broadcasted_iota(jnp.int32, (MOE_CHUNK, CHUNK_ROWS), 1).astype(F32)
    weights = jnp.where(r == info[:, 0:1], info[:, 2:3], 0.0) + jnp.where(r == info[:, 1:2], info[:, 3:4], 0.0)
    moe = _dot(weights.astype(BF16), buf_ref[slot].reshape(CHUNK_ROWS, D_MODEL))
    o_ref[...] = _rms(h_ref[...] + moe, fg_ref[...])


def _combine(h2, info, fg, ys, dst, npc):
    T = h2.shape[0]
    grid_spec = pltpu.PrefetchScalarGridSpec(
        num_scalar_prefetch=2,
        grid=(T // MOE_CHUNK,),
        in_specs=[
            pl.BlockSpec((MOE_CHUNK, D_MODEL), lambda c, dst, npc: (c, 0)),
            pl.BlockSpec((MOE_CHUNK, LANES), lambda c, dst, npc: (c, 0)),
            pl.BlockSpec((1, D_MODEL), lambda c, dst, npc: (0, 0)),
            pl.BlockSpec(memory_space=pl.ANY),
        ],
        out_specs=pl.BlockSpec((MOE_CHUNK, D_MODEL), lambda c, dst, npc: (c, 0)),
        scratch_shapes=[pltpu.VMEM((2, MAX_PIECES, PIECE, D_MODEL), BF16), pltpu.SemaphoreType.DMA((2,))],
    )
    ys = ys.reshape(-1, PIECE, D_MODEL)
    return pl.pallas_call(
        _combine_kernel,
        grid_spec=grid_spec,
        out_shape=jax.ShapeDtypeStruct((T, D_MODEL), F32),
        compiler_params=pltpu.CompilerParams(
            dimension_semantics=("arbitrary",), vmem_limit_bytes=VMEM_LIMIT),
        name="moe_combine",
    )(dst, npc, h2, info, fg, ys)


def _routing_tables(pieces, rows_max):
    tile_pieces = EXPERT_TILE // PIECE
    total = jnp.sum(pieces, axis=0)
    total_al = (total + tile_pieces - 1) // tile_pieces * tile_pieces
    seg_end = jnp.cumsum(total_al)
    seg_start = seg_end - total_al
    chunk_off = jnp.cumsum(pieces, axis=0) - pieces
    loc_end = jnp.cumsum(pieces, axis=1)
    loc_start = loc_end - pieces
    q = jnp.arange(MAX_PIECES, dtype=jnp.int32)
    owner = jnp.sum((q[None, :, None] >= loc_end[:, None, :]).astype(jnp.int32), axis=-1)
    owner = jnp.minimum(owner, N_EXPERTS - 1)
    experts = jnp.arange(N_EXPERTS, dtype=jnp.int32)
    is_owner = (owner[:, :, None] == experts).astype(jnp.int32)
    offset = seg_start[None, :] + chunk_off - loc_start
    dst = jnp.sum(is_owner * offset[:, None, :], axis=-1) + q[None, :]
    n_local = loc_end[:, -1]
    dst = jnp.where(q[None, :] < n_local[:, None], dst, 0)

    t0 = jnp.arange(rows_max // EXPERT_TILE, dtype=jnp.int32) * tile_pieces
    tile_expert = jnp.minimum(jnp.sum((t0[:, None] >= seg_end[None, :]).astype(jnp.int32), axis=-1), N_EXPERTS - 1)
    copies_end = jnp.sum((tile_expert[:, None] == experts).astype(jnp.int32) * (seg_start + total)[None, :], axis=-1)
    tile_valid = jnp.clip((copies_end - t0) * PIECE, 0, EXPERT_TILE)
    n_used = (seg_end[-1] // tile_pieces).reshape(1)

    k = jnp.arange(tile_pieces, dtype=jnp.int32)
    is_gap = (k[None, :] < (total_al - total)[:, None]).reshape(-1)
    gap_piece = (seg_start + total)[:, None] + k[None, :]
    order = jnp.argsort(jnp.logical_not(is_gap), stable=True)
    gaps = gap_piece.reshape(-1)[order]
    fill = jnp.stack([jnp.sum(is_gap.astype(jnp.int32)), n_used[0]])
    i32 = lambda a: a.astype(jnp.int32)
    return i32(dst.reshape(-1)), i32(n_local), i32(tile_expert), i32(tile_valid), i32(n_used), i32(gaps), i32(fill)


def _rope_tables(positions):
    inv_freq = 1.0 / (ROPE_THETA ** (jnp.arange(0, ROPE_DIM, 2, dtype=F32) / ROPE_DIM))
    ang = positions.astype(F32)[:, None, :] * inv_freq[None, :, None]
    cos, sin = jnp.cos(ang), jnp.sin(ang)
    return jnp.concatenate([cos, sin], axis=1)


def _pad_heads(w, heads, width):
    k = w.shape[0]
    w = w.reshape(k, heads, width)
    w = jnp.pad(w, ((0, 0), (0, 0), (0, HEAD_PAD - width)))
    return w.reshape(k, heads * HEAD_PAD)


def _layer(l, h, mem, tables, mix_norm_g, w_in, q_norm_g, w_q_up, kv_norm_g, w_kv_up, w_attn_branch,
           pool_w, pool_scale, w_pool_branch, w_mix_out, xattn_norm_g, mem_norm_g, w_xq, w_xkv, w_xo,
           ffn_norm_g, w_router_group, b_router_group, w_router_expert, b_router_expert,
           w_exp_gate, w_exp_up, w_exp_down, out_g, tm_proj):
    B, S, _ = h.shape
    row2 = lambda v: v.reshape(1, -1).astype(F32)

    wi = w_in[l]
    kr_cols = jnp.pad(wi[:, Q_LORA + KV_LORA:Q_LORA + KV_LORA + ROPE_DIM],
                      ((0, 0), (NOPE_DIM, LANES - NOPE_DIM - ROPE_DIM)))
    win = jnp.concatenate([wi[:, :Q_LORA + KV_LORA], kr_cols, wi[:, Q_LORA + KV_LORA + ROPE_DIM:]], axis=1).astype(BF16)
    scale = math.log2(math.e) / math.sqrt(NOPE_DIM + ROPE_DIM)
    wq3 = (w_q_up[l] * scale).reshape(Q_LORA, MLA_HEADS, NOPE_DIM + ROPE_DIM)
    half = ROPE_DIM // 2
    wq = jnp.concatenate([wq3[:, :, :NOPE_DIM].reshape(Q_LORA, -1),
                          wq3[:, :, NOPE_DIM:NOPE_DIM + half].reshape(Q_LORA, -1),
                          wq3[:, :, NOPE_DIM + half:].reshape(Q_LORA, -1)], axis=1).astype(BF16)
    wkv3 = w_kv_up[l].reshape(KV_LORA, MLA_HEADS, NOPE_DIM + V_DIM)
    wkv = jnp.concatenate([
        _pad_heads(wkv3[:, :, :NOPE_DIM].reshape(KV_LORA, -1), MLA_HEADS, NOPE_DIM),
        wkv3[:, :, NOPE_DIM:].reshape(KV_LORA, -1)], axis=1).astype(BF16)

    memkv = _mem_kv(mem.reshape(-1, D_MODEL), row2(mem_norm_g[l]), w_xkv[l].astype(BF16))
    memkv = memkv.reshape(B, -1, 2 * D_MODEL)

    qT, k, vT, ga, gyb = _in_proj(
        h, tables, row2(mix_norm_g[l]), win, row2(q_norm_g[l]), wq, row2(kv_norm_g[l]), wkv,
        pool_w[l].astype(BF16), row2(pool_scale[l]), w_pool_branch[l].astype(BF16), tm_proj)
    attn = _mla_attention(qT, k, vT)

    w_r = jnp.zeros((D_MODEL, 2 * LANES), F32)
    w_r = w_r.at[:, :N_EXPERTS].set(w_router_expert[l]).at[:, LANES:LANES + N_GROUPS].set(w_router_group[l])
    wr_hi = w_r.astype(BF16)
    wr_lo = (w_r - wr_hi.astype(F32)).astype(BF16)
    rb = jnp.zeros((1, 2 * LANES), F32)
    rb = rb.at[0, :N_EXPERTS].set(b_router_expert[l]).at[0, LANES:LANES + N_GROUPS].set(b_router_group[l])

    h2, xn, info, infoT, pieces = _mix_xattn(
        h, attn, ga, gyb, memkv, w_attn_branch[l].astype(BF16), w_mix_out[l].astype(BF16),
        row2(xattn_norm_g[l]), (w_xq[l] * (1.0 / math.sqrt(MEM_HEAD_DIM))).astype(BF16), w_xo[l].astype(BF16),
        row2(ffn_norm_g[l]), wr_hi, wr_lo, rb)

    T = B * S
    n_chunks = T // MOE_CHUNK
    tile_pieces = EXPERT_TILE // PIECE
    max_pieces = 2 * T // PIECE + n_chunks * N_EXPERTS + N_EXPERTS * tile_pieces
    rows_max = -(-max_pieces // tile_pieces) * EXPERT_TILE
    dst, n_local, tile_expert, tile_valid, n_used, gaps, fill = _routing_tables(
        pieces[:, :, 0].astype(jnp.int32), rows_max)

    xs = _dispatch(xn.reshape(T, D_MODEL), infoT, dst, n_local, gaps, fill, rows_max)
    ys = _experts(xs, w_exp_gate[l], w_exp_up[l], w_exp_down[l], tile_expert, tile_valid, n_used)
    out = _combine(h2.reshape(T, D_MODEL), info.reshape(T, LANES), row2(out_g), ys, dst, n_local)
    return out.reshape(B, S, D_MODEL)


def kernel(x, mem, positions, mix_norm_g, w_in, q_norm_g, w_q_up, kv_norm_g, w_kv_up, w_attn_branch, pool_w, pool_scale, w_pool_branch, w_mix_out, xattn_norm_g, mem_norm_g, w_xq, w_xkv, w_xo, ffn_norm_g, w_router_group, b_router_group, w_router_expert, b_router_expert, w_exp_gate, w_exp_up, w_exp_down, final_norm_g):
    depth = w_in.shape[0]
    assert depth == 1, "the combine kernel fuses the final RMSNorm, which is only valid after the last layer"
    assert x.shape[1] % Q_TILE == 0 and x.shape[1] % MOE_CHUNK == 0
    tables = _rope_tables(positions)
    return _layer(0, x, mem, tables, mix_norm_g, w_in, q_norm_g, w_q_up, kv_norm_g, w_kv_up, w_attn_branch,
                  pool_w, pool_scale, w_pool_branch, w_mix_out, xattn_norm_g, mem_norm_g, w_xq, w_xkv, w_xo,
                  ffn_norm_g, w_router_group, b_router_group, w_router_expert, b_router_expert,
                  w_exp_gate, w_exp_up, w_exp_down, final_norm_g, IN_PROJ_TILE)
```

```python
import functools
import math

import jax
import jax.numpy as jnp
from jax import lax
from jax.experimental import pallas as pl
from jax.experimental.pallas import tpu as pltpu

F32 = jnp.float32
BF16 = jnp.bfloat16

D_MODEL = 1024
CHUNK = 64
MLA_HEADS = 8
Q_LORA = 384
KV_LORA = 256
NOPE_DIM = 64
ROPE_DIM = 32
V_DIM = 64
ROPE_THETA = 10000.0
POOL_WIDTH = 512
POOL_WINDOWS = (2, 4, 8, 16)
POOL_GROUP_DIM = POOL_WIDTH // len(POOL_WINDOWS)
POOL_HALO = 16
MEM_HEADS = 4
MEM_HEAD_DIM = D_MODEL // MEM_HEADS
N_GROUPS = 4
EXPERTS_PER_GROUP = 8
N_EXPERTS = N_GROUPS * EXPERTS_PER_GROUP
EXPERT_FF = 256
EPS = 1e-6

LANES = 128
HEAD_PAD = LANES
Q_TILE = 512
KV_TILE = 512
KEY_SUB = 128
SCORE_SUB = 128
V_ROWS = V_DIM + 16
IN_PROJ_TILE = 512
MOE_CHUNK = 512
PIECE = 16
MAX_PIECES = 2 * MOE_CHUNK // PIECE + N_EXPERTS
CHUNK_ROWS = MAX_PIECES * PIECE
EXPERT_TILE = 1024
EXPERT_SUB = 256
PROJ_SUB = 256
MIX_SUB = 256
QK_AHEAD = 3
PV_BEHIND = 2

_C_Q = 0
_C_KV = _C_Q + Q_LORA
_C_KR = _C_KV + KV_LORA
_C_POOL = _C_KR + LANES
_C_GA = _C_POOL + POOL_WIDTH
_C_GB = _C_GA + D_MODEL
_C_END = _C_GB + D_MODEL

VMEM_LIMIT = 56 * 1024 * 1024


def _rms(x, g):
    return x * lax.rsqrt(jnp.mean(x * x, axis=-1, keepdims=True) + EPS) * g


def _dot(a, b):
    return jnp.dot(a, b, preferred_element_type=F32)


def _dot_nt(a, b):
    return lax.dot_general(a, b, (((1,), (1,)), ((), ())), preferred_element_type=F32)


def _mem_kv_kernel(mem_ref, g_ref, w_ref, kv_ref):
    mn = _rms(mem_ref[...], g_ref[...]).astype(BF16)
    kv_ref[...] = _dot(mn, w_ref[...]).astype(BF16)


def _mem_kv(mem2d, g, w_xkv):
    rows = mem2d.shape[0]
    tm = min(512, rows)
    assert rows % tm == 0
    return pl.pallas_call(
        _mem_kv_kernel,
        grid=(rows // tm,),
        in_specs=[
            pl.BlockSpec((tm, D_MODEL), lambda i: (i, 0)),
            pl.BlockSpec((1, D_MODEL), lambda i: (0, 0)),
            pl.BlockSpec((D_MODEL, 2 * D_MODEL), lambda i: (0, 0)),
        ],
        out_specs=pl.BlockSpec((tm, 2 * D_MODEL), lambda i: (i, 0)),
        out_shape=jax.ShapeDtypeStruct((rows, 2 * D_MODEL), BF16),
        compiler_params=pltpu.CompilerParams(vmem_limit_bytes=VMEM_LIMIT),
        name="mem_kv",
    )(mem2d, g, w_xkv)


def _rope(t, c, sa, sb):
    w = t.shape[-1]
    return t * c + pltpu.roll(t, ROPE_DIM // 2, 1) * sa + pltpu.roll(t, w - ROPE_DIM // 2, 1) * sb


def _in_proj_kernel(x_ref, rot_ref, g_ref, win_ref, qg_ref, wq_ref, kvg_ref, wkv_ref,
                    poolw_ref, pscale_ref, wpb_ref,
                    qT_out, k_out, vT_out, ga_out, gyb_out, hist_ref):
    tm = x_ref.shape[0]
    i = pl.program_id(1)
    blocks = [slice(b * PROJ_SUB, (b + 1) * PROJ_SUB) for b in range(tm // PROJ_SUB)]
    rows = lambda parts: jnp.concatenate(parts, axis=0)
    hn = [_rms(x_ref[r, :], g_ref[...]).astype(BF16) for r in blocks]
    proj = lambda lo, hi: [_dot(v, win_ref[:, lo:hi]) for v in hn]

    half = ROPE_DIM // 2
    cos8 = jnp.concatenate([rot_ref[0:half, :]] * MLA_HEADS, axis=0).T
    sin8 = jnp.concatenate([rot_ref[half:, :]] * MLA_HEADS, axis=0).T
    lane = lax.broadcasted_iota(jnp.int32, cos8.shape, 1)
    rope_lo = NOPE_DIM
    c1 = jnp.where(lane < rope_lo + ROPE_DIM, cos8, 0.0)
    sa1 = jnp.where((lane >= rope_lo + half) & (lane < rope_lo + ROPE_DIM), sin8, 0.0)
    sb1 = jnp.where((lane >= rope_lo) & (lane < rope_lo + half), -sin8, 0.0)

    q_lat = proj(_C_Q, _C_KV)
    kv_lat = proj(_C_KV, _C_KR)
    qn = [_rms(v, qg_ref[...]).astype(BF16) for v in q_lat]
    kvn = [_rms(v, kvg_ref[...]).astype(BF16) for v in kv_lat]
    q = rows([_dot(v, wq_ref[...]) for v in qn])
    n_nope = MLA_HEADS * NOPE_DIM
    x1 = q[:, n_nope:n_nope + LANES]
    x2 = q[:, n_nope + LANES:]
    qT = jnp.concatenate([q[:, :n_nope], x1 * cos8 - x2 * sin8, x2 * cos8 + x1 * sin8], axis=1).T
    pad = jnp.zeros((HEAD_PAD - NOPE_DIM - ROPE_DIM, tm), F32)
    qT = jnp.concatenate(
        [blk for h in range(MLA_HEADS) for blk in (
            qT[h * NOPE_DIM:(h + 1) * NOPE_DIM, :],
            qT[n_nope + h * half:n_nope + (h + 1) * half, :],
            qT[n_nope + LANES + h * half:n_nope + LANES + (h + 1) * half, :], pad)], axis=0).astype(BF16)

    k_nope = rows([_dot(v, wkv_ref[:, 0:MLA_HEADS * HEAD_PAD]) for v in kvn])
    kr = _rope(rows(proj(_C_KR, _C_POOL)), c1, sa1, sb1)
    k_out[...] = (k_nope + jnp.tile(kr, (1, MLA_HEADS))).astype(BF16)
    v = rows([_dot(t, wkv_ref[:, MLA_HEADS * HEAD_PAD:]) for t in kvn])
    for t in range(tm // Q_TILE):
        qT_out[t] = qT[:, t * Q_TILE:(t + 1) * Q_TILE]
    vT = v.T
    ones = jnp.ones((V_ROWS - V_DIM, tm), F32)
    vT = jnp.concatenate(
        [blk for h in range(MLA_HEADS) for blk in (vT[h * V_DIM:(h + 1) * V_DIM, :], ones)], axis=0).astype(BF16)
    for t in range(tm // KV_TILE):
        vT_out[t] = vT[:, t * KV_TILE:(t + 1) * KV_TILE]

    u = rows(proj(_C_POOL, _C_GA))

    @pl.when(i == 0)
    def _():
        hist_ref[...] = jnp.zeros_like(hist_ref)

    ext = jnp.concatenate([hist_ref[...], u], axis=0)
    hist_ref[...] = u[tm - POOL_HALO:, :]
    t_idx = i * tm + lax.broadcasted_iota(jnp.int32, (tm, POOL_GROUP_DIM), 0)
    ys = []
    for g, w in enumerate(POOL_WINDOWS):
        c0 = g * POOL_GROUP_DIM
        run = ext[:, c0:c0 + POOL_GROUP_DIM]
        span = 1
        while span < w:
            run = run + pltpu.roll(run, span, 0)
            span *= 2
        cnt = jnp.minimum(t_idx + 1, w).astype(F32)
        d = run[POOL_HALO:, :] / cnt - u[:, c0:c0 + POOL_GROUP_DIM]
        ys.append(_dot(d.astype(BF16), poolw_ref[g]))
    y = (jnp.concatenate(ys, axis=1) * pscale_ref[...]).astype(BF16)
    y_b = [_dot(y[r, :], wpb_ref[...]) for r in blocks]

    for r, t in zip(blocks, proj(_C_GA, _C_GB)):
        ga_out[r, :] = jax.nn.sigmoid(t).astype(BF16)
    for r, t, yb in zip(blocks, proj(_C_GB, _C_END), y_b):
        gyb_out[r, :] = (jax.nn.sigmoid(t) * yb).astype(BF16)


def _in_proj(x, rot, g, win, qg, wq, kvg, wkv, poolw, pscale, wpb, tm):
    B, S, _ = x.shape
    row = lambda b, i: (b, i, 0)
    const2 = lambda b, i: (0, 0)
    const3 = lambda b, i: (0, 0, 0)
    slab = lambda b, i: (b, i, 0, 0)
    return pl.pallas_call(
        _in_proj_kernel,
        grid=(B, S // tm),
        in_specs=[
            pl.BlockSpec((None, tm, D_MODEL), row),
            pl.BlockSpec((None, ROPE_DIM, tm), lambda b, i: (b, 0, i)),
            pl.BlockSpec((1, D_MODEL), const2),
            pl.BlockSpec(win.shape, const2),
            pl.BlockSpec((1, Q_LORA), const2),
            pl.BlockSpec(wq.shape, const2),
            pl.BlockSpec((1, KV_LORA), const2),
            pl.BlockSpec(wkv.shape, const2),
            pl.BlockSpec(poolw.shape, const3),
            pl.BlockSpec((1, POOL_WIDTH), const2),
            pl.BlockSpec(wpb.shape, const2),
        ],
        out_specs=[
            pl.BlockSpec((None, tm // Q_TILE, MLA_HEADS * HEAD_PAD, Q_TILE), slab),
            pl.BlockSpec((None, tm, MLA_HEADS * HEAD_PAD), row),
            pl.BlockSpec((None, tm // KV_TILE, MLA_HEADS * V_ROWS, KV_TILE), slab),
            pl.BlockSpec((None, tm, D_MODEL), row),
            pl.BlockSpec((None, tm, D_MODEL), row),
        ],
        out_shape=[jax.ShapeDtypeStruct((B, S // Q_TILE, MLA_HEADS * HEAD_PAD, Q_TILE), BF16),
                   jax.ShapeDtypeStruct((B, S, MLA_HEADS * HEAD_PAD), BF16),
                   jax.ShapeDtypeStruct((B, S // KV_TILE, MLA_HEADS * V_ROWS, KV_TILE), BF16),
                   jax.ShapeDtypeStruct((B, S, D_MODEL), BF16),
                   jax.ShapeDtypeStruct((B, S, D_MODEL), BF16)],
        scratch_shapes=[pltpu.VMEM((POOL_HALO, POOL_WIDTH), F32)],
        compiler_params=pltpu.CompilerParams(
            dimension_semantics=("arbitrary", "arbitrary"), vmem_limit_bytes=VMEM_LIMIT),
        name="in_proj",
    )(x, rot, g, win, qg, wq, kvg, wkv, poolw, pscale, wpb)


def _mla_kernel(qT_ref, k_ref, vT_ref, o_ref, m_ref, acc_ref):
    i = pl.program_id(1)
    n_sub = KV_TILE // KEY_SUB
    units = [(h, c) for c in range(n_sub) for h in range(MLA_HEADS)]
    qry_c = lax.broadcasted_iota(jnp.int32, (SCORE_SUB, Q_TILE), 1) // CHUNK
    parts = range(KEY_SUB // SCORE_SUB)

    def scores(j, h, c):
        hs = slice(h * HEAD_PAD, (h + 1) * HEAD_PAD)
        out = []
        for a in parts:
            rows = pl.ds(pl.multiple_of(j * KV_TILE + c * KEY_SUB + a * SCORE_SUB, SCORE_SUB), SCORE_SUB)
            out.append(_dot(k_ref[rows, hs], qT_ref[hs, :]))
        return out

    def fold(h, alpha, pv):
        acc_ref[h] = pv if alpha is None else alpha * acc_ref[h] + pv

    def sweep(j, diag):
        ahead = [scores(j, *u) for u in units[:QK_AHEAD]]
        pending = []
        for n, (h, c) in enumerate(units):
            s = ahead.pop(0)
            if n + QK_AHEAD < len(units):
                ahead.append(scores(j, *units[n + QK_AHEAD]))
            first = False
            if diag is not None:
                for a in parts:
                    key_c0 = (diag * KV_TILE + c * KEY_SUB + a * SCORE_SUB) // CHUNK
                    key_c = key_c0 + lax.broadcasted_iota(jnp.int32, (SCORE_SUB, Q_TILE), 0) // CHUNK
                    s[a] = jnp.where(key_c <= qry_c, s[a], -jnp.inf)
                first = diag == 0 and c == 0
            s_max = functools.reduce(jnp.maximum, [jnp.max(v, axis=0, keepdims=True) for v in s])
            m_new = s_max if first else jnp.maximum(m_ref[h], s_max)
            p = jnp.concatenate([jnp.exp2(v - m_new).astype(BF16) for v in s], axis=0)
            pv = _dot(vT_ref[j, h * V_ROWS:(h + 1) * V_ROWS, c * KEY_SUB:(c + 1) * KEY_SUB], p)
            alpha = None if first else jnp.exp2(m_ref[h] - m_new)
            m_ref[h] = m_new
            pending.append((h, alpha, pv))
            if len(pending) > PV_BEHIND:
                fold(*pending.pop(0))
        for item in pending:
            fold(*item)

    n_diag = Q_TILE // KV_TILE
    for d in range(n_diag):
        sweep(i * n_diag + d, d)

    def body(j, carry):
        sweep(j, None)
        return carry

    lax.fori_loop(0, i * n_diag, body, 0)
    oT = jnp.concatenate([acc_ref[h, :V_DIM, :] / acc_ref[h, V_DIM:V_DIM + 1, :] for h in range(MLA_HEADS)], axis=0)
    o_ref[...] = oT.T.astype(BF16)


def _mla_attention(qT, k, vT):
    B, S, W = k.shape
    return pl.pallas_call(
        _mla_kernel,
        grid=(B, S // Q_TILE),
        in_specs=[
            pl.BlockSpec((None, None, W, Q_TILE), lambda b, i: (b, i, 0, 0)),
            pl.BlockSpec((None, S, W), lambda b, i: (b, 0, 0)),
            pl.BlockSpec((None, S // KV_TILE, MLA_HEADS * V_ROWS, KV_TILE), lambda b, i: (b, 0, 0, 0)),
        ],
        out_specs=pl.BlockSpec((None, Q_TILE, MLA_HEADS * V_DIM), lambda b, i: (b, i, 0)),
        out_shape=jax.ShapeDtypeStruct((B, S, MLA_HEADS * V_DIM), BF16),
        scratch_shapes=[pltpu.VMEM((MLA_HEADS, 1, Q_TILE), F32),
                        pltpu.VMEM((MLA_HEADS, V_ROWS, Q_TILE), F32)],
        compiler_params=pltpu.CompilerParams(
            dimension_semantics=("arbitrary", "arbitrary"), vmem_limit_bytes=VMEM_LIMIT),
        name="mla_attn",
    )(qT, k, vT)


def _route(logits_t, bias_t, tri_upper, tri_lower):
    tm = logits_t.shape[1]
    neg = -jnp.inf
    rg = lax.broadcasted_iota(jnp.int32, (8, tm), 0)
    re = lax.broadcasted_iota(jnp.int32, (N_EXPERTS, tm), 0)
    top = lambda v: jnp.max(v, axis=0, keepdims=True)

    lg = jnp.where(rg < N_GROUPS, logits_t[LANES:LANES + 8, :] + bias_t[LANES:LANES + 8, :], neg)
    ge = jnp.exp(lg - top(lg))
    gp = ge / jnp.sum(ge, axis=0, keepdims=True)
    g_w = top(gp)
    g_idx = jnp.min(jnp.where(gp == g_w, rg, 8), axis=0, keepdims=True)

    sel = re // EXPERTS_PER_GROUP == g_idx
    le = jnp.where(sel, logits_t[:N_EXPERTS, :] + bias_t[:N_EXPERTS, :], neg)
    ee = jnp.exp(le - top(le))
    ep = jnp.where(sel, ee / jnp.sum(ee, axis=0, keepdims=True), -1.0)
    w1 = top(ep)
    i1 = jnp.min(jnp.where(ep == w1, re, N_EXPERTS), axis=0, keepdims=True)
    ep2 = jnp.where(re == i1, -1.0, ep)
    w2 = top(ep2)
    i2 = jnp.min(jnp.where(ep2 == w2, re, N_EXPERTS), axis=0, keepdims=True)
    den = w1 + w2
    c1 = g_w * (w1 / den)
    c2 = g_w * (w2 / den)

    oh1 = (re == i1).astype(F32)
    oh2 = (re == i2).astype(F32)
    both = oh1 + oh2
    earlier = _dot(both.astype(BF16), tri_upper)
    pieces = jnp.floor((jnp.sum(both, axis=1, keepdims=True) + (PIECE - 1)) * (1.0 / PIECE))
    start = _dot(tri_lower, jnp.broadcast_to(pieces, (N_EXPERTS, LANES)).astype(BF16))[:, 0:1] * PIECE
    pos1 = jnp.sum(oh1 * (earlier + start), axis=0, keepdims=True)
    pos2 = jnp.sum(oh2 * (earlier + start), axis=0, keepdims=True)
    info_t = jnp.concatenate([pos1, pos2, c1, c2, jnp.zeros((4, tm), F32)], axis=0)
    return info_t, pieces


def _mix_kernel(x_ref, attn_ref, ga_ref, gyb_ref, kv_ref, wab_ref, wmix_ref, xg_ref, wxq_ref, wxo_ref,
                fg_ref, wr_hi_ref, wr_lo_ref, rb_ref, tri_ref, lower_ref,
                h_out, xn_out, info_out, infoT_out, pieces_out, logits_ref):
    tm = x_ref.shape[0]

    @pl.when(pl.program_id(0) == 0)
    def _():
        logits_ref[...] = jnp.zeros_like(logits_ref)

    info_t, pieces = _route(logits_ref[...].T, rb_ref[...], tri_ref[...], lower_ref[...])
    infoT_out[...] = info_t
    info_out[...] = jnp.concatenate([info_t, jnp.zeros((LANES - 8, tm), F32)], axis=0).T
    pieces_out[...] = jnp.broadcast_to(pieces, (N_EXPERTS, LANES))

    blocks = [slice(b * MIX_SUB, (b + 1) * MIX_SUB) for b in range(tm // MIX_SUB)]
    y_a = [_dot(attn_ref[r, :], wab_ref[...]) for r in blocks]
    merged = [(ga_ref[r, :].astype(F32) * y + gyb_ref[r, :].astype(F32)).astype(BF16) for r, y in zip(blocks, y_a)]
    h1 = [x_ref[r, :] + _dot(m, wmix_ref[...]) for r, m in zip(blocks, merged)]

    hn = [_rms(h, xg_ref[...]).astype(BF16) for h in h1]
    q = [_dot(v, wxq_ref[...]).astype(BF16) for v in hn]
    pairs = [(b, h) for h in range(MEM_HEADS) for b in range(len(blocks))]
    hs = lambda h: slice(h * MEM_HEAD_DIM, (h + 1) * MEM_HEAD_DIM)
    vs = lambda h: slice(D_MODEL + h * MEM_HEAD_DIM, D_MODEL + (h + 1) * MEM_HEAD_DIM)
    s = [_dot_nt(q[b][:, hs(h)], kv_ref[:, hs(h)]) for b, h in pairs]
    p = [jnp.exp(v - jnp.max(v, axis=-1, keepdims=True)) for v in s]
    o = [_dot(v.astype(BF16), kv_ref[:, vs(h)]) for v, (b, h) in zip(p, pairs)]
    heads = [[None] * MEM_HEADS for _ in blocks]
    for (b, h), ov, pv in zip(pairs, o, p):
        heads[b][h] = (ov / jnp.sum(pv, axis=-1, keepdims=True)).astype(BF16)
    h2 = [h + _dot(jnp.concatenate(hd, axis=1), wxo_ref[...]) for h, hd in zip(h1, heads)]

    xn = [_rms(h, fg_ref[...]) for h in h2]
    xn_hi = [v.astype(BF16) for v in xn]
    xn_lo = [(v - hi.astype(F32)).astype(BF16) for v, hi in zip(xn, xn_hi)]
    logits = [_dot(hi, wr_hi_ref[...]) + (_dot(hi, wr_lo_ref[...]) + _dot(lo, wr_hi_ref[...]))
              for hi, lo in zip(xn_hi, xn_lo)]
    for r, h, hi in zip(blocks, h2, xn_hi):
        h_out[r, :] = h
        xn_out[r, :] = hi
    for r, v in zip(blocks, logits):
        logits_ref[r, :] = v


def _mix_xattn(x, attn, ga, gyb, memkv, wab, wmix, xg, wxq, wxo, fg, wr_hi, wr_lo, rb):
    B, S, _ = x.shape
    M = memkv.shape[1]
    tm = MOE_CHUNK
    nt = S // tm
    n_tiles = B * nt
    cur = lambda t: jnp.minimum(t, n_tiles - 1)
    prev = lambda t: jnp.maximum(t - 1, 0)
    row = lambda t: (cur(t) // nt, cur(t) % nt, 0)
    const2 = lambda t: (0, 0)
    tri = (lax.broadcasted_iota(jnp.int32, (tm, tm), 0) < lax.broadcasted_iota(jnp.int32, (tm, tm), 1)).astype(BF16)
    lower = (lax.broadcasted_iota(jnp.int32, (N_EXPERTS, N_EXPERTS), 1)
             < lax.broadcasted_iota(jnp.int32, (N_EXPERTS, N_EXPERTS), 0)).astype(BF16)
    rb = jnp.broadcast_to(rb.reshape(2 * LANES, 1), (2 * LANES, tm))
    return pl.pallas_call(
        _mix_kernel,
        grid=(n_tiles + 1,),
        in_specs=[
            pl.BlockSpec((None, tm, D_MODEL), row),
            pl.BlockSpec((None, tm, MLA_HEADS * V_DIM), row),
            pl.BlockSpec((None, tm, D_MODEL), row),
            pl.BlockSpec((None, tm, D_MODEL), row),
            pl.BlockSpec((None, M, 2 * D_MODEL), lambda t: (cur(t) // nt, 0, 0)),
            pl.BlockSpec(wab.shape, const2),
            pl.BlockSpec(wmix.shape, const2),
            pl.BlockSpec((1, D_MODEL), const2),
            pl.BlockSpec(wxq.shape, const2),
            pl.BlockSpec(wxo.shape, const2),
            pl.BlockSpec((1, D_MODEL), const2),
            pl.BlockSpec(wr_hi.shape, const2),
            pl.BlockSpec(wr_lo.shape, const2),
            pl.BlockSpec((2 * LANES, tm), const2),
            pl.BlockSpec((tm, tm), const2),
            pl.BlockSpec((N_EXPERTS, N_EXPERTS), const2),
        ],
        out_specs=[
            pl.BlockSpec((None, tm, D_MODEL), row),
            pl.BlockSpec((None, tm, D_MODEL), row),
            pl.BlockSpec((None, tm, LANES), lambda t: (prev(t) // nt, prev(t) % nt, 0)),
            pl.BlockSpec((8, tm), lambda t: (0, prev(t))),
            pl.BlockSpec((None, N_EXPERTS, LANES), lambda t: (prev(t), 0, 0)),
        ],
        out_shape=[jax.ShapeDtypeStruct((B, S, D_MODEL), F32),
                   jax.ShapeDtypeStruct((B, S, D_MODEL), BF16),
                   jax.ShapeDtypeStruct((B, S, LANES), F32),
                   jax.ShapeDtypeStruct((8, B * S), F32),
                   jax.ShapeDtypeStruct((B * nt, N_EXPERTS, LANES), F32)],
        scratch_shapes=[pltpu.VMEM((tm, 2 * LANES), F32)],
        compiler_params=pltpu.CompilerParams(
            dimension_semantics=("arbitrary",), vmem_limit_bytes=VMEM_LIMIT),
        name="mix_xattn",
    )(x, attn, ga, gyb, memkv, wab, wmix, xg, wxq, wxo, fg, wr_hi, wr_lo, rb, tri, lower)


def _piece_copy(src_ref, dst_ref, sem):
    return pltpu.make_async_copy(src_ref, dst_ref, sem)


def _dispatch_kernel(dst_ref, np_ref, gap_ref, fill_ref, xn_ref, infoT_ref, xs_hbm, buf_ref, zero_ref, sem_ref):
    c = pl.program_id(0)
    n = pl.num_programs(0)
    slot = c % 2

    tile_pieces = EXPERT_TILE // PIECE
    n_tiles = xs_hbm.shape[0] // tile_pieces

    def gap_copy(g):
        return _piece_copy(zero_ref.at[0], xs_hbm.at[gap_ref[g]], sem_ref.at[2])

    def tail_copy(t):
        return _piece_copy(zero_ref, xs_hbm.at[pl.ds(t * tile_pieces, tile_pieces)], sem_ref.at[2])

    @pl.when(c == 0)
    def _():
        zero_ref[...] = jnp.zeros_like(zero_ref)
        lax.fori_loop(0, fill_ref[0], lambda g, carry: (gap_copy(g).start(), carry)[1], 0)
        lax.fori_loop(fill_ref[1], n_tiles, lambda t, carry: (tail_copy(t).start(), carry)[1], 0)

    def copy(cc, s, q):
        return _piece_copy(buf_ref.at[s, q], xs_hbm.at[dst_ref[cc * MAX_PIECES + q]], sem_ref.at[s])

    def start_all(cc, s):
        lax.fori_loop(0, np_ref[cc], lambda q, carry: (copy(cc, s, q).start(), carry)[1], 0)

    def wait_all(cc, s):
        n_pieces = np_ref[cc]

        @pl.when(n_pieces > 0)
        def _():
            _piece_copy(buf_ref.at[s, pl.ds(0, n_pieces)], xs_hbm.at[pl.ds(0, n_pieces)], sem_ref.at[s]).wait()

    @pl.when(c >= 2)
    def _():
        wait_all(c - 2, slot)

    pos1 = infoT_ref[0:1, :]
    pos2 = infoT_ref[1:2, :]
    r = lax.broadcasted_iota(jnp.int32, (CHUNK_ROWS, MOE_CHUNK), 0).astype(F32)
    onehot = jnp.where((r == pos1) | (r == pos2), 1.0, 0.0).astype(BF16)
    buf_ref[slot] = _dot(onehot, xn_ref[...]).astype(BF16).reshape(MAX_PIECES, PIECE, D_MODEL)
    start_all(c, slot)

    @pl.when(c == n - 1)
    def _():
        @pl.when(c >= 1)
        def _():
            wait_all(c - 1, 1 - slot)
        wait_all(c, slot)
        lax.fori_loop(0, fill_ref[0], lambda g, carry: (gap_copy(g).wait(), carry)[1], 0)
        lax.fori_loop(fill_ref[1], n_tiles, lambda t, carry: (tail_copy(t).wait(), carry)[1], 0)


def _dispatch(xn, infoT, dst, npc, gaps, fill, rows_max):
    T = xn.shape[0]
    grid_spec = pltpu.PrefetchScalarGridSpec(
        num_scalar_prefetch=4,
        grid=(T // MOE_CHUNK,),
        in_specs=[
            pl.BlockSpec((MOE_CHUNK, D_MODEL), lambda c, *_: (c, 0)),
            pl.BlockSpec((8, MOE_CHUNK), lambda c, *_: (0, c)),
        ],
        out_specs=pl.BlockSpec(memory_space=pl.ANY),
        scratch_shapes=[pltpu.VMEM((2, MAX_PIECES, PIECE, D_MODEL), BF16),
                        pltpu.VMEM((EXPERT_TILE // PIECE, PIECE, D_MODEL), BF16),
                        pltpu.SemaphoreType.DMA((3,))],
    )
    xs = pl.pallas_call(
        _dispatch_kernel,
        grid_spec=grid_spec,
        out_shape=jax.ShapeDtypeStruct((rows_max // PIECE, PIECE, D_MODEL), BF16),
        compiler_params=pltpu.CompilerParams(
            dimension_semantics=("arbitrary",), vmem_limit_bytes=VMEM_LIMIT),
        name="moe_dispatch",
    )(dst, npc, gaps, fill, xn, infoT)
    return xs.reshape(rows_max, D_MODEL)


def _expert_kernel(te_ref, tv_ref, nu_ref, x_ref, wg_ref, wu_ref, wd_ref, y_ref, wgu_bf, wd_bf):
    i = pl.program_id(0)

    @pl.when(i < nu_ref[0])
    def _():
        @pl.when((i == 0) | (te_ref[i] != te_ref[jnp.maximum(i - 1, 0)]))
        def _():
            wgu_bf[:, :EXPERT_FF] = wg_ref[...].astype(BF16)
            wgu_bf[:, EXPERT_FF:] = wu_ref[...].astype(BF16)
            wd_bf[...] = wd_ref[...].astype(BF16)

        blocks = [slice(b * EXPERT_SUB, (b + 1) * EXPERT_SUB) for b in range(EXPERT_TILE // EXPERT_SUB)]
        row = lax.broadcasted_iota(jnp.int32, (EXPERT_SUB, D_MODEL), 0)
        gus = []
        for b, rows in enumerate(blocks):
            x = x_ref[rows, :]
            x = jnp.where(row < tv_ref[i] - b * EXPERT_SUB, x, jnp.zeros_like(x))
            gus.append(_dot(x, wgu_bf[...]))
        hids = []
        for gu in gus:
            gate = gu[:, :EXPERT_FF]
            hids.append((gate * jax.nn.sigmoid(gate) * gu[:, EXPERT_FF:]).astype(BF16))
        for rows, hid in zip(blocks, hids):
            y_ref[rows, :] = _dot(hid, wd_bf[...]).astype(BF16)

    @pl.when(i >= nu_ref[0])
    def _():
        y_ref[...] = jnp.zeros_like(y_ref)


def _experts(xs, w_gate, w_up, w_down, tile_expert, tile_valid, n_used):
    rows_max = xs.shape[0]
    last = lambda i, nu: jnp.minimum(i, nu[0] - 1)
    expert = lambda i, te, tv, nu: (te[last(i, nu)], 0, 0)
    grid_spec = pltpu.PrefetchScalarGridSpec(
        num_scalar_prefetch=3,
        grid=(rows_max // EXPERT_TILE,),
        in_specs=[
            pl.BlockSpec((EXPERT_TILE, D_MODEL), lambda i, te, tv, nu: (last(i, nu), 0)),
            pl.BlockSpec((None, D_MODEL, EXPERT_FF), expert),
            pl.BlockSpec((None, D_MODEL, EXPERT_FF), expert),
            pl.BlockSpec((None, EXPERT_FF, D_MODEL), expert),
        ],
        out_specs=pl.BlockSpec((EXPERT_TILE, D_MODEL), lambda i, te, tv, nu: (i, 0)),
        scratch_shapes=[pltpu.VMEM((D_MODEL, 2 * EXPERT_FF), BF16), pltpu.VMEM((EXPERT_FF, D_MODEL), BF16)],
    )
    return pl.pallas_call(
        _expert_kernel,
        grid_spec=grid_spec,
        out_shape=jax.ShapeDtypeStruct((rows_max, D_MODEL), BF16),
        compiler_params=pltpu.CompilerParams(
            dimension_semantics=("arbitrary",), vmem_limit_bytes=VMEM_LIMIT),
        name="moe_experts",
    )(tile_expert, tile_valid, n_used, xs, w_gate, w_up, w_down)


def _combine_kernel(dst_ref, np_ref, h_ref, info_ref, fg_ref, ys_hbm, o_ref, buf_ref, sem_ref):
    c = pl.program_id(0)
    n = pl.num_programs(0)
    slot = c % 2

    def copy(cc, s, q):
        return _piece_copy(ys_hbm.at[dst_ref[cc * MAX_PIECES + q]], buf_ref.at[s, q], sem_ref.at[s])

    def start_all(cc, s):
        lax.fori_loop(0, np_ref[cc], lambda q, carry: (copy(cc, s, q).start(), carry)[1], 0)

    def wait_all(cc, s):
        n_pieces = np_ref[cc]

        @pl.when(n_pieces > 0)
        def _():
            _piece_copy(ys_hbm.at[pl.ds(0, n_pieces)], buf_ref.at[s, pl.ds(0, n_pieces)], sem_ref.at[s]).wait()

    @pl.when(c == 0)
    def _():
        buf_ref[...] = jnp.zeros_like(buf_ref)
        start_all(0, 0)

    @pl.when(c + 1 < n)
    def _():
        start_all(c + 1, 1 - slot)

    wait_all(c, slot)
    info = info_ref[...]
    r = lax.broadcasted_iota(jnp.int32, (MOE_CHUNK, CHUNK_ROWS), 1).astype(F32)
    weights = jnp.where(r == info[:, 0:1], info[:, 2:3], 0.0) + jnp.where(r == info[:, 1:2], info[:, 3:4], 0.0)
    moe = _dot(weights.astype(BF16), buf_ref[slot].reshape(CHUNK_ROWS, D_MODEL))
    o_ref[...] = _rms(h_ref[...] + moe, fg_ref[...])


def _combine(h2, info, fg, ys, dst, npc):
    T = h2.shape[0]
    grid_spec = pltpu.PrefetchScalarGridSpec(
        num_scalar_prefetch=2,
        grid=(T // MOE_CHUNK,),
        in_specs=[
            pl.BlockSpec((MOE_CHUNK, D_MODEL), lambda c, dst, npc: (c, 0)),
            pl.BlockSpec((MOE_CHUNK, LANES), lambda c, dst, npc: (c, 0)),
            pl.BlockSpec((1, D_MODEL), lambda c, dst, npc: (0, 0)),
            pl.BlockSpec(memory_space=pl.ANY),
        ],
        out_specs=pl.BlockSpec((MOE_CHUNK, D_MODEL), lambda c, dst, npc: (c, 0)),
        scratch_shapes=[pltpu.VMEM((2, MAX_PIECES, PIECE, D_MODEL), BF16), pltpu.SemaphoreType.DMA((2,))],
    )
    ys = ys.reshape(-1, PIECE, D_MODEL)
    return pl.pallas_call(
        _combine_kernel,
        grid_spec=grid_spec,
        out_shape=jax.ShapeDtypeStruct((T, D_MODEL), F32),
        compiler_params=pltpu.CompilerParams(
            dimension_semantics=("arbitrary",), vmem_limit_bytes=VMEM_LIMIT),
        name="moe_combine",
    )(dst, npc, h2, info, fg, ys)


def _routing_tables(pieces, rows_max):
    tile_pieces = EXPERT_TILE // PIECE
    total = jnp.sum(pieces, axis=0)
    total_al = (total + tile_pieces - 1) // tile_pieces * tile_pieces
    seg_end = jnp.cumsum(total_al)
    seg_start = seg_end - total_al
    chunk_off = jnp.cumsum(pieces, axis=0) - pieces
    loc_end = jnp.cumsum(pieces, axis=1)
    loc_start = loc_end - pieces
    q = jnp.arange(MAX_PIECES, dtype=jnp.int32)
    owner = jnp.sum((q[None, :, None] >= loc_end[:, None, :]).astype(jnp.int32), axis=-1)
    owner = jnp.minimum(owner, N_EXPERTS - 1)
    experts = jnp.arange(N_EXPERTS, dtype=jnp.int32)
    is_owner = (owner[:, :, None] == experts).astype(jnp.int32)
    offset = seg_start[None, :] + chunk_off - loc_start
    dst = jnp.sum(is_owner * offset[:, None, :], axis=-1) + q[None, :]
    n_local = loc_end[:, -1]
    dst = jnp.where(q[None, :] < n_local[:, None], dst, 0)

    t0 = jnp.arange(rows_max // EXPERT_TILE, dtype=jnp.int32) * tile_pieces
    tile_expert = jnp.minimum(jnp.sum((t0[:, None] >= seg_end[None, :]).astype(jnp.int32), axis=-1), N_EXPERTS - 1)
    copies_end = jnp.sum((tile_expert[:, None] == experts).astype(jnp.int32) * (seg_start + total)[None, :], axis=-1)
    tile_valid = jnp.clip((copies_end - t0) * PIECE, 0, EXPERT_TILE)
    n_used = (seg_end[-1] // tile_pieces).reshape(1)

    k = jnp.arange(tile_pieces, dtype=jnp.int32)
    is_gap = (k[None, :] < (total_al - total)[:, None]).reshape(-1)
    gap_piece = (seg_start + total)[:, None] + k[None, :]
    order = jnp.argsort(jnp.logical_not(is_gap), stable=True)
    gaps = gap_piece.reshape(-1)[order]
    fill = jnp.stack([jnp.sum(is_gap.astype(jnp.int32)), n_used[0]])
    i32 = lambda a: a.astype(jnp.int32)
    return i32(dst.reshape(-1)), i32(n_local), i32(tile_expert), i32(tile_valid), i32(n_used), i32(gaps), i32(fill)


def _rope_tables(positions):
    inv_freq = 1.0 / (ROPE_THETA ** (jnp.arange(0, ROPE_DIM, 2, dtype=F32) / ROPE_DIM))
    ang = positions.astype(F32)[:, None, :] * inv_freq[None, :, None]
    cos, sin = jnp.cos(ang), jnp.sin(ang)
    return jnp.concatenate([cos, sin], axis=1)


def _pad_heads(w, heads, width):
    k = w.shape[0]
    w = w.reshape(k, heads, width)
    w = jnp.pad(w, ((0, 0), (0, 0), (0, HEAD_PAD - width)))
    return w.reshape(k, heads * HEAD_PAD)


def _layer(l, h, mem, tables, mix_norm_g, w_in, q_norm_g, w_q_up, kv_norm_g, w_kv_up, w_attn_branch,
           pool_w, pool_scale, w_pool_branch, w_mix_out, xattn_norm_g, mem_norm_g, w_xq, w_xkv, w_xo,
           ffn_norm_g, w_router_group, b_router_group, w_router_expert, b_router_expert,
           w_exp_gate, w_exp_up, w_exp_down, out_g, tm_proj):
    B, S, _ = h.shape
    row2 = lambda v: v.reshape(1, -1).astype(F32)

    wi = w_in[l]
    kr_cols = jnp.pad(wi[:, Q_LORA + KV_LORA:Q_LORA + KV_LORA + ROPE_DIM],
                      ((0, 0), (NOPE_DIM, LANES - NOPE_DIM - ROPE_DIM)))
    win = jnp.concatenate([wi[:, :Q_LORA + KV_LORA], kr_cols, wi[:, Q_LORA + KV_LORA + ROPE_DIM:]], axis=1).astype(BF16)
    scale = math.log2(math.e) / math.sqrt(NOPE_DIM + ROPE_DIM)
    wq3 = (w_q_up[l] * scale).reshape(Q_LORA, MLA_HEADS, NOPE_DIM + ROPE_DIM)
    half = ROPE_DIM // 2
    wq = jnp.concatenate([wq3[:, :, :NOPE_DIM].reshape(Q_LORA, -1),
                          wq3[:, :, NOPE_DIM:NOPE_DIM + half].reshape(Q_LORA, -1),
                          wq3[:, :, NOPE_DIM + half:].reshape(Q_LORA, -1)], axis=1).astype(BF16)
    wkv3 = w_kv_up[l].reshape(KV_LORA, MLA_HEADS, NOPE_DIM + V_DIM)
    wkv = jnp.concatenate([
        _pad_heads(wkv3[:, :, :NOPE_DIM].reshape(KV_LORA, -1), MLA_HEADS, NOPE_DIM),
        wkv3[:, :, NOPE_DIM:].reshape(KV_LORA, -1)], axis=1).astype(BF16)

    memkv = _mem_kv(mem.reshape(-1, D_MODEL), row2(mem_norm_g[l]), w_xkv[l].astype(BF16))
    memkv = memkv.reshape(B, -1, 2 * D_MODEL)

    qT, k, vT, ga, gyb = _in_proj(
        h, tables, row2(mix_norm_g[l]), win, row2(q_norm_g[l]), wq, row2(kv_norm_g[l]), wkv,
        pool_w[l].astype(BF16), row2(pool_scale[l]), w_pool_branch[l].astype(BF16), tm_proj)
    attn = _mla_attention(qT, k, vT)

    w_r = jnp.zeros((D_MODEL, 2 * LANES), F32)
    w_r = w_r.at[:, :N_EXPERTS].set(w_router_expert[l]).at[:, LANES:LANES + N_GROUPS].set(w_router_group[l])
    wr_hi = w_r.astype(BF16)
    wr_lo = (w_r - wr_hi.astype(F32)).astype(BF16)
    rb = jnp.zeros((1, 2 * LANES), F32)
    rb = rb.at[0, :N_EXPERTS].set(b_router_expert[l]).at[0, LANES:LANES + N_GROUPS].set(b_router_group[l])

    h2, xn, info, infoT, pieces = _mix_xattn(
        h, attn, ga, gyb, memkv, w_attn_branch[l].astype(BF16), w_mix_out[l].astype(BF16),
        row2(xattn_norm_g[l]), (w_xq[l] * (1.0 / math.sqrt(MEM_HEAD_DIM))).astype(BF16), w_xo[l].astype(BF16),
        row2(ffn_norm_g[l]), wr_hi, wr_lo, rb)

    T = B * S
    n_chunks = T // MOE_CHUNK
    tile_pieces = EXPERT_TILE // PIECE
    max_pieces = 2 * T // PIECE + n_chunks * N_EXPERTS + N_EXPERTS * tile_pieces
    rows_max = -(-max_pieces // tile_pieces) * EXPERT_TILE
    dst, n_local, tile_expert, tile_valid, n_used, gaps, fill = _routing_tables(
        pieces[:, :, 0].astype(jnp.int32), rows_max)

    xs = _dispatch(xn.reshape(T, D_MODEL), infoT, dst, n_local, gaps, fill, rows_max)
    ys = _experts(xs, w_exp_gate[l], w_exp_up[l], w_exp_down[l], tile_expert, tile_valid, n_used)
    out = _combine(h2.reshape(T, D_MODEL), info.reshape(T, LANES), row2(out_g), ys, dst, n_local)
    return out.reshape(B, S, D_MODEL)


def kernel(x, mem, positions, mix_norm_g, w_in, q_norm_g, w_q_up, kv_norm_g, w_kv_up, w_attn_branch, pool_w, pool_scale, w_pool_branch, w_mix_out, xattn_norm_g, mem_norm_g, w_xq, w_xkv, w_xo, ffn_norm_g, w_router_group, b_router_group, w_router_expert, b_router_expert, w_exp_gate, w_exp_up, w_exp_down, final_norm_g):
    depth = w_in.shape[0]
    assert depth == 1, "the combine kernel fuses the final RMSNorm, which is only valid after the last layer"
    assert x.shape[1] % Q_TILE == 0 and x.shape[1] % MOE_CHUNK == 0
    tables = _rope_tables(positions)
    return _layer(0, x, mem, tables, mix_norm_g, w_in, q_norm_g, w_q_up, kv_norm_g, w_kv_up, w_attn_branch,
                  pool_w, pool_scale, w_pool_branch, w_mix_out, xattn_norm_g, mem_norm_g, w_xq, w_xkv, w_xo,
                  ffn_norm_g, w_router_group, b_router_group, w_router_expert, b_router_expert,
                  w_exp_gate, w_exp_up, w_exp_down, final_norm_g, IN_PROJ_TILE)
```

```python
import functools
import math

import jax
import jax.numpy as jnp
from jax import lax
from jax.experimental import pallas as pl
from jax.experimental.pallas import tpu as pltpu

F32 = jnp.float32
BF16 = jnp.bfloat16

D_MODEL = 1024
CHUNK = 64
MLA_HEADS = 8
Q_LORA = 384
KV_LORA = 256
NOPE_DIM = 64
ROPE_DIM = 32
V_DIM = 64
ROPE_THETA = 10000.0
POOL_WIDTH = 512
POOL_WINDOWS = (2, 4, 8, 16)
POOL_GROUP_DIM = POOL_WIDTH // len(POOL_WINDOWS)
POOL_HALO = 16
MEM_HEADS = 4
MEM_HEAD_DIM = D_MODEL // MEM_HEADS
N_GROUPS = 4
EXPERTS_PER_GROUP = 8
N_EXPERTS = N_GROUPS * EXPERTS_PER_GROUP
EXPERT_FF = 256
EPS = 1e-6

LANES = 128
HEAD_PAD = LANES
Q_TILE = 512
KV_TILE = 512
KEY_SUB = 128
SCORE_SUB = 128
V_ROWS = V_DIM + 16
IN_PROJ_TILE = 512
MOE_CHUNK = 512
PIECE = 16
MAX_PIECES = 2 * MOE_CHUNK // PIECE + N_EXPERTS
SEG_STRIDE = 1 + 3 * N_EXPERTS
CHUNK_ROWS = MAX_PIECES * PIECE
EXPERT_TILE = 1024
EXPERT_SUB = 256
PROJ_SUB = 256
MIX_SUB = 256
QK_AHEAD = 3
PV_BEHIND = 2

_C_Q = 0
_C_KV = _C_Q + Q_LORA
_C_KR = _C_KV + KV_LORA
_C_POOL = _C_KR + LANES
_C_GA = _C_POOL + POOL_WIDTH
_C_GB = _C_GA + D_MODEL
_C_END = _C_GB + D_MODEL

VMEM_LIMIT = 56 * 1024 * 1024


def _rms(x, g):
    return x * lax.rsqrt(jnp.mean(x * x, axis=-1, keepdims=True) + EPS) * g


def _dot(a, b):
    return jnp.dot(a, b, preferred_element_type=F32)


def _dot_nt(a, b):
    return lax.dot_general(a, b, (((1,), (1,)), ((), ())), preferred_element_type=F32)


def _mem_kv_kernel(mem_ref, g_ref, w_ref, kv_ref):
    mn = _rms(mem_ref[...], g_ref[...]).astype(BF16)
    kv_ref[...] = _dot(mn, w_ref[...]).astype(BF16)


def _mem_kv(mem2d, g, w_xkv):
    rows = mem2d.shape[0]
    tm = min(512, rows)
    assert rows % tm == 0
    return pl.pallas_call(
        _mem_kv_kernel,
        grid=(rows // tm,),
        in_specs=[
            pl.BlockSpec((tm, D_MODEL), lambda i: (i, 0)),
            pl.BlockSpec((1, D_MODEL), lambda i: (0, 0)),
            pl.BlockSpec((D_MODEL, 2 * D_MODEL), lambda i: (0, 0)),
        ],
        out_specs=pl.BlockSpec((tm, 2 * D_MODEL), lambda i: (i, 0)),
        out_shape=jax.ShapeDtypeStruct((rows, 2 * D_MODEL), BF16),
        compiler_params=pltpu.CompilerParams(vmem_limit_bytes=VMEM_LIMIT),
        name="mem_kv",
    )(mem2d, g, w_xkv)


def _rope(t, c, sa, sb):
    w = t.shape[-1]
    return t * c + pltpu.roll(t, ROPE_DIM // 2, 1) * sa + pltpu.roll(t, w - ROPE_DIM // 2, 1) * sb


def _in_proj_kernel(x_ref, rot_ref, g_ref, win_ref, qg_ref, wq_ref, kvg_ref, wkv_ref,
                    poolw_ref, pscale_ref, wpb_ref,
                    qT_out, k_out, vT_out, ga_out, gyb_out, hist_ref):
    tm = x_ref.shape[0]
    i = pl.program_id(1)
    blocks = [slice(b * PROJ_SUB, (b + 1) * PROJ_SUB) for b in range(tm // PROJ_SUB)]
    rows = lambda parts: jnp.concatenate(parts, axis=0)
    hn = [_rms(x_ref[r, :], g_ref[...]).astype(BF16) for r in blocks]
    proj = lambda lo, hi: [_dot(v, win_ref[:, lo:hi]) for v in hn]

    half = ROPE_DIM // 2
    cos8 = jnp.concatenate([rot_ref[0:half, :]] * MLA_HEADS, axis=0).T
    sin8 = jnp.concatenate([rot_ref[half:, :]] * MLA_HEADS, axis=0).T
    lane = lax.broadcasted_iota(jnp.int32, cos8.shape, 1)
    rope_lo = NOPE_DIM
    c1 = jnp.where(lane < rope_lo + ROPE_DIM, cos8, 0.0)
    sa1 = jnp.where((lane >= rope_lo + half) & (lane < rope_lo + ROPE_DIM), sin8, 0.0)
    sb1 = jnp.where((lane >= rope_lo) & (lane < rope_lo + half), -sin8, 0.0)

    q_lat = proj(_C_Q, _C_KV)
    kv_lat = proj(_C_KV, _C_KR)
    qn = [_rms(v, qg_ref[...]).astype(BF16) for v in q_lat]
    kvn = [_rms(v, kvg_ref[...]).astype(BF16) for v in kv_lat]
    q = rows([_dot(v, wq_ref[...]) for v in qn])
    n_nope = MLA_HEADS * NOPE_DIM
    x1 = q[:, n_nope:n_nope + LANES]
    x2 = q[:, n_nope + LANES:]
    qT = jnp.concatenate([q[:, :n_nope], x1 * cos8 - x2 * sin8, x2 * cos8 + x1 * sin8], axis=1).T
    pad = jnp.zeros((HEAD_PAD - NOPE_DIM - ROPE_DIM, tm), F32)
    qT = jnp.concatenate(
        [blk for h in range(MLA_HEADS) for blk in (
            qT[h * NOPE_DIM:(h + 1) * NOPE_DIM, :],
            qT[n_nope + h * half:n_nope + (h + 1) * half, :],
            qT[n_nope + LANES + h * half:n_nope + LANES + (h + 1) * half, :], pad)], axis=0).astype(BF16)

    k_nope = rows([_dot(v, wkv_ref[:, 0:MLA_HEADS * HEAD_PAD]) for v in kvn])
    kr = _rope(rows(proj(_C_KR, _C_POOL)), c1, sa1, sb1)
    k_out[...] = (k_nope + jnp.tile(kr, (1, MLA_HEADS))).astype(BF16)
    v = rows([_dot(t, wkv_ref[:, MLA_HEADS * HEAD_PAD:]) for t in kvn])
    for t in range(tm // Q_TILE):
        qT_out[t] = qT[:, t * Q_TILE:(t + 1) * Q_TILE]
    vT = v.T
    ones = jnp.ones((V_ROWS - V_DIM, tm), F32)
    vT = jnp.concatenate(
        [blk for h in range(MLA_HEADS) for blk in (vT[h * V_DIM:(h + 1) * V_DIM, :], ones)], axis=0).astype(BF16)
    for t in range(tm // KV_TILE):
        vT_out[t] = vT[:, t * KV_TILE:(t + 1) * KV_TILE]

    u = rows(proj(_C_POOL, _C_GA))

    @pl.when(i == 0)
    def _():
        hist_ref[...] = jnp.zeros_like(hist_ref)

    ext = jnp.concatenate([hist_ref[...], u], axis=0)
    hist_ref[...] = u[tm - POOL_HALO:, :]
    t_idx = i * tm + lax.broadcasted_iota(jnp.int32, (tm, POOL_GROUP_DIM), 0)
    ys = []
    for g, w in enumerate(POOL_WINDOWS):
        c0 = g * POOL_GROUP_DIM
        run = ext[:, c0:c0 + POOL_GROUP_DIM]
        span = 1
        while span < w:
            run = run + pltpu.roll(run, span, 0)
            span *= 2
        cnt = jnp.minimum(t_idx + 1, w).astype(F32)
        d = run[POOL_HALO:, :] / cnt - u[:, c0:c0 + POOL_GROUP_DIM]
        ys.append(_dot(d.astype(BF16), poolw_ref[g]))
    y = (jnp.concatenate(ys, axis=1) * pscale_ref[...]).astype(BF16)
    y_b = [_dot(y[r, :], wpb_ref[...]) for r in blocks]

    for r, t in zip(blocks, proj(_C_GA, _C_GB)):
        ga_out[r, :] = jax.nn.sigmoid(t).astype(BF16)
    for r, t, yb in zip(blocks, proj(_C_GB, _C_END), y_b):
        gyb_out[r, :] = (jax.nn.sigmoid(t) * yb).astype(BF16)


def _in_proj(x, rot, g, win, qg, wq, kvg, wkv, poolw, pscale, wpb, tm):
    B, S, _ = x.shape
    row = lambda b, i: (b, i, 0)
    const2 = lambda b, i: (0, 0)
    const3 = lambda b, i: (0, 0, 0)
    slab = lambda b, i: (b, i, 0, 0)
    return pl.pallas_call(
        _in_proj_kernel,
        grid=(B, S // tm),
        in_specs=[
            pl.BlockSpec((None, tm, D_MODEL), row),
            pl.BlockSpec((None, ROPE_DIM, tm), lambda b, i: (b, 0, i)),
            pl.BlockSpec((1, D_MODEL), const2),
            pl.BlockSpec(win.shape, const2),
            pl.BlockSpec((1, Q_LORA), const2),
            pl.BlockSpec(wq.shape, const2),
            pl.BlockSpec((1, KV_LORA), const2),
            pl.BlockSpec(wkv.shape, const2),
            pl.BlockSpec(poolw.shape, const3),
            pl.BlockSpec((1, POOL_WIDTH), const2),
            pl.BlockSpec(wpb.shape, const2),
        ],
        out_specs=[
            pl.BlockSpec((None, tm // Q_TILE, MLA_HEADS * HEAD_PAD, Q_TILE), slab),
            pl.BlockSpec((None, tm, MLA_HEADS * HEAD_PAD), row),
            pl.BlockSpec((None, tm // KV_TILE, MLA_HEADS * V_ROWS, KV_TILE), slab),
            pl.BlockSpec((None, tm, D_MODEL), row),
            pl.BlockSpec((None, tm, D_MODEL), row),
        ],
        out_shape=[jax.ShapeDtypeStruct((B, S // Q_TILE, MLA_HEADS * HEAD_PAD, Q_TILE), BF16),
                   jax.ShapeDtypeStruct((B, S, MLA_HEADS * HEAD_PAD), BF16),
                   jax.ShapeDtypeStruct((B, S // KV_TILE, MLA_HEADS * V_ROWS, KV_TILE), BF16),
                   jax.ShapeDtypeStruct((B, S, D_MODEL), BF16),
                   jax.ShapeDtypeStruct((B, S, D_MODEL), BF16)],
        scratch_shapes=[pltpu.VMEM((POOL_HALO, POOL_WIDTH), F32)],
        compiler_params=pltpu.CompilerParams(
            dimension_semantics=("arbitrary", "arbitrary"), vmem_limit_bytes=VMEM_LIMIT),
        name="in_proj",
    )(x, rot, g, win, qg, wq, kvg, wkv, poolw, pscale, wpb)


def _mla_kernel(qT_ref, k_ref, vT_ref, o_ref, m_ref, acc_ref):
    i = pl.program_id(1)
    n_sub = KV_TILE // KEY_SUB
    units = [(h, c) for c in range(n_sub) for h in range(MLA_HEADS)]
    qry_c = lax.broadcasted_iota(jnp.int32, (SCORE_SUB, Q_TILE), 1) // CHUNK
    parts = range(KEY_SUB // SCORE_SUB)

    def scores(j, h, c):
        hs = slice(h * HEAD_PAD, (h + 1) * HEAD_PAD)
        out = []
        for a in parts:
            rows = pl.ds(pl.multiple_of(j * KV_TILE + c * KEY_SUB + a * SCORE_SUB, SCORE_SUB), SCORE_SUB)
            out.append(_dot(k_ref[rows, hs], qT_ref[hs, :]))
        return out

    def fold(h, alpha, pv):
        acc_ref[h] = pv if alpha is None else alpha * acc_ref[h] + pv

    def sweep(j, diag):
        ahead = [scores(j, *u) for u in units[:QK_AHEAD]]
        pending = []
        for n, (h, c) in enumerate(units):
            s = ahead.pop(0)
            if n + QK_AHEAD < len(units):
                ahead.append(scores(j, *units[n + QK_AHEAD]))
            first = False
            if diag is not None:
                for a in parts:
                    key_c0 = (diag * KV_TILE + c * KEY_SUB + a * SCORE_SUB) // CHUNK
                    key_c = key_c0 + lax.broadcasted_iota(jnp.int32, (SCORE_SUB, Q_TILE), 0) // CHUNK
                    s[a] = jnp.where(key_c <= qry_c, s[a], -jnp.inf)
                first = diag == 0 and c == 0
            s_max = functools.reduce(jnp.maximum, [jnp.max(v, axis=0, keepdims=True) for v in s])
            m_new = s_max if first else jnp.maximum(m_ref[h], s_max)
            p = jnp.concatenate([jnp.exp2(v - m_new).astype(BF16) for v in s], axis=0)
            pv = _dot(vT_ref[j, h * V_ROWS:(h + 1) * V_ROWS, c * KEY_SUB:(c + 1) * KEY_SUB], p)
            alpha = None if first else jnp.exp2(m_ref[h] - m_new)
            m_ref[h] = m_new
            pending.append((h, alpha, pv))
            if len(pending) > PV_BEHIND:
                fold(*pending.pop(0))
        for item in pending:
            fold(*item)

    n_diag = Q_TILE // KV_TILE
    for d in range(n_diag):
        sweep(i * n_diag + d, d)

    def body(j, carry):
        sweep(j, None)
        return carry

    lax.fori_loop(0, i * n_diag, body, 0)
    oT = jnp.concatenate([acc_ref[h, :V_DIM, :] / acc_ref[h, V_DIM:V_DIM + 1, :] for h in range(MLA_HEADS)], axis=0)
    o_ref[...] = oT.T.astype(BF16)


def _mla_attention(qT, k, vT):
    B, S, W = k.shape
    return pl.pallas_call(
        _mla_kernel,
        grid=(B, S // Q_TILE),
        in_specs=[
            pl.BlockSpec((None, None, W, Q_TILE), lambda b, i: (b, i, 0, 0)),
            pl.BlockSpec((None, S, W), lambda b, i: (b, 0, 0)),
            pl.BlockSpec((None, S // KV_TILE, MLA_HEADS * V_ROWS, KV_TILE), lambda b, i: (b, 0, 0, 0)),
        ],
        out_specs=pl.BlockSpec((None, Q_TILE, MLA_HEADS * V_DIM), lambda b, i: (b, i, 0)),
        out_shape=jax.ShapeDtypeStruct((B, S, MLA_HEADS * V_DIM), BF16),
        scratch_shapes=[pltpu.VMEM((MLA_HEADS, 1, Q_TILE), F32),
                        pltpu.VMEM((MLA_HEADS, V_ROWS, Q_TILE), F32)],
        compiler_params=pltpu.CompilerParams(
            dimension_semantics=("arbitrary", "arbitrary"), vmem_limit_bytes=VMEM_LIMIT),
        name="mla_attn",
    )(qT, k, vT)


def _route(logits_t, bias_t, tri_upper, tri_lower):
    tm = logits_t.shape[1]
    neg = -jnp.inf
    rg = lax.broadcasted_iota(jnp.int32, (8, tm), 0)
    re = lax.broadcasted_iota(jnp.int32, (N_EXPERTS, tm), 0)
    top = lambda v: jnp.max(v, axis=0, keepdims=True)

    lg = jnp.where(rg < N_GROUPS, logits_t[LANES:LANES + 8, :] + bias_t[LANES:LANES + 8, :], neg)
    ge = jnp.exp(lg - top(lg))
    gp = ge / jnp.sum(ge, axis=0, keepdims=True)
    g_w = top(gp)
    g_idx = jnp.min(jnp.where(gp == g_w, rg, 8), axis=0, keepdims=True)

    sel = re // EXPERTS_PER_GROUP == g_idx
    le = jnp.where(sel, logits_t[:N_EXPERTS, :] + bias_t[:N_EXPERTS, :], neg)
    ee = jnp.exp(le - top(le))
    ep = jnp.where(sel, ee / jnp.sum(ee, axis=0, keepdims=True), -1.0)
    w1 = top(ep)
    i1 = jnp.min(jnp.where(ep == w1, re, N_EXPERTS), axis=0, keepdims=True)
    ep2 = jnp.where(re == i1, -1.0, ep)
    w2 = top(ep2)
    i2 = jnp.min(jnp.where(ep2 == w2, re, N_EXPERTS), axis=0, keepdims=True)
    den = w1 + w2
    c1 = g_w * (w1 / den)
    c2 = g_w * (w2 / den)

    oh1 = (re == i1).astype(F32)
    oh2 = (re == i2).astype(F32)
    both = oh1 + oh2
    earlier = _dot(both.astype(BF16), tri_upper)
    pieces = jnp.floor((jnp.sum(both, axis=1, keepdims=True) + (PIECE - 1)) * (1.0 / PIECE))
    start = _dot(tri_lower, jnp.broadcast_to(pieces, (N_EXPERTS, LANES)).astype(BF16))[:, 0:1] * PIECE
    pos1 = jnp.sum(oh1 * (earlier + start), axis=0, keepdims=True)
    pos2 = jnp.sum(oh2 * (earlier + start), axis=0, keepdims=True)
    info_t = jnp.concatenate([pos1, pos2, c1, c2, jnp.zeros((4, tm), F32)], axis=0)
    return info_t, pieces


def _mix_kernel(x_ref, attn_ref, ga_ref, gyb_ref, kv_ref, wab_ref, wmix_ref, xg_ref, wxq_ref, wxo_ref,
                fg_ref, wr_hi_ref, wr_lo_ref, rb_ref, tri_ref, lower_ref,
                h_out, xn_out, info_out, infoT_out, pieces_out, logits_ref):
    tm = x_ref.shape[0]

    @pl.when(pl.program_id(0) == 0)
    def _():
        logits_ref[...] = jnp.zeros_like(logits_ref)

    info_t, pieces = _route(logits_ref[...].T, rb_ref[...], tri_ref[...], lower_ref[...])
    infoT_out[...] = info_t
    info_out[...] = jnp.concatenate([info_t, jnp.zeros((LANES - 8, tm), F32)], axis=0).T
    pieces_out[...] = jnp.broadcast_to(pieces, (N_EXPERTS, LANES))

    blocks = [slice(b * MIX_SUB, (b + 1) * MIX_SUB) for b in range(tm // MIX_SUB)]
    y_a = [_dot(attn_ref[r, :], wab_ref[...]) for r in blocks]
    merged = [(ga_ref[r, :].astype(F32) * y + gyb_ref[r, :].astype(F32)).astype(BF16) for r, y in zip(blocks, y_a)]
    h1 = [x_ref[r, :] + _dot(m, wmix_ref[...]) for r, m in zip(blocks, merged)]

    hn = [_rms(h, xg_ref[...]).astype(BF16) for h in h1]
    q = [_dot(v, wxq_ref[...]).astype(BF16) for v in hn]
    pairs = [(b, h) for h in range(MEM_HEADS) for b in range(len(blocks))]
    hs = lambda h: slice(h * MEM_HEAD_DIM, (h + 1) * MEM_HEAD_DIM)
    vs = lambda h: slice(D_MODEL + h * MEM_HEAD_DIM, D_MODEL + (h + 1) * MEM_HEAD_DIM)
    s = [_dot_nt(q[b][:, hs(h)], kv_ref[:, hs(h)]) for b, h in pairs]
    p = [jnp.exp(v - jnp.max(v, axis=-1, keepdims=True)) for v in s]
    o = [_dot(v.astype(BF16), kv_ref[:, vs(h)]) for v, (b, h) in zip(p, pairs)]
    heads = [[None] * MEM_HEADS for _ in blocks]
    for (b, h), ov, pv in zip(pairs, o, p):
        heads[b][h] = (ov / jnp.sum(pv, axis=-1, keepdims=True)).astype(BF16)
    h2 = [h + _dot(jnp.concatenate(hd, axis=1), wxo_ref[...]) for h, hd in zip(h1, heads)]

    xn = [_rms(h, fg_ref[...]) for h in h2]
    xn_hi = [v.astype(BF16) for v in xn]
    xn_lo = [(v - hi.astype(F32)).astype(BF16) for v, hi in zip(xn, xn_hi)]
    logits = [_dot(hi, wr_hi_ref[...]) + (_dot(hi, wr_lo_ref[...]) + _dot(lo, wr_hi_ref[...]))
              for hi, lo in zip(xn_hi, xn_lo)]
    for r, h, hi in zip(blocks, h2, xn_hi):
        h_out[r, :] = h
        xn_out[r, :] = hi
    for r, v in zip(blocks, logits):
        logits_ref[r, :] = v


def _mix_xattn(x, attn, ga, gyb, memkv, wab, wmix, xg, wxq, wxo, fg, wr_hi, wr_lo, rb):
    B, S, _ = x.shape
    M = memkv.shape[1]
    tm = MOE_CHUNK
    nt = S // tm
    n_tiles = B * nt
    cur = lambda t: jnp.minimum(t, n_tiles - 1)
    prev = lambda t: jnp.maximum(t - 1, 0)
    row = lambda t: (cur(t) // nt, cur(t) % nt, 0)
    const2 = lambda t: (0, 0)
    tri = (lax.broadcasted_iota(jnp.int32, (tm, tm), 0) < lax.broadcasted_iota(jnp.int32, (tm, tm), 1)).astype(BF16)
    lower = (lax.broadcasted_iota(jnp.int32, (N_EXPERTS, N_EXPERTS), 1)
             < lax.broadcasted_iota(jnp.int32, (N_EXPERTS, N_EXPERTS), 0)).astype(BF16)
    rb = jnp.broadcast_to(rb.reshape(2 * LANES, 1), (2 * LANES, tm))
    return pl.pallas_call(
        _mix_kernel,
        grid=(n_tiles + 1,),
        in_specs=[
            pl.BlockSpec((None, tm, D_MODEL), row),
            pl.BlockSpec((None, tm, MLA_HEADS * V_DIM), row),
            pl.BlockSpec((None, tm, D_MODEL), row),
            pl.BlockSpec((None, tm, D_MODEL), row),
            pl.BlockSpec((None, M, 2 * D_MODEL), lambda t: (cur(t) // nt, 0, 0)),
            pl.BlockSpec(wab.shape, const2),
            pl.BlockSpec(wmix.shape, const2),
            pl.BlockSpec((1, D_MODEL), const2),
            pl.BlockSpec(wxq.shape, const2),
            pl.BlockSpec(wxo.shape, const2),
            pl.BlockSpec((1, D_MODEL), const2),
            pl.BlockSpec(wr_hi.shape, const2),
            pl.BlockSpec(wr_lo.shape, const2),
            pl.BlockSpec((2 * LANES, tm), const2),
            pl.BlockSpec((tm, tm), const2),
            pl.BlockSpec((N_EXPERTS, N_EXPERTS), const2),
        ],
        out_specs=[
            pl.BlockSpec((None, tm, D_MODEL), row),
            pl.BlockSpec((None, tm, D_MODEL), row),
            pl.BlockSpec((None, tm, LANES), lambda t: (prev(t) // nt, prev(t) % nt, 0)),
            pl.BlockSpec((8, tm), lambda t: (0, prev(t))),
            pl.BlockSpec((None, N_EXPERTS, LANES), lambda t: (prev(t), 0, 0)),
        ],
        out_shape=[jax.ShapeDtypeStruct((B, S, D_MODEL), F32),
                   jax.ShapeDtypeStruct((B, S, D_MODEL), BF16),
                   jax.ShapeDtypeStruct((B, S, LANES), F32),
                   jax.ShapeDtypeStruct((8, B * S), F32),
                   jax.ShapeDtypeStruct((B * nt, N_EXPERTS, LANES), F32)],
        scratch_shapes=[pltpu.VMEM((tm, 2 * LANES), F32)],
        compiler_params=pltpu.CompilerParams(
            dimension_semantics=("arbitrary",), vmem_limit_bytes=VMEM_LIMIT),
        name="mix_xattn",
    )(x, attn, ga, gyb, memkv, wab, wmix, xg, wxq, wxo, fg, wr_hi, wr_lo, rb, tri, lower)


def _piece_copy(src_ref, dst_ref, sem):
    return pltpu.make_async_copy(src_ref, dst_ref, sem)


def _for_segments(seg_ref, chunk, fn):
    base = chunk * SEG_STRIDE

    def body(k, carry):
        at = base + 1 + 3 * k
        fn(seg_ref[at], seg_ref[at + 1], seg_ref[at + 2])
        return carry

    lax.fori_loop(0, seg_ref[base], body, 0)


def _dispatch_kernel(seg_ref, np_ref, gap_ref, fill_ref, xn_ref, infoT_ref, xs_hbm, buf_ref, zero_ref, sem_ref):
    c = pl.program_id(0)
    n = pl.num_programs(0)
    slot = c % 2

    tile_pieces = EXPERT_TILE // PIECE
    n_tiles = xs_hbm.shape[0] // tile_pieces

    def gap_copy(g):
        return _piece_copy(zero_ref.at[0], xs_hbm.at[gap_ref[g]], sem_ref.at[2])

    def tail_copy(t):
        return _piece_copy(zero_ref, xs_hbm.at[pl.ds(t * tile_pieces, tile_pieces)], sem_ref.at[2])

    @pl.when(c == 0)
    def _():
        zero_ref[...] = jnp.zeros_like(zero_ref)
        lax.fori_loop(0, fill_ref[0], lambda g, carry: (gap_copy(g).start(), carry)[1], 0)
        lax.fori_loop(fill_ref[1], n_tiles, lambda t, carry: (tail_copy(t).start(), carry)[1], 0)

    def start_all(cc, s):
        _for_segments(seg_ref, cc, lambda lo, hi, cnt: _piece_copy(
            buf_ref.at[s, pl.ds(lo, cnt)], xs_hbm.at[pl.ds(hi, cnt)], sem_ref.at[s]).start())

    def wait_all(cc, s):
        n_pieces = np_ref[cc]

        @pl.when(n_pieces > 0)
        def _():
            _piece_copy(buf_ref.at[s, pl.ds(0, n_pieces)], xs_hbm.at[pl.ds(0, n_pieces)], sem_ref.at[s]).wait()

    @pl.when(c >= 2)
    def _():
        wait_all(c - 2, slot)

    pos1 = infoT_ref[0:1, :]
    pos2 = infoT_ref[1:2, :]
    r = lax.broadcasted_iota(jnp.int32, (CHUNK_ROWS, MOE_CHUNK), 0).astype(F32)
    onehot = jnp.where((r == pos1) | (r == pos2), 1.0, 0.0).astype(BF16)
    buf_ref[slot] = _dot(onehot, xn_ref[...]).astype(BF16).reshape(MAX_PIECES, PIECE, D_MODEL)
    start_all(c, slot)

    @pl.when(c == n - 1)
    def _():
        @pl.when(c >= 1)
        def _():
            wait_all(c - 1, 1 - slot)
        wait_all(c, slot)
        lax.fori_loop(0, fill_ref[0], lambda g, carry: (gap_copy(g).wait(), carry)[1], 0)
        lax.fori_loop(fill_ref[1], n_tiles, lambda t, carry: (tail_copy(t).wait(), carry)[1], 0)


def _dispatch(xn, infoT, seg, npc, gaps, fill, rows_max):
    T = xn.shape[0]
    grid_spec = pltpu.PrefetchScalarGridSpec(
        num_scalar_prefetch=4,
        grid=(T // MOE_CHUNK,),
        in_specs=[
            pl.BlockSpec((MOE_CHUNK, D_MODEL), lambda c, *_: (c, 0)),
            pl.BlockSpec((8, MOE_CHUNK), lambda c, *_: (0, c)),
        ],
        out_specs=pl.BlockSpec(memory_space=pl.ANY),
        scratch_shapes=[pltpu.VMEM((2, MAX_PIECES, PIECE, D_MODEL), BF16),
                        pltpu.VMEM((EXPERT_TILE // PIECE, PIECE, D_MODEL), BF16),
                        pltpu.SemaphoreType.DMA((3,))],
    )
    xs = pl.pallas_call(
        _dispatch_kernel,
        grid_spec=grid_spec,
        out_shape=jax.ShapeDtypeStruct((rows_max // PIECE, PIECE, D_MODEL), BF16),
        compiler_params=pltpu.CompilerParams(
            dimension_semantics=("arbitrary",), vmem_limit_bytes=VMEM_LIMIT),
        name="moe_dispatch",
    )(seg, npc, gaps, fill, xn, infoT)
    return xs.reshape(rows_max, D_MODEL)


def _expert_kernel(te_ref, tv_ref, nu_ref, x_ref, wg_ref, wu_ref, wd_ref, y_ref, wgu_bf, wd_bf):
    i = pl.program_id(0)

    @pl.when(i < nu_ref[0])
    def _():
        @pl.when((i == 0) | (te_ref[i] != te_ref[jnp.maximum(i - 1, 0)]))
        def _():
            wgu_bf[:, :EXPERT_FF] = wg_ref[...].astype(BF16)
            wgu_bf[:, EXPERT_FF:] = wu_ref[...].astype(BF16)
            wd_bf[...] = wd_ref[...].astype(BF16)

        blocks = [slice(b * EXPERT_SUB, (b + 1) * EXPERT_SUB) for b in range(EXPERT_TILE // EXPERT_SUB)]
        row = lax.broadcasted_iota(jnp.int32, (EXPERT_SUB, D_MODEL), 0)
        gus = []
        for b, rows in enumerate(blocks):
            x = x_ref[rows, :]
            x = jnp.where(row < tv_ref[i] - b * EXPERT_SUB, x, jnp.zeros_like(x))
            gus.append(_dot(x, wgu_bf[...]))
        hids = []
        for gu in gus:
            gate = gu[:, :EXPERT_FF]
            hids.append((gate * jax.nn.sigmoid(gate) * gu[:, EXPERT_FF:]).astype(BF16))
        for rows, hid in zip(blocks, hids):
            y_ref[rows, :] = _dot(hid, wd_bf[...]).astype(BF16)

    @pl.when(i >= nu_ref[0])
    def _():
        y_ref[...] = jnp.zeros_like(y_ref)


def _experts(xs, w_gate, w_up, w_down, tile_expert, tile_valid, n_used):
    rows_max = xs.shape[0]
    last = lambda i, nu: jnp.minimum(i, nu[0] - 1)
    expert = lambda i, te, tv, nu: (te[last(i, nu)], 0, 0)
    grid_spec = pltpu.PrefetchScalarGridSpec(
        num_scalar_prefetch=3,
        grid=(rows_max // EXPERT_TILE,),
        in_specs=[
            pl.BlockSpec((EXPERT_TILE, D_MODEL), lambda i, te, tv, nu: (last(i, nu), 0)),
            pl.BlockSpec((None, D_MODEL, EXPERT_FF), expert),
            pl.BlockSpec((None, D_MODEL, EXPERT_FF), expert),
            pl.BlockSpec((None, EXPERT_FF, D_MODEL), expert),
        ],
        out_specs=pl.BlockSpec((EXPERT_TILE, D_MODEL), lambda i, te, tv, nu: (i, 0)),
        scratch_shapes=[pltpu.VMEM((D_MODEL, 2 * EXPERT_FF), BF16), pltpu.VMEM((EXPERT_FF, D_MODEL), BF16)],
    )
    return pl.pallas_call(
        _expert_kernel,
        grid_spec=grid_spec,
        out_shape=jax.ShapeDtypeStruct((rows_max, D_MODEL), BF16),
        compiler_params=pltpu.CompilerParams(
            dimension_semantics=("arbitrary",), vmem_limit_bytes=VMEM_LIMIT),
        name="moe_experts",
    )(tile_expert, tile_valid, n_used, xs, w_gate, w_up, w_down)


def _combine_kernel(seg_ref, np_ref, h_ref, info_ref, fg_ref, ys_hbm, o_ref, buf_ref, sem_ref):
    c = pl.program_id(0)
    n = pl.num_programs(0)
    slot = c % 2

    def start_all(cc, s):
        _for_segments(seg_ref, cc, lambda lo, hi, cnt: _piece_copy(
            ys_hbm.at[pl.ds(hi, cnt)], buf_ref.at[s, pl.ds(lo, cnt)], sem_ref.at[s]).start())

    def wait_all(cc, s):
        n_pieces = np_ref[cc]

        @pl.when(n_pieces > 0)
        def _():
            _piece_copy(ys_hbm.at[pl.ds(0, n_pieces)], buf_ref.at[s, pl.ds(0, n_pieces)], sem_ref.at[s]).wait()

    @pl.when(c == 0)
    def _():
        buf_ref[...] = jnp.zeros_like(buf_ref)
        start_all(0, 0)

    @pl.when(c + 1 < n)
    def _():
        start_all(c + 1, 1 - slot)

    wait_all(c, slot)
    info = info_ref[...]
    r = lax.broadcasted_iota(jnp.int32, (MOE_CHUNK, CHUNK_ROWS), 1).astype(F32)
    weights = jnp.where(r == info[:, 0:1], info[:, 2:3], 0.0) + jnp.where(r == info[:, 1:2], info[:, 3:4], 0.0)
    moe = _dot(weights.astype(BF16), buf_ref[slot].reshape(CHUNK_ROWS, D_MODEL))
    o_ref[...] = _rms(h_ref[...] + moe, fg_ref[...])


def _combine(h2, info, fg, ys, seg, npc):
    T = h2.shape[0]
    grid_spec = pltpu.PrefetchScalarGridSpec(
        num_scalar_prefetch=2,
        grid=(T // MOE_CHUNK,),
        in_specs=[
            pl.BlockSpec((MOE_CHUNK, D_MODEL), lambda c, seg, npc: (c, 0)),
            pl.BlockSpec((MOE_CHUNK, LANES), lambda c, seg, npc: (c, 0)),
            pl.BlockSpec((1, D_MODEL), lambda c, seg, npc: (0, 0)),
            pl.BlockSpec(memory_space=pl.ANY),
        ],
        out_specs=pl.BlockSpec((MOE_CHUNK, D_MODEL), lambda c, seg, npc: (c, 0)),
        scratch_shapes=[pltpu.VMEM((2, MAX_PIECES, PIECE, D_MODEL), BF16), pltpu.SemaphoreType.DMA((2,))],
    )
    ys = ys.reshape(-1, PIECE, D_MODEL)
    return pl.pallas_call(
        _combine_kernel,
        grid_spec=grid_spec,
        out_shape=jax.ShapeDtypeStruct((T, D_MODEL), F32),
        compiler_params=pltpu.CompilerParams(
            dimension_semantics=("arbitrary",), vmem_limit_bytes=VMEM_LIMIT),
        name="moe_combine",
    )(seg, npc, h2, info, fg, ys)


def _routing_tables(pieces, rows_max):
    tile_pieces = EXPERT_TILE // PIECE
    total = jnp.sum(pieces, axis=0)
    total_al = (total + tile_pieces - 1) // tile_pieces * tile_pieces
    seg_end = jnp.cumsum(total_al)
    seg_start = seg_end - total_al
    chunk_off = jnp.cumsum(pieces, axis=0) - pieces
    loc_end = jnp.cumsum(pieces, axis=1)
    loc_start = loc_end - pieces
    n_local = loc_end[:, -1]
    experts = jnp.arange(N_EXPERTS, dtype=jnp.int32)
    nonempty = pieces > 0
    slot = jnp.cumsum(nonempty.astype(jnp.int32), axis=1) - 1
    pick = (nonempty[:, None, :] & (slot[:, None, :] == experts[None, :, None])).astype(jnp.int32)
    compact = lambda a: jnp.sum(pick * a[:, None, :], axis=-1)
    triples = jnp.stack([compact(loc_start), compact(seg_start[None, :] + chunk_off), compact(pieces)], axis=-1)
    n_seg = jnp.sum(nonempty.astype(jnp.int32), axis=1, keepdims=True)
    seg = jnp.concatenate([n_seg, triples.reshape(pieces.shape[0], 3 * N_EXPERTS)], axis=1)

    t0 = jnp.arange(rows_max // EXPERT_TILE, dtype=jnp.int32) * tile_pieces
    tile_expert = jnp.minimum(jnp.sum((t0[:, None] >= seg_end[None, :]).astype(jnp.int32), axis=-1), N_EXPERTS - 1)
    copies_end = jnp.sum((tile_expert[:, None] == experts).astype(jnp.int32) * (seg_start + total)[None, :], axis=-1)
    tile_valid = jnp.clip((copies_end - t0) * PIECE, 0, EXPERT_TILE)
    n_used = (seg_end[-1] // tile_pieces).reshape(1)

    k = jnp.arange(tile_pieces, dtype=jnp.int32)
    is_gap = (k[None, :] < (total_al - total)[:, None]).reshape(-1)
    gap_piece = (seg_start + total)[:, None] + k[None, :]
    order = jnp.argsort(jnp.logical_not(is_gap), stable=True)
    gaps = gap_piece.reshape(-1)[order]
    fill = jnp.stack([jnp.sum(is_gap.astype(jnp.int32)), n_used[0]])
    i32 = lambda a: a.astype(jnp.int32)
    return i32(seg.reshape(-1)), i32(n_local), i32(tile_expert), i32(tile_valid), i32(n_used), i32(gaps), i32(fill)


def _rope_tables(positions):
    inv_freq = 1.0 / (ROPE_THETA ** (jnp.arange(0, ROPE_DIM, 2, dtype=F32) / ROPE_DIM))
    ang = positions.astype(F32)[:, None, :] * inv_freq[None, :, None]
    cos, sin = jnp.cos(ang), jnp.sin(ang)
    return jnp.concatenate([cos, sin], axis=1)


def _pad_heads(w, heads, width):
    k = w.shape[0]
    w = w.reshape(k, heads, width)
    w = jnp.pad(w, ((0, 0), (0, 0), (0, HEAD_PAD - width)))
    return w.reshape(k, heads * HEAD_PAD)


def _layer(l, h, mem, tables, mix_norm_g, w_in, q_norm_g, w_q_up, kv_norm_g, w_kv_up, w_attn_branch,
           pool_w, pool_scale, w_pool_branch, w_mix_out, xattn_norm_g, mem_norm_g, w_xq, w_xkv, w_xo,
           ffn_norm_g, w_router_group, b_router_group, w_router_expert, b_router_expert,
           w_exp_gate, w_exp_up, w_exp_down, out_g, tm_proj):
    B, S, _ = h.shape
    row2 = lambda v: v.reshape(1, -1).astype(F32)

    wi = w_in[l]
    kr_cols = jnp.pad(wi[:, Q_LORA + KV_LORA:Q_LORA + KV_LORA + ROPE_DIM],
                      ((0, 0), (NOPE_DIM, LANES - NOPE_DIM - ROPE_DIM)))
    win = jnp.concatenate([wi[:, :Q_LORA + KV_LORA], kr_cols, wi[:, Q_LORA + KV_LORA + ROPE_DIM:]], axis=1).astype(BF16)
    scale = math.log2(math.e) / math.sqrt(NOPE_DIM + ROPE_DIM)
    wq3 = (w_q_up[l] * scale).reshape(Q_LORA, MLA_HEADS, NOPE_DIM + ROPE_DIM)
    half = ROPE_DIM // 2
    wq = jnp.concatenate([wq3[:, :, :NOPE_DIM].reshape(Q_LORA, -1),
                          wq3[:, :, NOPE_DIM:NOPE_DIM + half].reshape(Q_LORA, -1),
                          wq3[:, :, NOPE_DIM + half:].reshape(Q_LORA, -1)], axis=1).astype(BF16)
    wkv3 = w_kv_up[l].reshape(KV_LORA, MLA_HEADS, NOPE_DIM + V_DIM)
    wkv = jnp.concatenate([
        _pad_heads(wkv3[:, :, :NOPE_DIM].reshape(KV_LORA, -1), MLA_HEADS, NOPE_DIM),
        wkv3[:, :, NOPE_DIM:].reshape(KV_LORA, -1)], axis=1).astype(BF16)

    memkv = _mem_kv(mem.reshape(-1, D_MODEL), row2(mem_norm_g[l]), w_xkv[l].astype(BF16))
    memkv = memkv.reshape(B, -1, 2 * D_MODEL)

    qT, k, vT, ga, gyb = _in_proj(
        h, tables, row2(mix_norm_g[l]), win, row2(q_norm_g[l]), wq, row2(kv_norm_g[l]), wkv,
        pool_w[l].astype(BF16), row2(pool_scale[l]), w_pool_branch[l].astype(BF16), tm_proj)
    attn = _mla_attention(qT, k, vT)

    w_r = jnp.zeros((D_MODEL, 2 * LANES), F32)
    w_r = w_r.at[:, :N_EXPERTS].set(w_router_expert[l]).at[:, LANES:LANES + N_GROUPS].set(w_router_group[l])
    wr_hi = w_r.astype(BF16)
    wr_lo = (w_r - wr_hi.astype(F32)).astype(BF16)
    rb = jnp.zeros((1, 2 * LANES), F32)
    rb = rb.at[0, :N_EXPERTS].set(b_router_expert[l]).at[0, LANES:LANES + N_GROUPS].set(b_router_group[l])

    h2, xn, info, infoT, pieces = _mix_xattn(
        h, attn, ga, gyb, memkv, w_attn_branch[l].astype(BF16), w_mix_out[l].astype(BF16),
        row2(xattn_norm_g[l]), (w_xq[l] * (1.0 / math.sqrt(MEM_HEAD_DIM))).astype(BF16), w_xo[l].astype(BF16),
        row2(ffn_norm_g[l]), wr_hi, wr_lo, rb)

    T = B * S
    n_chunks = T // MOE_CHUNK
    tile_pieces = EXPERT_TILE // PIECE
    max_pieces = 2 * T // PIECE + n_chunks * N_EXPERTS + N_EXPERTS * tile_pieces
    rows_max = -(-max_pieces // tile_pieces) * EXPERT_TILE
    seg, n_local, tile_expert, tile_valid, n_used, gaps, fill = _routing_tables(
        pieces[:, :, 0].astype(jnp.int32), rows_max)

    xs = _dispatch(xn.reshape(T, D_MODEL), infoT, seg, n_local, gaps, fill, rows_max)
    ys = _experts(xs, w_exp_gate[l], w_exp_up[l], w_exp_down[l], tile_expert, tile_valid, n_used)
    out = _combine(h2.reshape(T, D_MODEL), info.reshape(T, LANES), row2(out_g), ys, seg, n_local)
    return out.reshape(B, S, D_MODEL)


def kernel(x, mem, positions, mix_norm_g, w_in, q_norm_g, w_q_up, kv_norm_g, w_kv_up, w_attn_branch, pool_w, pool_scale, w_pool_branch, w_mix_out, xattn_norm_g, mem_norm_g, w_xq, w_xkv, w_xo, ffn_norm_g, w_router_group, b_router_group, w_router_expert, b_router_expert, w_exp_gate, w_exp_up, w_exp_down, final_norm_g):
    depth = w_in.shape[0]
    assert depth == 1, "the combine kernel fuses the final RMSNorm, which is only valid after the last layer"
    assert x.shape[1] % Q_TILE == 0 and x.shape[1] % MOE_CHUNK == 0
    tables = _rope_tables(positions)
    return _layer(0, x, mem, tables, mix_norm_g, w_in, q_norm_g, w_q_up, kv_norm_g, w_kv_up, w_attn_branch,
                  pool_w, pool_scale, w_pool_branch, w_mix_out, xattn_norm_g, mem_norm_g, w_xq, w_xkv, w_xo,
                  ffn_norm_g, w_router_group, b_router_group, w_router_expert, b_router_expert,
                  w_exp_gate, w_exp_up, w_exp_down, final_norm_g, IN_PROJ_TILE)
```

```python
import functools
import math

import jax
import jax.numpy as jnp
from jax import lax
from jax.experimental import pallas as pl
from jax.experimental.pallas import tpu as pltpu

F32 = jnp.float32
BF16 = jnp.bfloat16

D_MODEL = 1024
CHUNK = 64
MLA_HEADS = 8
Q_LORA = 384
KV_LORA = 256
NOPE_DIM = 64
ROPE_DIM = 32
V_DIM = 64
ROPE_THETA = 10000.0
POOL_WIDTH = 512
POOL_WINDOWS = (2, 4, 8, 16)
POOL_GROUP_DIM = POOL_WIDTH // len(POOL_WINDOWS)
POOL_HALO = 16
MEM_HEADS = 4
MEM_HEAD_DIM = D_MODEL // MEM_HEADS
N_GROUPS = 4
EXPERTS_PER_GROUP = 8
N_EXPERTS = N_GROUPS * EXPERTS_PER_GROUP
EXPERT_FF = 256
EPS = 1e-6

LANES = 128
HEAD_PAD = LANES
Q_TILE = 512
KV_TILE = 512
KEY_SUB = 128
SCORE_SUB = 128
V_ROWS = V_DIM + 16
IN_PROJ_TILE = 512
MOE_CHUNK = 512
PIECE = 16
MAX_PIECES = 2 * MOE_CHUNK // PIECE + N_EXPERTS
COMBINE_SUB = 256
SEG_STRIDE = 1 + 3 * N_EXPERTS
CHUNK_ROWS = MAX_PIECES * PIECE
EXPERT_TILE = 1024
EXPERT_SUB = 256
PROJ_SUB = 256
MIX_SUB = 256
QK_AHEAD = 3
PV_BEHIND = 2

_C_Q = 0
_C_KV = _C_Q + Q_LORA
_C_KR = _C_KV + KV_LORA
_C_POOL = _C_KR + LANES
_C_GA = _C_POOL + POOL_WIDTH
_C_GB = _C_GA + D_MODEL
_C_END = _C_GB + D_MODEL

VMEM_LIMIT = 56 * 1024 * 1024


def _rms(x, g):
    return x * lax.rsqrt(jnp.mean(x * x, axis=-1, keepdims=True) + EPS) * g


def _dot(a, b):
    return jnp.dot(a, b, preferred_element_type=F32)


def _dot_nt(a, b):
    return lax.dot_general(a, b, (((1,), (1,)), ((), ())), preferred_element_type=F32)


def _mem_kv_kernel(mem_ref, g_ref, w_ref, kv_ref):
    mn = _rms(mem_ref[...], g_ref[...]).astype(BF16)
    kv_ref[...] = _dot(mn, w_ref[...]).astype(BF16)


def _mem_kv(mem2d, g, w_xkv):
    rows = mem2d.shape[0]
    tm = min(512, rows)
    assert rows % tm == 0
    return pl.pallas_call(
        _mem_kv_kernel,
        grid=(rows // tm,),
        in_specs=[
            pl.BlockSpec((tm, D_MODEL), lambda i: (i, 0)),
            pl.BlockSpec((1, D_MODEL), lambda i: (0, 0)),
            pl.BlockSpec((D_MODEL, 2 * D_MODEL), lambda i: (0, 0)),
        ],
        out_specs=pl.BlockSpec((tm, 2 * D_MODEL), lambda i: (i, 0)),
        out_shape=jax.ShapeDtypeStruct((rows, 2 * D_MODEL), BF16),
        compiler_params=pltpu.CompilerParams(vmem_limit_bytes=VMEM_LIMIT),
        name="mem_kv",
    )(mem2d, g, w_xkv)


def _rope(t, c, sa, sb):
    w = t.shape[-1]
    return t * c + pltpu.roll(t, ROPE_DIM // 2, 1) * sa + pltpu.roll(t, w - ROPE_DIM // 2, 1) * sb


def _in_proj_kernel(x_ref, rot_ref, g_ref, win_ref, qg_ref, wq_ref, kvg_ref, wkv_ref,
                    poolw_ref, pscale_ref, wpb_ref,
                    qT_out, k_out, vT_out, ga_out, gyb_out, hist_ref):
    tm = x_ref.shape[0]
    i = pl.program_id(1)
    blocks = [slice(b * PROJ_SUB, (b + 1) * PROJ_SUB) for b in range(tm // PROJ_SUB)]
    rows = lambda parts: jnp.concatenate(parts, axis=0)
    hn = [_rms(x_ref[r, :], g_ref[...]).astype(BF16) for r in blocks]
    proj = lambda lo, hi: [_dot(v, win_ref[:, lo:hi]) for v in hn]

    half = ROPE_DIM // 2
    cos8 = jnp.concatenate([rot_ref[0:half, :]] * MLA_HEADS, axis=0).T
    sin8 = jnp.concatenate([rot_ref[half:, :]] * MLA_HEADS, axis=0).T
    lane = lax.broadcasted_iota(jnp.int32, cos8.shape, 1)
    rope_lo = NOPE_DIM
    c1 = jnp.where(lane < rope_lo + ROPE_DIM, cos8, 0.0)
    sa1 = jnp.where((lane >= rope_lo + half) & (lane < rope_lo + ROPE_DIM), sin8, 0.0)
    sb1 = jnp.where((lane >= rope_lo) & (lane < rope_lo + half), -sin8, 0.0)

    q_lat = proj(_C_Q, _C_KV)
    kv_lat = proj(_C_KV, _C_KR)
    qn = [_rms(v, qg_ref[...]).astype(BF16) for v in q_lat]
    kvn = [_rms(v, kvg_ref[...]).astype(BF16) for v in kv_lat]
    q = rows([_dot(v, wq_ref[...]) for v in qn])
    n_nope = MLA_HEADS * NOPE_DIM
    x1 = q[:, n_nope:n_nope + LANES]
    x2 = q[:, n_nope + LANES:]
    qT = jnp.concatenate([q[:, :n_nope], x1 * cos8 - x2 * sin8, x2 * cos8 + x1 * sin8], axis=1).T
    pad = jnp.zeros((HEAD_PAD - NOPE_DIM - ROPE_DIM, tm), F32)
    qT = jnp.concatenate(
        [blk for h in range(MLA_HEADS) for blk in (
            qT[h * NOPE_DIM:(h + 1) * NOPE_DIM, :],
            qT[n_nope + h * half:n_nope + (h + 1) * half, :],
            qT[n_nope + LANES + h * half:n_nope + LANES + (h + 1) * half, :], pad)], axis=0).astype(BF16)

    k_nope = rows([_dot(v, wkv_ref[:, 0:MLA_HEADS * HEAD_PAD]) for v in kvn])
    kr = _rope(rows(proj(_C_KR, _C_POOL)), c1, sa1, sb1)
    k_out[...] = (k_nope + jnp.tile(kr, (1, MLA_HEADS))).astype(BF16)
    v = rows([_dot(t, wkv_ref[:, MLA_HEADS * HEAD_PAD:]) for t in kvn])
    for t in range(tm // Q_TILE):
        qT_out[t] = qT[:, t * Q_TILE:(t + 1) * Q_TILE]
    vT = v.T
    ones = jnp.ones((V_ROWS - V_DIM, tm), F32)
    vT = jnp.concatenate(
        [blk for h in range(MLA_HEADS) for blk in (vT[h * V_DIM:(h + 1) * V_DIM, :], ones)], axis=0).astype(BF16)
    for t in range(tm // KV_TILE):
        vT_out[t] = vT[:, t * KV_TILE:(t + 1) * KV_TILE]

    u = rows(proj(_C_POOL, _C_GA))

    @pl.when(i == 0)
    def _():
        hist_ref[...] = jnp.zeros_like(hist_ref)

    ext = jnp.concatenate([hist_ref[...], u], axis=0)
    hist_ref[...] = u[tm - POOL_HALO:, :]
    t_idx = i * tm + lax.broadcasted_iota(jnp.int32, (tm, POOL_GROUP_DIM), 0)
    ys = []
    for g, w in enumerate(POOL_WINDOWS):
        c0 = g * POOL_GROUP_DIM
        run = ext[:, c0:c0 + POOL_GROUP_DIM]
        span = 1
        while span < w:
            run = run + pltpu.roll(run, span, 0)
            span *= 2
        cnt = jnp.minimum(t_idx + 1, w).astype(F32)
        d = run[POOL_HALO:, :] / cnt - u[:, c0:c0 + POOL_GROUP_DIM]
        ys.append(_dot(d.astype(BF16), poolw_ref[g]))
    y = (jnp.concatenate(ys, axis=1) * pscale_ref[...]).astype(BF16)
    y_b = [_dot(y[r, :], wpb_ref[...]) for r in blocks]

    for r, t in zip(blocks, proj(_C_GA, _C_GB)):
        ga_out[r, :] = jax.nn.sigmoid(t).astype(BF16)
    for r, t, yb in zip(blocks, proj(_C_GB, _C_END), y_b):
        gyb_out[r, :] = (jax.nn.sigmoid(t) * yb).astype(BF16)


def _in_proj(x, rot, g, win, qg, wq, kvg, wkv, poolw, pscale, wpb, tm):
    B, S, _ = x.shape
    row = lambda b, i: (b, i, 0)
    const2 = lambda b, i: (0, 0)
    const3 = lambda b, i: (0, 0, 0)
    slab = lambda b, i: (b, i, 0, 0)
    return pl.pallas_call(
        _in_proj_kernel,
        grid=(B, S // tm),
        in_specs=[
            pl.BlockSpec((None, tm, D_MODEL), row),
            pl.BlockSpec((None, ROPE_DIM, tm), lambda b, i: (b, 0, i)),
            pl.BlockSpec((1, D_MODEL), const2),
            pl.BlockSpec(win.shape, const2),
            pl.BlockSpec((1, Q_LORA), const2),
            pl.BlockSpec(wq.shape, const2),
            pl.BlockSpec((1, KV_LORA), const2),
            pl.BlockSpec(wkv.shape, const2),
            pl.BlockSpec(poolw.shape, const3),
            pl.BlockSpec((1, POOL_WIDTH), const2),
            pl.BlockSpec(wpb.shape, const2),
        ],
        out_specs=[
            pl.BlockSpec((None, tm // Q_TILE, MLA_HEADS * HEAD_PAD, Q_TILE), slab),
            pl.BlockSpec((None, tm, MLA_HEADS * HEAD_PAD), row),
            pl.BlockSpec((None, tm // KV_TILE, MLA_HEADS * V_ROWS, KV_TILE), slab),
            pl.BlockSpec((None, tm, D_MODEL), row),
            pl.BlockSpec((None, tm, D_MODEL), row),
        ],
        out_shape=[jax.ShapeDtypeStruct((B, S // Q_TILE, MLA_HEADS * HEAD_PAD, Q_TILE), BF16),
                   jax.ShapeDtypeStruct((B, S, MLA_HEADS * HEAD_PAD), BF16),
                   jax.ShapeDtypeStruct((B, S // KV_TILE, MLA_HEADS * V_ROWS, KV_TILE), BF16),
                   jax.ShapeDtypeStruct((B, S, D_MODEL), BF16),
                   jax.ShapeDtypeStruct((B, S, D_MODEL), BF16)],
        scratch_shapes=[pltpu.VMEM((POOL_HALO, POOL_WIDTH), F32)],
        compiler_params=pltpu.CompilerParams(
            dimension_semantics=("arbitrary", "arbitrary"), vmem_limit_bytes=VMEM_LIMIT),
        name="in_proj",
    )(x, rot, g, win, qg, wq, kvg, wkv, poolw, pscale, wpb)


def _mla_kernel(qT_ref, k_ref, vT_ref, o_ref, m_ref, acc_ref):
    i = pl.program_id(1)
    n_sub = KV_TILE // KEY_SUB
    units = [(h, c) for c in range(n_sub) for h in range(MLA_HEADS)]
    qry_c = lax.broadcasted_iota(jnp.int32, (SCORE_SUB, Q_TILE), 1) // CHUNK
    parts = range(KEY_SUB // SCORE_SUB)

    def scores(j, h, c):
        hs = slice(h * HEAD_PAD, (h + 1) * HEAD_PAD)
        out = []
        for a in parts:
            rows = pl.ds(pl.multiple_of(j * KV_TILE + c * KEY_SUB + a * SCORE_SUB, SCORE_SUB), SCORE_SUB)
            out.append(_dot(k_ref[rows, hs], qT_ref[hs, :]))
        return out

    def fold(h, alpha, pv):
        acc_ref[h] = pv if alpha is None else alpha * acc_ref[h] + pv

    def sweep(j, diag):
        ahead = [scores(j, *u) for u in units[:QK_AHEAD]]
        pending = []
        for n, (h, c) in enumerate(units):
            s = ahead.pop(0)
            if n + QK_AHEAD < len(units):
                ahead.append(scores(j, *units[n + QK_AHEAD]))
            first = False
            if diag is not None:
                for a in parts:
                    key_c0 = (diag * KV_TILE + c * KEY_SUB + a * SCORE_SUB) // CHUNK
                    key_c = key_c0 + lax.broadcasted_iota(jnp.int32, (SCORE_SUB, Q_TILE), 0) // CHUNK
                    s[a] = jnp.where(key_c <= qry_c, s[a], -jnp.inf)
                first = diag == 0 and c == 0
            s_max = functools.reduce(jnp.maximum, [jnp.max(v, axis=0, keepdims=True) for v in s])
            m_new = s_max if first else jnp.maximum(m_ref[h], s_max)
            p = jnp.concatenate([jnp.exp2(v - m_new).astype(BF16) for v in s], axis=0)
            pv = _dot(vT_ref[j, h * V_ROWS:(h + 1) * V_ROWS, c * KEY_SUB:(c + 1) * KEY_SUB], p)
            alpha = None if first else jnp.exp2(m_ref[h] - m_new)
            m_ref[h] = m_new
            pending.append((h, alpha, pv))
            if len(pending) > PV_BEHIND:
                fold(*pending.pop(0))
        for item in pending:
            fold(*item)

    n_diag = Q_TILE // KV_TILE
    for d in range(n_diag):
        sweep(i * n_diag + d, d)

    def body(j, carry):
        sweep(j, None)
        return carry

    lax.fori_loop(0, i * n_diag, body, 0)
    oT = jnp.concatenate([acc_ref[h, :V_DIM, :] / acc_ref[h, V_DIM:V_DIM + 1, :] for h in range(MLA_HEADS)], axis=0)
    o_ref[...] = oT.T.astype(BF16)


def _mla_attention(qT, k, vT):
    B, S, W = k.shape
    return pl.pallas_call(
        _mla_kernel,
        grid=(B, S // Q_TILE),
        in_specs=[
            pl.BlockSpec((None, None, W, Q_TILE), lambda b, i: (b, i, 0, 0)),
            pl.BlockSpec((None, S, W), lambda b, i: (b, 0, 0)),
            pl.BlockSpec((None, S // KV_TILE, MLA_HEADS * V_ROWS, KV_TILE), lambda b, i: (b, 0, 0, 0)),
        ],
        out_specs=pl.BlockSpec((None, Q_TILE, MLA_HEADS * V_DIM), lambda b, i: (b, i, 0)),
        out_shape=jax.ShapeDtypeStruct((B, S, MLA_HEADS * V_DIM), BF16),
        scratch_shapes=[pltpu.VMEM((MLA_HEADS, 1, Q_TILE), F32),
                        pltpu.VMEM((MLA_HEADS, V_ROWS, Q_TILE), F32)],
        compiler_params=pltpu.CompilerParams(
            dimension_semantics=("arbitrary", "arbitrary"), vmem_limit_bytes=VMEM_LIMIT),
        name="mla_attn",
    )(qT, k, vT)


def _route(logits_t, bias_t, tri_upper, tri_lower):
    tm = logits_t.shape[1]
    neg = -jnp.inf
    rg = lax.broadcasted_iota(jnp.int32, (8, tm), 0)
    re = lax.broadcasted_iota(jnp.int32, (N_EXPERTS, tm), 0)
    top = lambda v: jnp.max(v, axis=0, keepdims=True)

    lg = jnp.where(rg < N_GROUPS, logits_t[LANES:LANES + 8, :] + bias_t[LANES:LANES + 8, :], neg)
    ge = jnp.exp(lg - top(lg))
    gp = ge / jnp.sum(ge, axis=0, keepdims=True)
    g_w = top(gp)
    g_idx = jnp.min(jnp.where(gp == g_w, rg, 8), axis=0, keepdims=True)

    sel = re // EXPERTS_PER_GROUP == g_idx
    le = jnp.where(sel, logits_t[:N_EXPERTS, :] + bias_t[:N_EXPERTS, :], neg)
    ee = jnp.exp(le - top(le))
    ep = jnp.where(sel, ee / jnp.sum(ee, axis=0, keepdims=True), -1.0)
    w1 = top(ep)
    i1 = jnp.min(jnp.where(ep == w1, re, N_EXPERTS), axis=0, keepdims=True)
    ep2 = jnp.where(re == i1, -1.0, ep)
    w2 = top(ep2)
    i2 = jnp.min(jnp.where(ep2 == w2, re, N_EXPERTS), axis=0, keepdims=True)
    den = w1 + w2
    c1 = g_w * (w1 / den)
    c2 = g_w * (w2 / den)

    oh1 = (re == i1).astype(F32)
    oh2 = (re == i2).astype(F32)
    both = oh1 + oh2
    earlier = _dot(both.astype(BF16), tri_upper)
    pieces = jnp.floor((jnp.sum(both, axis=1, keepdims=True) + (PIECE - 1)) * (1.0 / PIECE))
    start = _dot(tri_lower, jnp.broadcast_to(pieces, (N_EXPERTS, LANES)).astype(BF16))[:, 0:1] * PIECE
    pos1 = jnp.sum(oh1 * (earlier + start), axis=0, keepdims=True)
    pos2 = jnp.sum(oh2 * (earlier + start), axis=0, keepdims=True)
    info_t = jnp.concatenate([pos1, pos2, c1, c2, jnp.zeros((4, tm), F32)], axis=0)
    return info_t, pieces


def _mix_kernel(x_ref, attn_ref, ga_ref, gyb_ref, kv_ref, wab_ref, wmix_ref, xg_ref, wxq_ref, wxo_ref,
                fg_ref, wr_hi_ref, wr_lo_ref, rb_ref, tri_ref, lower_ref,
                h_out, xn_out, info_out, infoT_out, pieces_out, logits_ref):
    tm = x_ref.shape[0]

    @pl.when(pl.program_id(0) == 0)
    def _():
        logits_ref[...] = jnp.zeros_like(logits_ref)

    info_t, pieces = _route(logits_ref[...].T, rb_ref[...], tri_ref[...], lower_ref[...])
    infoT_out[...] = info_t
    info_out[...] = jnp.concatenate([info_t, jnp.zeros((LANES - 8, tm), F32)], axis=0).T
    pieces_out[...] = jnp.broadcast_to(pieces, (N_EXPERTS, LANES))

    blocks = [slice(b * MIX_SUB, (b + 1) * MIX_SUB) for b in range(tm // MIX_SUB)]
    y_a = [_dot(attn_ref[r, :], wab_ref[...]) for r in blocks]
    merged = [(ga_ref[r, :].astype(F32) * y + gyb_ref[r, :].astype(F32)).astype(BF16) for r, y in zip(blocks, y_a)]
    h1 = [x_ref[r, :] + _dot(m, wmix_ref[...]) for r, m in zip(blocks, merged)]

    hn = [_rms(h, xg_ref[...]).astype(BF16) for h in h1]
    q = [_dot(v, wxq_ref[...]).astype(BF16) for v in hn]
    pairs = [(b, h) for h in range(MEM_HEADS) for b in range(len(blocks))]
    hs = lambda h: slice(h * MEM_HEAD_DIM, (h + 1) * MEM_HEAD_DIM)
    vs = lambda h: slice(D_MODEL + h * MEM_HEAD_DIM, D_MODEL + (h + 1) * MEM_HEAD_DIM)
    s = [_dot_nt(q[b][:, hs(h)], kv_ref[:, hs(h)]) for b, h in pairs]
    p = [jnp.exp(v - jnp.max(v, axis=-1, keepdims=True)) for v in s]
    o = [_dot(v.astype(BF16), kv_ref[:, vs(h)]) for v, (b, h) in zip(p, pairs)]
    heads = [[None] * MEM_HEADS for _ in blocks]
    for (b, h), ov, pv in zip(pairs, o, p):
        heads[b][h] = (ov / jnp.sum(pv, axis=-1, keepdims=True)).astype(BF16)
    h2 = [h + _dot(jnp.concatenate(hd, axis=1), wxo_ref[...]) for h, hd in zip(h1, heads)]

    xn = [_rms(h, fg_ref[...]) for h in h2]
    xn_hi = [v.astype(BF16) for v in xn]
    xn_lo = [(v - hi.astype(F32)).astype(BF16) for v, hi in zip(xn, xn_hi)]
    logits = [_dot(hi, wr_hi_ref[...]) + (_dot(hi, wr_lo_ref[...]) + _dot(lo, wr_hi_ref[...]))
              for hi, lo in zip(xn_hi, xn_lo)]
    for r, h, hi in zip(blocks, h2, xn_hi):
        h_out[r, :] = h
        xn_out[r, :] = hi
    for r, v in zip(blocks, logits):
        logits_ref[r, :] = v


def _mix_xattn(x, attn, ga, gyb, memkv, wab, wmix, xg, wxq, wxo, fg, wr_hi, wr_lo, rb):
    B, S, _ = x.shape
    M = memkv.shape[1]
    tm = MOE_CHUNK
    nt = S // tm
    n_tiles = B * nt
    cur = lambda t: jnp.minimum(t, n_tiles - 1)
    prev = lambda t: jnp.maximum(t - 1, 0)
    row = lambda t: (cur(t) // nt, cur(t) % nt, 0)
    const2 = lambda t: (0, 0)
    tri = (lax.broadcasted_iota(jnp.int32, (tm, tm), 0) < lax.broadcasted_iota(jnp.int32, (tm, tm), 1)).astype(BF16)
    lower = (lax.broadcasted_iota(jnp.int32, (N_EXPERTS, N_EXPERTS), 1)
             < lax.broadcasted_iota(jnp.int32, (N_EXPERTS, N_EXPERTS), 0)).astype(BF16)
    rb = jnp.broadcast_to(rb.reshape(2 * LANES, 1), (2 * LANES, tm))
    return pl.pallas_call(
        _mix_kernel,
        grid=(n_tiles + 1,),
        in_specs=[
            pl.BlockSpec((None, tm, D_MODEL), row),
            pl.BlockSpec((None, tm, MLA_HEADS * V_DIM), row),
            pl.BlockSpec((None, tm, D_MODEL), row),
            pl.BlockSpec((None, tm, D_MODEL), row),
            pl.BlockSpec((None, M, 2 * D_MODEL), lambda t: (cur(t) // nt, 0, 0)),
            pl.BlockSpec(wab.shape, const2),
            pl.BlockSpec(wmix.shape, const2),
            pl.BlockSpec((1, D_MODEL), const2),
            pl.BlockSpec(wxq.shape, const2),
            pl.BlockSpec(wxo.shape, const2),
            pl.BlockSpec((1, D_MODEL), const2),
            pl.BlockSpec(wr_hi.shape, const2),
            pl.BlockSpec(wr_lo.shape, const2),
            pl.BlockSpec((2 * LANES, tm), const2),
            pl.BlockSpec((tm, tm), const2),
            pl.BlockSpec((N_EXPERTS, N_EXPERTS), const2),
        ],
        out_specs=[
            pl.BlockSpec((None, tm, D_MODEL), row),
            pl.BlockSpec((None, tm, D_MODEL), row),
            pl.BlockSpec((None, tm, LANES), lambda t: (prev(t) // nt, prev(t) % nt, 0)),
            pl.BlockSpec((8, tm), lambda t: (0, prev(t))),
            pl.BlockSpec((None, N_EXPERTS, LANES), lambda t: (prev(t), 0, 0)),
        ],
        out_shape=[jax.ShapeDtypeStruct((B, S, D_MODEL), F32),
                   jax.ShapeDtypeStruct((B, S, D_MODEL), BF16),
                   jax.ShapeDtypeStruct((B, S, LANES), F32),
                   jax.ShapeDtypeStruct((8, B * S), F32),
                   jax.ShapeDtypeStruct((B * nt, N_EXPERTS, LANES), F32)],
        scratch_shapes=[pltpu.VMEM((tm, 2 * LANES), F32)],
        compiler_params=pltpu.CompilerParams(
            dimension_semantics=("arbitrary",), vmem_limit_bytes=VMEM_LIMIT),
        name="mix_xattn",
    )(x, attn, ga, gyb, memkv, wab, wmix, xg, wxq, wxo, fg, wr_hi, wr_lo, rb, tri, lower)


def _piece_copy(src_ref, dst_ref, sem):
    return pltpu.make_async_copy(src_ref, dst_ref, sem)


def _for_segments(seg_ref, chunk, fn):
    base = chunk * SEG_STRIDE

    def body(k, carry):
        at = base + 1 + 3 * k
        fn(seg_ref[at], seg_ref[at + 1], seg_ref[at + 2])
        return carry

    lax.fori_loop(0, seg_ref[base], body, 0)


def _dispatch_kernel(seg_ref, np_ref, gap_ref, fill_ref, xn_ref, infoT_ref, xs_hbm, buf_ref, zero_ref, sem_ref):
    c = pl.program_id(0)
    n = pl.num_programs(0)
    slot = c % 2

    tile_pieces = EXPERT_TILE // PIECE
    n_tiles = xs_hbm.shape[0] // tile_pieces

    def gap_copy(g):
        return _piece_copy(zero_ref.at[0], xs_hbm.at[gap_ref[g]], sem_ref.at[2])

    def tail_copy(t):
        return _piece_copy(zero_ref, xs_hbm.at[pl.ds(t * tile_pieces, tile_pieces)], sem_ref.at[2])

    @pl.when(c == 0)
    def _():
        zero_ref[...] = jnp.zeros_like(zero_ref)
        lax.fori_loop(0, fill_ref[0], lambda g, carry: (gap_copy(g).start(), carry)[1], 0)
        lax.fori_loop(fill_ref[1], n_tiles, lambda t, carry: (tail_copy(t).start(), carry)[1], 0)

    def start_all(cc, s):
        _for_segments(seg_ref, cc, lambda lo, hi, cnt: _piece_copy(
            buf_ref.at[s, pl.ds(lo, cnt)], xs_hbm.at[pl.ds(hi, cnt)], sem_ref.at[s]).start())

    def wait_all(cc, s):
        n_pieces = np_ref[cc]

        @pl.when(n_pieces > 0)
        def _():
            _piece_copy(buf_ref.at[s, pl.ds(0, n_pieces)], xs_hbm.at[pl.ds(0, n_pieces)], sem_ref.at[s]).wait()

    @pl.when(c >= 2)
    def _():
        wait_all(c - 2, slot)

    pos1 = infoT_ref[0:1, :]
    pos2 = infoT_ref[1:2, :]
    r = lax.broadcasted_iota(jnp.int32, (CHUNK_ROWS, MOE_CHUNK), 0).astype(F32)
    onehot = jnp.where((r == pos1) | (r == pos2), 1.0, 0.0).astype(BF16)
    buf_ref[slot] = _dot(onehot, xn_ref[...]).astype(BF16).reshape(MAX_PIECES, PIECE, D_MODEL)
    start_all(c, slot)

    @pl.when(c == n - 1)
    def _():
        @pl.when(c >= 1)
        def _():
            wait_all(c - 1, 1 - slot)
        wait_all(c, slot)
        lax.fori_loop(0, fill_ref[0], lambda g, carry: (gap_copy(g).wait(), carry)[1], 0)
        lax.fori_loop(fill_ref[1], n_tiles, lambda t, carry: (tail_copy(t).wait(), carry)[1], 0)


def _dispatch(xn, infoT, seg, npc, gaps, fill, rows_max):
    T = xn.shape[0]
    grid_spec = pltpu.PrefetchScalarGridSpec(
        num_scalar_prefetch=4,
        grid=(T // MOE_CHUNK,),
        in_specs=[
            pl.BlockSpec((MOE_CHUNK, D_MODEL), lambda c, *_: (c, 0)),
            pl.BlockSpec((8, MOE_CHUNK), lambda c, *_: (0, c)),
        ],
        out_specs=pl.BlockSpec(memory_space=pl.ANY),
        scratch_shapes=[pltpu.VMEM((2, MAX_PIECES, PIECE, D_MODEL), BF16),
                        pltpu.VMEM((EXPERT_TILE // PIECE, PIECE, D_MODEL), BF16),
                        pltpu.SemaphoreType.DMA((3,))],
    )
    xs = pl.pallas_call(
        _dispatch_kernel,
        grid_spec=grid_spec,
        out_shape=jax.ShapeDtypeStruct((rows_max // PIECE, PIECE, D_MODEL), BF16),
        compiler_params=pltpu.CompilerParams(
            dimension_semantics=("arbitrary",), vmem_limit_bytes=VMEM_LIMIT),
        name="moe_dispatch",
    )(seg, npc, gaps, fill, xn, infoT)
    return xs.reshape(rows_max, D_MODEL)


def _expert_kernel(te_ref, tv_ref, nu_ref, x_ref, wg_ref, wu_ref, wd_ref, y_ref, wgu_bf, wd_bf):
    i = pl.program_id(0)

    @pl.when(i < nu_ref[0])
    def _():
        @pl.when((i == 0) | (te_ref[i] != te_ref[jnp.maximum(i - 1, 0)]))
        def _():
            wgu_bf[:, :EXPERT_FF] = wg_ref[...].astype(BF16)
            wgu_bf[:, EXPERT_FF:] = wu_ref[...].astype(BF16)
            wd_bf[...] = wd_ref[...].astype(BF16)

        blocks = [slice(b * EXPERT_SUB, (b + 1) * EXPERT_SUB) for b in range(EXPERT_TILE // EXPERT_SUB)]
        row = lax.broadcasted_iota(jnp.int32, (EXPERT_SUB, D_MODEL), 0)
        gus = []
        for b, rows in enumerate(blocks):
            x = x_ref[rows, :]
            x = jnp.where(row < tv_ref[i] - b * EXPERT_SUB, x, jnp.zeros_like(x))
            gus.append(_dot(x, wgu_bf[...]))
        hids = []
        for gu in gus:
            gate = gu[:, :EXPERT_FF]
            hids.append((gate * jax.nn.sigmoid(gate) * gu[:, EXPERT_FF:]).astype(BF16))
        for rows, hid in zip(blocks, hids):
            y_ref[rows, :] = _dot(hid, wd_bf[...]).astype(BF16)

    @pl.when(i >= nu_ref[0])
    def _():
        y_ref[...] = jnp.zeros_like(y_ref)


def _experts(xs, w_gate, w_up, w_down, tile_expert, tile_valid, n_used):
    rows_max = xs.shape[0]
    last = lambda i, nu: jnp.minimum(i, nu[0] - 1)
    expert = lambda i, te, tv, nu: (te[last(i, nu)], 0, 0)
    grid_spec = pltpu.PrefetchScalarGridSpec(
        num_scalar_prefetch=3,
        grid=(rows_max // EXPERT_TILE,),
        in_specs=[
            pl.BlockSpec((EXPERT_TILE, D_MODEL), lambda i, te, tv, nu: (last(i, nu), 0)),
            pl.BlockSpec((None, D_MODEL, EXPERT_FF), expert),
            pl.BlockSpec((None, D_MODEL, EXPERT_FF), expert),
            pl.BlockSpec((None, EXPERT_FF, D_MODEL), expert),
        ],
        out_specs=pl.BlockSpec((EXPERT_TILE, D_MODEL), lambda i, te, tv, nu: (i, 0)),
        scratch_shapes=[pltpu.VMEM((D_MODEL, 2 * EXPERT_FF), BF16), pltpu.VMEM((EXPERT_FF, D_MODEL), BF16)],
    )
    return pl.pallas_call(
        _expert_kernel,
        grid_spec=grid_spec,
        out_shape=jax.ShapeDtypeStruct((rows_max, D_MODEL), BF16),
        compiler_params=pltpu.CompilerParams(
            dimension_semantics=("arbitrary",), vmem_limit_bytes=VMEM_LIMIT),
        name="moe_experts",
    )(tile_expert, tile_valid, n_used, xs, w_gate, w_up, w_down)


def _combine_kernel(seg_ref, np_ref, h_ref, info_ref, fg_ref, ys_hbm, o_ref, buf_ref, sem_ref):
    c = pl.program_id(0)
    n = pl.num_programs(0)
    slot = c % 2

    def start_all(cc, s):
        _for_segments(seg_ref, cc, lambda lo, hi, cnt: _piece_copy(
            ys_hbm.at[pl.ds(hi, cnt)], buf_ref.at[s, pl.ds(lo, cnt)], sem_ref.at[s]).start())

    def wait_all(cc, s):
        n_pieces = np_ref[cc]

        @pl.when(n_pieces > 0)
        def _():
            _piece_copy(ys_hbm.at[pl.ds(0, n_pieces)], buf_ref.at[s, pl.ds(0, n_pieces)], sem_ref.at[s]).wait()

    @pl.when(c == 0)
    def _():
        buf_ref[...] = jnp.zeros_like(buf_ref)
        start_all(0, 0)

    @pl.when(c + 1 < n)
    def _():
        start_all(c + 1, 1 - slot)

    wait_all(c, slot)
    info = info_ref[...]
    r = lax.broadcasted_iota(jnp.int32, (COMBINE_SUB, CHUNK_ROWS), 1).astype(F32)
    sorted_rows = buf_ref[slot].reshape(CHUNK_ROWS, D_MODEL)
    blocks = [slice(b * COMBINE_SUB, (b + 1) * COMBINE_SUB) for b in range(MOE_CHUNK // COMBINE_SUB)]
    moe = []
    for t in blocks:
        weights = (jnp.where(r == info[t, 0:1], info[t, 2:3], 0.0) + jnp.where(r == info[t, 1:2], info[t, 3:4], 0.0))
        moe.append(_dot(weights.astype(BF16), sorted_rows))
    for t, m in zip(blocks, moe):
        o_ref[t, :] = _rms(h_ref[t, :] + m, fg_ref[...])


def _combine(h2, info, fg, ys, seg, npc):
    T = h2.shape[0]
    grid_spec = pltpu.PrefetchScalarGridSpec(
        num_scalar_prefetch=2,
        grid=(T // MOE_CHUNK,),
        in_specs=[
            pl.BlockSpec((MOE_CHUNK, D_MODEL), lambda c, seg, npc: (c, 0)),
            pl.BlockSpec((MOE_CHUNK, LANES), lambda c, seg, npc: (c, 0)),
            pl.BlockSpec((1, D_MODEL), lambda c, seg, npc: (0, 0)),
            pl.BlockSpec(memory_space=pl.ANY),
        ],
        out_specs=pl.BlockSpec((MOE_CHUNK, D_MODEL), lambda c, seg, npc: (c, 0)),
        scratch_shapes=[pltpu.VMEM((2, MAX_PIECES, PIECE, D_MODEL), BF16), pltpu.SemaphoreType.DMA((2,))],
    )
    ys = ys.reshape(-1, PIECE, D_MODEL)
    return pl.pallas_call(
        _combine_kernel,
        grid_spec=grid_spec,
        out_shape=jax.ShapeDtypeStruct((T, D_MODEL), F32),
        compiler_params=pltpu.CompilerParams(
            dimension_semantics=("arbitrary",), vmem_limit_bytes=VMEM_LIMIT),
        name="moe_combine",
    )(seg, npc, h2, info, fg, ys)


def _routing_tables(pieces, rows_max):
    tile_pieces = EXPERT_TILE // PIECE
    total = jnp.sum(pieces, axis=0)
    total_al = (total + tile_pieces - 1) // tile_pieces * tile_pieces
    seg_end = jnp.cumsum(total_al)
    seg_start = seg_end - total_al
    chunk_off = jnp.cumsum(pieces, axis=0) - pieces
    loc_end = jnp.cumsum(pieces, axis=1)
    loc_start = loc_end - pieces
    n_local = loc_end[:, -1]
    experts = jnp.arange(N_EXPERTS, dtype=jnp.int32)
    nonempty = pieces > 0
    slot = jnp.cumsum(nonempty.astype(jnp.int32), axis=1) - 1
    pick = (nonempty[:, None, :] & (slot[:, None, :] == experts[None, :, None])).astype(jnp.int32)
    compact = lambda a: jnp.sum(pick * a[:, None, :], axis=-1)
    triples = jnp.stack([compact(loc_start), compact(seg_start[None, :] + chunk_off), compact(pieces)], axis=-1)
    n_seg = jnp.sum(nonempty.astype(jnp.int32), axis=1, keepdims=True)
    seg = jnp.concatenate([n_seg, triples.reshape(pieces.shape[0], 3 * N_EXPERTS)], axis=1)

    t0 = jnp.arange(rows_max // EXPERT_TILE, dtype=jnp.int32) * tile_pieces
    tile_expert = jnp.minimum(jnp.sum((t0[:, None] >= seg_end[None, :]).astype(jnp.int32), axis=-1), N_EXPERTS - 1)
    copies_end = jnp.sum((tile_expert[:, None] == experts).astype(jnp.int32) * (seg_start + total)[None, :], axis=-1)
    tile_valid = jnp.clip((copies_end - t0) * PIECE, 0, EXPERT_TILE)
    n_used = (seg_end[-1] // tile_pieces).reshape(1)

    k = jnp.arange(tile_pieces, dtype=jnp.int32)
    is_gap = (k[None, :] < (total_al - total)[:, None]).reshape(-1)
    gap_piece = (seg_start + total)[:, None] + k[None, :]
    order = jnp.argsort(jnp.logical_not(is_gap), stable=True)
    gaps = gap_piece.reshape(-1)[order]
    fill = jnp.stack([jnp.sum(is_gap.astype(jnp.int32)), n_used[0]])
    i32 = lambda a: a.astype(jnp.int32)
    return i32(seg.reshape(-1)), i32(n_local), i32(tile_expert), i32(tile_valid), i32(n_used), i32(gaps), i32(fill)


def _rope_tables(positions):
    inv_freq = 1.0 / (ROPE_THETA ** (jnp.arange(0, ROPE_DIM, 2, dtype=F32) / ROPE_DIM))
    ang = positions.astype(F32)[:, None, :] * inv_freq[None, :, None]
    cos, sin = jnp.cos(ang), jnp.sin(ang)
    return jnp.concatenate([cos, sin], axis=1)


def _pad_heads(w, heads, width):
    k = w.shape[0]
    w = w.reshape(k, heads, width)
    w = jnp.pad(w, ((0, 0), (0, 0), (0, HEAD_PAD - width)))
    return w.reshape(k, heads * HEAD_PAD)


def _layer(l, h, mem, tables, mix_norm_g, w_in, q_norm_g, w_q_up, kv_norm_g, w_kv_up, w_attn_branch,
           pool_w, pool_scale, w_pool_branch, w_mix_out, xattn_norm_g, mem_norm_g, w_xq, w_xkv, w_xo,
           ffn_norm_g, w_router_group, b_router_group, w_router_expert, b_router_expert,
           w_exp_gate, w_exp_up, w_exp_down, out_g, tm_proj):
    B, S, _ = h.shape
    row2 = lambda v: v.reshape(1, -1).astype(F32)

    wi = w_in[l]
    kr_cols = jnp.pad(wi[:, Q_LORA + KV_LORA:Q_LORA + KV_LORA + ROPE_DIM],
                      ((0, 0), (NOPE_DIM, LANES - NOPE_DIM - ROPE_DIM)))
    win = jnp.concatenate([wi[:, :Q_LORA + KV_LORA], kr_cols, wi[:, Q_LORA + KV_LORA + ROPE_DIM:]], axis=1).astype(BF16)
    scale = math.log2(math.e) / math.sqrt(NOPE_DIM + ROPE_DIM)
    wq3 = (w_q_up[l] * scale).reshape(Q_LORA, MLA_HEADS, NOPE_DIM + ROPE_DIM)
    half = ROPE_DIM // 2
    wq = jnp.concatenate([wq3[:, :, :NOPE_DIM].reshape(Q_LORA, -1),
                          wq3[:, :, NOPE_DIM:NOPE_DIM + half].reshape(Q_LORA, -1),
                          wq3[:, :, NOPE_DIM + half:].reshape(Q_LORA, -1)], axis=1).astype(BF16)
    wkv3 = w_kv_up[l].reshape(KV_LORA, MLA_HEADS, NOPE_DIM + V_DIM)
    wkv = jnp.concatenate([
        _pad_heads(wkv3[:, :, :NOPE_DIM].reshape(KV_LORA, -1), MLA_HEADS, NOPE_DIM),
        wkv3[:, :, NOPE_DIM:].reshape(KV_LORA, -1)], axis=1).astype(BF16)

    memkv = _mem_kv(mem.reshape(-1, D_MODEL), row2(mem_norm_g[l]), w_xkv[l].astype(BF16))
    memkv = memkv.reshape(B, -1, 2 * D_MODEL)

    qT, k, vT, ga, gyb = _in_proj(
        h, tables, row2(mix_norm_g[l]), win, row2(q_norm_g[l]), wq, row2(kv_norm_g[l]), wkv,
        pool_w[l].astype(BF16), row2(pool_scale[l]), w_pool_branch[l].astype(BF16), tm_proj)
    attn = _mla_attention(qT, k, vT)

    w_r = jnp.zeros((D_MODEL, 2 * LANES), F32)
    w_r = w_r.at[:, :N_EXPERTS].set(w_router_expert[l]).at[:, LANES:LANES + N_GROUPS].set(w_router_group[l])
    wr_hi = w_r.astype(BF16)
    wr_lo = (w_r - wr_hi.astype(F32)).astype(BF16)
    rb = jnp.zeros((1, 2 * LANES), F32)
    rb = rb.at[0, :N_EXPERTS].set(b_router_expert[l]).at[0, LANES:LANES + N_GROUPS].set(b_router_group[l])

    h2, xn, info, infoT, pieces = _mix_xattn(
        h, attn, ga, gyb, memkv, w_attn_branch[l].astype(BF16), w_mix_out[l].astype(BF16),
        row2(xattn_norm_g[l]), (w_xq[l] * (1.0 / math.sqrt(MEM_HEAD_DIM))).astype(BF16), w_xo[l].astype(BF16),
        row2(ffn_norm_g[l]), wr_hi, wr_lo, rb)

    T = B * S
    n_chunks = T // MOE_CHUNK
    tile_pieces = EXPERT_TILE // PIECE
    max_pieces = 2 * T // PIECE + n_chunks * N_EXPERTS + N_EXPERTS * tile_pieces
    rows_max = -(-max_pieces // tile_pieces) * EXPERT_TILE
    seg, n_local, tile_expert, tile_valid, n_used, gaps, fill = _routing_tables(
        pieces[:, :, 0].astype(jnp.int32), rows_max)

    xs = _dispatch(xn.reshape(T, D_MODEL), infoT, seg, n_local, gaps, fill, rows_max)
    ys = _experts(xs, w_exp_gate[l], w_exp_up[l], w_exp_down[l], tile_expert, tile_valid, n_used)
    out = _combine(h2.reshape(T, D_MODEL), info.reshape(T, LANES), row2(out_g), ys, seg, n_local)
    return out.reshape(B, S, D_MODEL)


def kernel(x, mem, positions, mix_norm_g, w_in, q_norm_g, w_q_up, kv_norm_g, w_kv_up, w_attn_branch, pool_w, pool_scale, w_pool_branch, w_mix_out, xattn_norm_g, mem_norm_g, w_xq, w_xkv, w_xo, ffn_norm_g, w_router_group, b_router_group, w_router_expert, b_router_expert, w_exp_gate, w_exp_up, w_exp_down, final_norm_g):
    depth = w_in.shape[0]
    assert depth == 1, "the combine kernel fuses the final RMSNorm, which is only valid after the last layer"
    assert x.shape[1] % Q_TILE == 0 and x.shape[1] % MOE_CHUNK == 0
    tables = _rope_tables(positions)
    return _layer(0, x, mem, tables, mix_norm_g, w_in, q_norm_g, w_q_up, kv_norm_g, w_kv_up, w_attn_branch,
                  pool_w, pool_scale, w_pool_branch, w_mix_out, xattn_norm_g, mem_norm_g, w_xq, w_xkv, w_xo,
                  ffn_norm_g, w_router_group, b_router_group, w_router_expert, b_router_expert,
                  w_exp_gate, w_exp_up, w_exp_down, final_norm_g, IN_PROJ_TILE)
```

```python
import functools
import math

import jax
import jax.numpy as jnp
from jax import lax
from jax.experimental import pallas as pl
from jax.experimental.pallas import tpu as pltpu

F32 = jnp.float32
BF16 = jnp.bfloat16

D_MODEL = 1024
CHUNK = 64
MLA_HEADS = 8
Q_LORA = 384
KV_LORA = 256
NOPE_DIM = 64
ROPE_DIM = 32
V_DIM = 64
ROPE_THETA = 10000.0
POOL_WIDTH = 512
POOL_WINDOWS = (2, 4, 8, 16)
POOL_GROUP_DIM = POOL_WIDTH // len(POOL_WINDOWS)
POOL_HALO = 16
MEM_HEADS = 4
MEM_HEAD_DIM = D_MODEL // MEM_HEADS
N_GROUPS = 4
EXPERTS_PER_GROUP = 8
N_EXPERTS = N_GROUPS * EXPERTS_PER_GROUP
EXPERT_FF = 256
EPS = 1e-6

LANES = 128
HEAD_PAD = LANES
Q_TILE = 512
KV_TILE = 512
KEY_SUB = 128
SCORE_SUB = 128
V_ROWS = V_DIM + 16
IN_PROJ_TILE = 512
MOE_CHUNK = 512
PIECE = 16
MAX_PIECES = 2 * MOE_CHUNK // PIECE + N_EXPERTS
COMBINE_SUB = 256
SEG_STRIDE = 1 + 3 * N_EXPERTS
CHUNK_ROWS = MAX_PIECES * PIECE
EXPERT_TILE = 1024
EXPERT_SUB = 256
PROJ_SUB = 256
MIX_SUB = 256
QK_AHEAD = 3
PV_BEHIND = 2

_C_Q = 0
_C_KV = _C_Q + Q_LORA
_C_KR = _C_KV + KV_LORA
_C_POOL = _C_KR + LANES
_C_GA = _C_POOL + POOL_WIDTH
_C_GB = _C_GA + D_MODEL
_C_END = _C_GB + D_MODEL

VMEM_LIMIT = 56 * 1024 * 1024


def _rms(x, g):
    return x * lax.rsqrt(jnp.mean(x * x, axis=-1, keepdims=True) + EPS) * g


def _dot(a, b):
    return jnp.dot(a, b, preferred_element_type=F32)


def _dot_nt(a, b):
    return lax.dot_general(a, b, (((1,), (1,)), ((), ())), preferred_element_type=F32)


def _mem_kv_kernel(mem_ref, g_ref, w_ref, kv_ref):
    mn = _rms(mem_ref[...], g_ref[...]).astype(BF16)
    kv_ref[...] = _dot(mn, w_ref[...]).astype(BF16)


def _mem_kv(mem2d, g, w_xkv):
    rows = mem2d.shape[0]
    tm = min(512, rows)
    assert rows % tm == 0
    return pl.pallas_call(
        _mem_kv_kernel,
        grid=(rows // tm,),
        in_specs=[
            pl.BlockSpec((tm, D_MODEL), lambda i: (i, 0)),
            pl.BlockSpec((1, D_MODEL), lambda i: (0, 0)),
            pl.BlockSpec((D_MODEL, 2 * D_MODEL), lambda i: (0, 0)),
        ],
        out_specs=pl.BlockSpec((tm, 2 * D_MODEL), lambda i: (i, 0)),
        out_shape=jax.ShapeDtypeStruct((rows, 2 * D_MODEL), BF16),
        compiler_params=pltpu.CompilerParams(vmem_limit_bytes=VMEM_LIMIT),
        name="mem_kv",
    )(mem2d, g, w_xkv)


def _rope(t, c, sa, sb):
    w = t.shape[-1]
    return t * c + pltpu.roll(t, ROPE_DIM // 2, 1) * sa + pltpu.roll(t, w - ROPE_DIM // 2, 1) * sb


def _in_proj_kernel(x_ref, rot_ref, g_ref, win_ref, qg_ref, wq_ref, kvg_ref, wkv_ref,
                    poolw_ref, pscale_ref, wpb_ref,
                    qT_out, k_out, vT_out, ga_out, gyb_out, hist_ref):
    tm = x_ref.shape[0]
    i = pl.program_id(1)
    blocks = [slice(b * PROJ_SUB, (b + 1) * PROJ_SUB) for b in range(tm // PROJ_SUB)]
    rows = lambda parts: jnp.concatenate(parts, axis=0)
    hn = [_rms(x_ref[r, :], g_ref[...]).astype(BF16) for r in blocks]
    proj = lambda lo, hi: [_dot(v, win_ref[:, lo:hi]) for v in hn]

    half = ROPE_DIM // 2
    cos8 = jnp.concatenate([rot_ref[0:half, :]] * MLA_HEADS, axis=0).T
    sin8 = jnp.concatenate([rot_ref[half:, :]] * MLA_HEADS, axis=0).T
    lane = lax.broadcasted_iota(jnp.int32, cos8.shape, 1)
    rope_lo = NOPE_DIM
    c1 = jnp.where(lane < rope_lo + ROPE_DIM, cos8, 0.0)
    sa1 = jnp.where((lane >= rope_lo + half) & (lane < rope_lo + ROPE_DIM), sin8, 0.0)
    sb1 = jnp.where((lane >= rope_lo) & (lane < rope_lo + half), -sin8, 0.0)

    q_lat = proj(_C_Q, _C_KV)
    kv_lat = proj(_C_KV, _C_KR)
    qn = [_rms(v, qg_ref[...]).astype(BF16) for v in q_lat]
    kvn = [_rms(v, kvg_ref[...]).astype(BF16) for v in kv_lat]
    q = rows([_dot(v, wq_ref[...]) for v in qn])
    n_nope = MLA_HEADS * NOPE_DIM
    x1 = q[:, n_nope:n_nope + LANES]
    x2 = q[:, n_nope + LANES:]
    qT = jnp.concatenate([q[:, :n_nope], x1 * cos8 - x2 * sin8, x2 * cos8 + x1 * sin8], axis=1).T
    pad = jnp.zeros((HEAD_PAD - NOPE_DIM - ROPE_DIM, tm), F32)
    qT = jnp.concatenate(
        [blk for h in range(MLA_HEADS) for blk in (
            qT[h * NOPE_DIM:(h + 1) * NOPE_DIM, :],
            qT[n_nope + h * half:n_nope + (h + 1) * half, :],
            qT[n_nope + LANES + h * half:n_nope + LANES + (h + 1) * half, :], pad)], axis=0).astype(BF16)

    k_nope = rows([_dot(v, wkv_ref[:, 0:MLA_HEADS * HEAD_PAD]) for v in kvn])
    kr = _rope(rows(proj(_C_KR, _C_POOL)), c1, sa1, sb1)
    k_out[...] = (k_nope + jnp.tile(kr, (1, MLA_HEADS))).astype(BF16)
    v = rows([_dot(t, wkv_ref[:, MLA_HEADS * HEAD_PAD:]) for t in kvn])
    for t in range(tm // Q_TILE):
        qT_out[t] = qT[:, t * Q_TILE:(t + 1) * Q_TILE]
    vT = v.T
    ones = jnp.ones((V_ROWS - V_DIM, tm), F32)
    vT = jnp.concatenate(
        [blk for h in range(MLA_HEADS) for blk in (vT[h * V_DIM:(h + 1) * V_DIM, :], ones)], axis=0).astype(BF16)
    for t in range(tm // KV_TILE):
        vT_out[t] = vT[:, t * KV_TILE:(t + 1) * KV_TILE]

    u = rows(proj(_C_POOL, _C_GA))

    @pl.when(i == 0)
    def _():
        hist_ref[...] = jnp.zeros_like(hist_ref)

    ext = jnp.concatenate([hist_ref[...], u], axis=0)
    hist_ref[...] = u[tm - POOL_HALO:, :]
    t_idx = i * tm + lax.broadcasted_iota(jnp.int32, (tm, POOL_GROUP_DIM), 0)
    ys = []
    for g, w in enumerate(POOL_WINDOWS):
        c0 = g * POOL_GROUP_DIM
        run = ext[:, c0:c0 + POOL_GROUP_DIM]
        span = 1
        while span < w:
            run = run + pltpu.roll(run, span, 0)
            span *= 2
        cnt = jnp.minimum(t_idx + 1, w).astype(F32)
        d = run[POOL_HALO:, :] / cnt - u[:, c0:c0 + POOL_GROUP_DIM]
        ys.append(_dot(d.astype(BF16), poolw_ref[g]))
    y = (jnp.concatenate(ys, axis=1) * pscale_ref[...]).astype(BF16)
    y_b = [_dot(y[r, :], wpb_ref[...]) for r in blocks]

    for r, t in zip(blocks, proj(_C_GA, _C_GB)):
        ga_out[r, :] = jax.nn.sigmoid(t).astype(BF16)
    for r, t, yb in zip(blocks, proj(_C_GB, _C_END), y_b):
        gyb_out[r, :] = (jax.nn.sigmoid(t) * yb).astype(BF16)


def _in_proj(x, rot, g, win, qg, wq, kvg, wkv, poolw, pscale, wpb, tm):
    B, S, _ = x.shape
    row = lambda b, i: (b, i, 0)
    const2 = lambda b, i: (0, 0)
    const3 = lambda b, i: (0, 0, 0)
    slab = lambda b, i: (b, i, 0, 0)
    return pl.pallas_call(
        _in_proj_kernel,
        grid=(B, S // tm),
        in_specs=[
            pl.BlockSpec((None, tm, D_MODEL), row),
            pl.BlockSpec((None, ROPE_DIM, tm), lambda b, i: (b, 0, i)),
            pl.BlockSpec((1, D_MODEL), const2),
            pl.BlockSpec(win.shape, const2),
            pl.BlockSpec((1, Q_LORA), const2),
            pl.BlockSpec(wq.shape, const2),
            pl.BlockSpec((1, KV_LORA), const2),
            pl.BlockSpec(wkv.shape, const2),
            pl.BlockSpec(poolw.shape, const3),
            pl.BlockSpec((1, POOL_WIDTH), const2),
            pl.BlockSpec(wpb.shape, const2),
        ],
        out_specs=[
            pl.BlockSpec((None, tm // Q_TILE, MLA_HEADS * HEAD_PAD, Q_TILE), slab),
            pl.BlockSpec((None, tm, MLA_HEADS * HEAD_PAD), row),
            pl.BlockSpec((None, tm // KV_TILE, MLA_HEADS * V_ROWS, KV_TILE), slab),
            pl.BlockSpec((None, tm, D_MODEL), row),
            pl.BlockSpec((None, tm, D_MODEL), row),
        ],
        out_shape=[jax.ShapeDtypeStruct((B, S // Q_TILE, MLA_HEADS * HEAD_PAD, Q_TILE), BF16),
                   jax.ShapeDtypeStruct((B, S, MLA_HEADS * HEAD_PAD), BF16),
                   jax.ShapeDtypeStruct((B, S // KV_TILE, MLA_HEADS * V_ROWS, KV_TILE), BF16),
                   jax.ShapeDtypeStruct((B, S, D_MODEL), BF16),
                   jax.ShapeDtypeStruct((B, S, D_MODEL), BF16)],
        scratch_shapes=[pltpu.VMEM((POOL_HALO, POOL_WIDTH), F32)],
        compiler_params=pltpu.CompilerParams(
            dimension_semantics=("arbitrary", "arbitrary"), vmem_limit_bytes=VMEM_LIMIT),
        name="in_proj",
    )(x, rot, g, win, qg, wq, kvg, wkv, poolw, pscale, wpb)


def _mla_kernel(qT_ref, k_ref, vT_ref, o_ref, m_ref, acc_ref):
    i = pl.program_id(1)
    n_sub = KV_TILE // KEY_SUB
    units = [(h, c) for c in range(n_sub) for h in range(MLA_HEADS)]
    qry_c = lax.broadcasted_iota(jnp.int32, (SCORE_SUB, Q_TILE), 1) // CHUNK
    parts = range(KEY_SUB // SCORE_SUB)

    def scores(j, h, c):
        hs = slice(h * HEAD_PAD, (h + 1) * HEAD_PAD)
        out = []
        for a in parts:
            rows = pl.ds(pl.multiple_of(j * KV_TILE + c * KEY_SUB + a * SCORE_SUB, SCORE_SUB), SCORE_SUB)
            out.append(_dot(k_ref[rows, hs], qT_ref[hs, :]))
        return out

    def fold(h, alpha, pv):
        acc_ref[h] = pv if alpha is None else alpha * acc_ref[h] + pv

    def sweep(j, diag):
        ahead = [scores(j, *u) for u in units[:QK_AHEAD]]
        pending = []
        for n, (h, c) in enumerate(units):
            s = ahead.pop(0)
            if n + QK_AHEAD < len(units):
                ahead.append(scores(j, *units[n + QK_AHEAD]))
            first = False
            if diag is not None:
                for a in parts:
                    key_c0 = (diag * KV_TILE + c * KEY_SUB + a * SCORE_SUB) // CHUNK
                    key_c = key_c0 + lax.broadcasted_iota(jnp.int32, (SCORE_SUB, Q_TILE), 0) // CHUNK
                    s[a] = jnp.where(key_c <= qry_c, s[a], -jnp.inf)
                first = diag == 0 and c == 0
            s_max = functools.reduce(jnp.maximum, [jnp.max(v, axis=0, keepdims=True) for v in s])
            m_new = s_max if first else jnp.maximum(m_ref[h], s_max)
            p = jnp.concatenate([jnp.exp2(v - m_new).astype(BF16) for v in s], axis=0)
            pv = _dot(vT_ref[j, h * V_ROWS:(h + 1) * V_ROWS, c * KEY_SUB:(c + 1) * KEY_SUB], p)
            alpha = None if first else jnp.exp2(m_ref[h] - m_new)
            m_ref[h] = m_new
            pending.append((h, alpha, pv))
            if len(pending) > PV_BEHIND:
                fold(*pending.pop(0))
        for item in pending:
            fold(*item)

    n_diag = Q_TILE // KV_TILE
    for d in range(n_diag):
        sweep(i * n_diag + d, d)

    def body(j, carry):
        sweep(j, None)
        return carry

    lax.fori_loop(0, i * n_diag, body, 0)
    oT = jnp.concatenate([acc_ref[h, :V_DIM, :] / acc_ref[h, V_DIM:V_DIM + 1, :] for h in range(MLA_HEADS)], axis=0)
    o_ref[...] = oT.T.astype(BF16)


def _mla_attention(qT, k, vT):
    B, S, W = k.shape
    return pl.pallas_call(
        _mla_kernel,
        grid=(B, S // Q_TILE),
        in_specs=[
            pl.BlockSpec((None, None, W, Q_TILE), lambda b, i: (b, i, 0, 0)),
            pl.BlockSpec((None, S, W), lambda b, i: (b, 0, 0)),
            pl.BlockSpec((None, S // KV_TILE, MLA_HEADS * V_ROWS, KV_TILE), lambda b, i: (b, 0, 0, 0)),
        ],
        out_specs=pl.BlockSpec((None, Q_TILE, MLA_HEADS * V_DIM), lambda b, i: (b, i, 0)),
        out_shape=jax.ShapeDtypeStruct((B, S, MLA_HEADS * V_DIM), BF16),
        scratch_shapes=[pltpu.VMEM((MLA_HEADS, 1, Q_TILE), F32),
                        pltpu.VMEM((MLA_HEADS, V_ROWS, Q_TILE), F32)],
        compiler_params=pltpu.CompilerParams(
            dimension_semantics=("arbitrary", "arbitrary"), vmem_limit_bytes=VMEM_LIMIT),
        name="mla_attn",
    )(qT, k, vT)


def _route(logits_t, bias_t, tri_upper, tri_lower):
    tm = logits_t.shape[1]
    neg = -jnp.inf
    rg = lax.broadcasted_iota(jnp.int32, (8, tm), 0)
    re = lax.broadcasted_iota(jnp.int32, (N_EXPERTS, tm), 0)
    top = lambda v: jnp.max(v, axis=0, keepdims=True)

    lg = jnp.where(rg < N_GROUPS, logits_t[LANES:LANES + 8, :] + bias_t[LANES:LANES + 8, :], neg)
    ge = jnp.exp(lg - top(lg))
    gp = ge / jnp.sum(ge, axis=0, keepdims=True)
    g_w = top(gp)
    g_idx = jnp.min(jnp.where(gp == g_w, rg, 8), axis=0, keepdims=True)

    sel = re // EXPERTS_PER_GROUP == g_idx
    le = jnp.where(sel, logits_t[:N_EXPERTS, :] + bias_t[:N_EXPERTS, :], neg)
    ee = jnp.exp(le - top(le))
    ep = jnp.where(sel, ee / jnp.sum(ee, axis=0, keepdims=True), -1.0)
    w1 = top(ep)
    i1 = jnp.min(jnp.where(ep == w1, re, N_EXPERTS), axis=0, keepdims=True)
    ep2 = jnp.where(re == i1, -1.0, ep)
    w2 = top(ep2)
    i2 = jnp.min(jnp.where(ep2 == w2, re, N_EXPERTS), axis=0, keepdims=True)
    den = w1 + w2
    c1 = g_w * (w1 / den)
    c2 = g_w * (w2 / den)

    oh1 = (re == i1).astype(F32)
    oh2 = (re == i2).astype(F32)
    both = oh1 + oh2
    earlier = _dot(both.astype(BF16), tri_upper)
    pieces = jnp.floor((jnp.sum(both, axis=1, keepdims=True) + (PIECE - 1)) * (1.0 / PIECE))
    start = _dot(tri_lower, jnp.broadcast_to(pieces, (N_EXPERTS, LANES)).astype(BF16))[:, 0:1] * PIECE
    pos1 = jnp.sum(oh1 * (earlier + start), axis=0, keepdims=True)
    pos2 = jnp.sum(oh2 * (earlier + start), axis=0, keepdims=True)
    info_t = jnp.concatenate([pos1, pos2, c1, c2, jnp.zeros((4, tm), F32)], axis=0)
    return info_t, pieces


def _mix_kernel(x_ref, attn_ref, ga_ref, gyb_ref, kv_ref, wab_ref, wmix_ref, xg_ref, wxq_ref, wxo_ref,
                fg_ref, wr_hi_ref, wr_lo_ref, rb_ref, tri_ref, lower_ref,
                h_out, xn_out, info_out, infoT_out, pieces_out, logits_ref):
    tm = x_ref.shape[0]

    @pl.when(pl.program_id(0) == 0)
    def _():
        logits_ref[...] = jnp.zeros_like(logits_ref)

    info_t, pieces = _route(logits_ref[...].T, rb_ref[...], tri_ref[...], lower_ref[...])
    infoT_out[...] = info_t
    info_out[...] = jnp.concatenate([info_t, jnp.zeros((LANES - 8, tm), F32)], axis=0).T
    pieces_out[...] = jnp.broadcast_to(pieces, (N_EXPERTS, LANES))

    blocks = [slice(b * MIX_SUB, (b + 1) * MIX_SUB) for b in range(tm // MIX_SUB)]
    y_a = [_dot(attn_ref[r, :], wab_ref[...]) for r in blocks]
    merged = [(ga_ref[r, :].astype(F32) * y + gyb_ref[r, :].astype(F32)).astype(BF16) for r, y in zip(blocks, y_a)]
    h1 = [x_ref[r, :] + _dot(m, wmix_ref[...]) for r, m in zip(blocks, merged)]

    hn = [_rms(h, xg_ref[...]).astype(BF16) for h in h1]
    q = [_dot(v, wxq_ref[...]).astype(BF16) for v in hn]
    pairs = [(b, h) for h in range(MEM_HEADS) for b in range(len(blocks))]
    hs = lambda h: slice(h * MEM_HEAD_DIM, (h + 1) * MEM_HEAD_DIM)
    vs = lambda h: slice(D_MODEL + h * MEM_HEAD_DIM, D_MODEL + (h + 1) * MEM_HEAD_DIM)
    s = [_dot_nt(q[b][:, hs(h)], kv_ref[:, hs(h)]) for b, h in pairs]
    p = [jnp.exp(v - jnp.max(v, axis=-1, keepdims=True)) for v in s]
    o = [_dot(v.astype(BF16), kv_ref[:, vs(h)]) for v, (b, h) in zip(p, pairs)]
    heads = [[None] * MEM_HEADS for _ in blocks]
    for (b, h), ov, pv in zip(pairs, o, p):
        heads[b][h] = (ov / jnp.sum(pv, axis=-1, keepdims=True)).astype(BF16)
    h2 = [h + _dot(jnp.concatenate(hd, axis=1), wxo_ref[...]) for h, hd in zip(h1, heads)]

    xn = [_rms(h, fg_ref[...]) for h in h2]
    xn_hi = [v.astype(BF16) for v in xn]
    xn_lo = [(v - hi.astype(F32)).astype(BF16) for v, hi in zip(xn, xn_hi)]
    logits = [_dot(hi, wr_hi_ref[...]) + (_dot(hi, wr_lo_ref[...]) + _dot(lo, wr_hi_ref[...]))
              for hi, lo in zip(xn_hi, xn_lo)]
    for r, h, hi in zip(blocks, h2, xn_hi):
        h_out[r, :] = h
        xn_out[r, :] = hi
    for r, v in zip(blocks, logits):
        logits_ref[r, :] = v


def _mix_xattn(x, attn, ga, gyb, memkv, wab, wmix, xg, wxq, wxo, fg, wr_hi, wr_lo, rb):
    B, S, _ = x.shape
    M = memkv.shape[1]
    tm = MOE_CHUNK
    nt = S // tm
    n_tiles = B * nt
    cur = lambda t: jnp.minimum(t, n_tiles - 1)
    prev = lambda t: jnp.maximum(t - 1, 0)
    row = lambda t: (cur(t) // nt, cur(t) % nt, 0)
    const2 = lambda t: (0, 0)
    tri = (lax.broadcasted_iota(jnp.int32, (tm, tm), 0) < lax.broadcasted_iota(jnp.int32, (tm, tm), 1)).astype(BF16)
    lower = (lax.broadcasted_iota(jnp.int32, (N_EXPERTS, N_EXPERTS), 1)
             < lax.broadcasted_iota(jnp.int32, (N_EXPERTS, N_EXPERTS), 0)).astype(BF16)
    rb = jnp.broadcast_to(rb.reshape(2 * LANES, 1), (2 * LANES, tm))
    return pl.pallas_call(
        _mix_kernel,
        grid=(n_tiles + 1,),
        in_specs=[
            pl.BlockSpec((None, tm, D_MODEL), row),
            pl.BlockSpec((None, tm, MLA_HEADS * V_DIM), row),
            pl.BlockSpec((None, tm, D_MODEL), row),
            pl.BlockSpec((None, tm, D_MODEL), row),
            pl.BlockSpec((None, M, 2 * D_MODEL), lambda t: (cur(t) // nt, 0, 0)),
            pl.BlockSpec(wab.shape, const2),
            pl.BlockSpec(wmix.shape, const2),
            pl.BlockSpec((1, D_MODEL), const2),
            pl.BlockSpec(wxq.shape, const2),
            pl.BlockSpec(wxo.shape, const2),
            pl.BlockSpec((1, D_MODEL), const2),
            pl.BlockSpec(wr_hi.shape, const2),
            pl.BlockSpec(wr_lo.shape, const2),
            pl.BlockSpec((2 * LANES, tm), const2),
            pl.BlockSpec((tm, tm), const2),
            pl.BlockSpec((N_EXPERTS, N_EXPERTS), const2),
        ],
        out_specs=[
            pl.BlockSpec((None, tm, D_MODEL), row),
            pl.BlockSpec((None, tm, D_MODEL), row),
            pl.BlockSpec((None, tm, LANES), lambda t: (prev(t) // nt, prev(t) % nt, 0)),
            pl.BlockSpec((8, tm), lambda t: (0, prev(t))),
            pl.BlockSpec((None, N_EXPERTS, LANES), lambda t: (prev(t), 0, 0)),
        ],
        out_shape=[jax.ShapeDtypeStruct((B, S, D_MODEL), F32),
                   jax.ShapeDtypeStruct((B, S, D_MODEL), BF16),
                   jax.ShapeDtypeStruct((B, S, LANES), F32),
                   jax.ShapeDtypeStruct((8, B * S), F32),
                   jax.ShapeDtypeStruct((B * nt, N_EXPERTS, LANES), F32)],
        scratch_shapes=[pltpu.VMEM((tm, 2 * LANES), F32)],
        compiler_params=pltpu.CompilerParams(
            dimension_semantics=("arbitrary",), vmem_limit_bytes=VMEM_LIMIT),
        name="mix_xattn",
    )(x, attn, ga, gyb, memkv, wab, wmix, xg, wxq, wxo, fg, wr_hi, wr_lo, rb, tri, lower)


def _piece_copy(src_ref, dst_ref, sem):
    return pltpu.make_async_copy(src_ref, dst_ref, sem)


def _for_segments(seg_ref, chunk, fn):
    base = chunk * SEG_STRIDE

    def body(k, carry):
        at = base + 1 + 3 * k
        fn(seg_ref[at], seg_ref[at + 1], seg_ref[at + 2])
        return carry

    lax.fori_loop(0, seg_ref[base], body, 0)


def _dispatch_kernel(seg_ref, np_ref, gap_ref, fill_ref, xn_ref, infoT_ref, xs_hbm, buf_ref, zero_ref, sem_ref):
    c = pl.program_id(0)
    n = pl.num_programs(0)
    slot = c % 2

    tile_pieces = EXPERT_TILE // PIECE
    n_tiles = xs_hbm.shape[0] // tile_pieces

    def for_gaps(fn):
        def body(e, carry):
            cnt = gap_ref[2 * e + 1]

            @pl.when(cnt > 0)
            def _():
                fn(_piece_copy(zero_ref.at[pl.ds(0, cnt)], xs_hbm.at[pl.ds(gap_ref[2 * e], cnt)], sem_ref.at[2]))
            return carry

        lax.fori_loop(0, N_EXPERTS, body, 0)

    def tail_copy(t):
        return _piece_copy(zero_ref, xs_hbm.at[pl.ds(t * tile_pieces, tile_pieces)], sem_ref.at[2])

    @pl.when(c == 0)
    def _():
        zero_ref[...] = jnp.zeros_like(zero_ref)
        for_gaps(lambda cp: cp.start())
        lax.fori_loop(fill_ref[0], n_tiles, lambda t, carry: (tail_copy(t).start(), carry)[1], 0)

    def start_all(cc, s):
        _for_segments(seg_ref, cc, lambda lo, hi, cnt: _piece_copy(
            buf_ref.at[s, pl.ds(lo, cnt)], xs_hbm.at[pl.ds(hi, cnt)], sem_ref.at[s]).start())

    def wait_all(cc, s):
        n_pieces = np_ref[cc]

        @pl.when(n_pieces > 0)
        def _():
            _piece_copy(buf_ref.at[s, pl.ds(0, n_pieces)], xs_hbm.at[pl.ds(0, n_pieces)], sem_ref.at[s]).wait()

    @pl.when(c >= 2)
    def _():
        wait_all(c - 2, slot)

    pos1 = infoT_ref[0:1, :]
    pos2 = infoT_ref[1:2, :]
    r = lax.broadcasted_iota(jnp.int32, (CHUNK_ROWS, MOE_CHUNK), 0).astype(F32)
    onehot = jnp.where((r == pos1) | (r == pos2), 1.0, 0.0).astype(BF16)
    buf_ref[slot] = _dot(onehot, xn_ref[...]).astype(BF16).reshape(MAX_PIECES, PIECE, D_MODEL)
    start_all(c, slot)

    @pl.when(c == n - 1)
    def _():
        @pl.when(c >= 1)
        def _():
            wait_all(c - 1, 1 - slot)
        wait_all(c, slot)
        for_gaps(lambda cp: cp.wait())
        lax.fori_loop(fill_ref[0], n_tiles, lambda t, carry: (tail_copy(t).wait(), carry)[1], 0)


def _dispatch(xn, infoT, seg, npc, gaps, fill, rows_max):
    T = xn.shape[0]
    grid_spec = pltpu.PrefetchScalarGridSpec(
        num_scalar_prefetch=4,
        grid=(T // MOE_CHUNK,),
        in_specs=[
            pl.BlockSpec((MOE_CHUNK, D_MODEL), lambda c, *_: (c, 0)),
            pl.BlockSpec((8, MOE_CHUNK), lambda c, *_: (0, c)),
        ],
        out_specs=pl.BlockSpec(memory_space=pl.ANY),
        scratch_shapes=[pltpu.VMEM((2, MAX_PIECES, PIECE, D_MODEL), BF16),
                        pltpu.VMEM((EXPERT_TILE // PIECE, PIECE, D_MODEL), BF16),
                        pltpu.SemaphoreType.DMA((3,))],
    )
    xs = pl.pallas_call(
        _dispatch_kernel,
        grid_spec=grid_spec,
        out_shape=jax.ShapeDtypeStruct((rows_max // PIECE, PIECE, D_MODEL), BF16),
        compiler_params=pltpu.CompilerParams(
            dimension_semantics=("arbitrary",), vmem_limit_bytes=VMEM_LIMIT),
        name="moe_dispatch",
    )(seg, npc, gaps, fill, xn, infoT)
    return xs.reshape(rows_max, D_MODEL)


def _expert_kernel(te_ref, tv_ref, nu_ref, x_ref, wg_ref, wu_ref, wd_ref, y_ref, wgu_bf, wd_bf):
    i = pl.program_id(0)

    @pl.when(i < nu_ref[0])
    def _():
        @pl.when((i == 0) | (te_ref[i] != te_ref[jnp.maximum(i - 1, 0)]))
        def _():
            wgu_bf[:, :EXPERT_FF] = wg_ref[...].astype(BF16)
            wgu_bf[:, EXPERT_FF:] = wu_ref[...].astype(BF16)
            wd_bf[...] = wd_ref[...].astype(BF16)

        blocks = [slice(b * EXPERT_SUB, (b + 1) * EXPERT_SUB) for b in range(EXPERT_TILE // EXPERT_SUB)]
        row = lax.broadcasted_iota(jnp.int32, (EXPERT_SUB, D_MODEL), 0)
        gus = []
        for b, rows in enumerate(blocks):
            x = x_ref[rows, :]
            x = jnp.where(row < tv_ref[i] - b * EXPERT_SUB, x, jnp.zeros_like(x))
            gus.append(_dot(x, wgu_bf[...]))
        hids = []
        for gu in gus:
            gate = gu[:, :EXPERT_FF]
            hids.append((gate * jax.nn.sigmoid(gate) * gu[:, EXPERT_FF:]).astype(BF16))
        for rows, hid in zip(blocks, hids):
            y_ref[rows, :] = _dot(hid, wd_bf[...]).astype(BF16)

    @pl.when(i >= nu_ref[0])
    def _():
        y_ref[...] = jnp.zeros_like(y_ref)


def _experts(xs, w_gate, w_up, w_down, tile_expert, tile_valid, n_used):
    rows_max = xs.shape[0]
    last = lambda i, nu: jnp.minimum(i, nu[0] - 1)
    expert = lambda i, te, tv, nu: (te[last(i, nu)], 0, 0)
    grid_spec = pltpu.PrefetchScalarGridSpec(
        num_scalar_prefetch=3,
        grid=(rows_max // EXPERT_TILE,),
        in_specs=[
            pl.BlockSpec((EXPERT_TILE, D_MODEL), lambda i, te, tv, nu: (last(i, nu), 0)),
            pl.BlockSpec((None, D_MODEL, EXPERT_FF), expert),
            pl.BlockSpec((None, D_MODEL, EXPERT_FF), expert),
            pl.BlockSpec((None, EXPERT_FF, D_MODEL), expert),
        ],
        out_specs=pl.BlockSpec((EXPERT_TILE, D_MODEL), lambda i, te, tv, nu: (i, 0)),
        scratch_shapes=[pltpu.VMEM((D_MODEL, 2 * EXPERT_FF), BF16), pltpu.VMEM((EXPERT_FF, D_MODEL), BF16)],
    )
    return pl.pallas_call(
        _expert_kernel,
        grid_spec=grid_spec,
        out_shape=jax.ShapeDtypeStruct((rows_max, D_MODEL), BF16),
        compiler_params=pltpu.CompilerParams(
            dimension_semantics=("arbitrary",), vmem_limit_bytes=VMEM_LIMIT),
        name="moe_experts",
    )(tile_expert, tile_valid, n_used, xs, w_gate, w_up, w_down)


def _combine_kernel(seg_ref, np_ref, h_ref, info_ref, fg_ref, ys_hbm, o_ref, buf_ref, sem_ref):
    c = pl.program_id(0)
    n = pl.num_programs(0)
    slot = c % 2

    def start_all(cc, s):
        _for_segments(seg_ref, cc, lambda lo, hi, cnt: _piece_copy(
            ys_hbm.at[pl.ds(hi, cnt)], buf_ref.at[s, pl.ds(lo, cnt)], sem_ref.at[s]).start())

    def wait_all(cc, s):
        n_pieces = np_ref[cc]

        @pl.when(n_pieces > 0)
        def _():
            _piece_copy(ys_hbm.at[pl.ds(0, n_pieces)], buf_ref.at[s, pl.ds(0, n_pieces)], sem_ref.at[s]).wait()

    @pl.when(c == 0)
    def _():
        buf_ref[...] = jnp.zeros_like(buf_ref)
        start_all(0, 0)

    @pl.when(c + 1 < n)
    def _():
        start_all(c + 1, 1 - slot)

    wait_all(c, slot)
    info = info_ref[...]
    r = lax.broadcasted_iota(jnp.int32, (COMBINE_SUB, CHUNK_ROWS), 1).astype(F32)
    sorted_rows = buf_ref[slot].reshape(CHUNK_ROWS, D_MODEL)
    blocks = [slice(b * COMBINE_SUB, (b + 1) * COMBINE_SUB) for b in range(MOE_CHUNK // COMBINE_SUB)]
    moe = []
    for t in blocks:
        weights = (jnp.where(r == info[t, 0:1], info[t, 2:3], 0.0) + jnp.where(r == info[t, 1:2], info[t, 3:4], 0.0))
        moe.append(_dot(weights.astype(BF16), sorted_rows))
    for t, m in zip(blocks, moe):
        o_ref[t, :] = _rms(h_ref[t, :] + m, fg_ref[...])


def _combine(h2, info, fg, ys, seg, npc):
    T = h2.shape[0]
    grid_spec = pltpu.PrefetchScalarGridSpec(
        num_scalar_prefetch=2,
        grid=(T // MOE_CHUNK,),
        in_specs=[
            pl.BlockSpec((MOE_CHUNK, D_MODEL), lambda c, seg, npc: (c, 0)),
            pl.BlockSpec((MOE_CHUNK, LANES), lambda c, seg, npc: (c, 0)),
            pl.BlockSpec((1, D_MODEL), lambda c, seg, npc: (0, 0)),
            pl.BlockSpec(memory_space=pl.ANY),
        ],
        out_specs=pl.BlockSpec((MOE_CHUNK, D_MODEL), lambda c, seg, npc: (c, 0)),
        scratch_shapes=[pltpu.VMEM((2, MAX_PIECES, PIECE, D_MODEL), BF16), pltpu.SemaphoreType.DMA((2,))],
    )
    ys = ys.reshape(-1, PIECE, D_MODEL)
    return pl.pallas_call(
        _combine_kernel,
        grid_spec=grid_spec,
        out_shape=jax.ShapeDtypeStruct((T, D_MODEL), F32),
        compiler_params=pltpu.CompilerParams(
            dimension_semantics=("arbitrary",), vmem_limit_bytes=VMEM_LIMIT),
        name="moe_combine",
    )(seg, npc, h2, info, fg, ys)


def _routing_tables(pieces, rows_max):
    tile_pieces = EXPERT_TILE // PIECE
    total = jnp.sum(pieces, axis=0)
    total_al = (total + tile_pieces - 1) // tile_pieces * tile_pieces
    seg_end = jnp.cumsum(total_al)
    seg_start = seg_end - total_al
    chunk_off = jnp.cumsum(pieces, axis=0) - pieces
    loc_end = jnp.cumsum(pieces, axis=1)
    loc_start = loc_end - pieces
    n_local = loc_end[:, -1]
    experts = jnp.arange(N_EXPERTS, dtype=jnp.int32)
    nonempty = pieces > 0
    slot = jnp.cumsum(nonempty.astype(jnp.int32), axis=1) - 1
    pick = (nonempty[:, None, :] & (slot[:, None, :] == experts[None, :, None])).astype(jnp.int32)
    compact = lambda a: jnp.sum(pick * a[:, None, :], axis=-1)
    triples = jnp.stack([compact(loc_start), compact(seg_start[None, :] + chunk_off), compact(pieces)], axis=-1)
    n_seg = jnp.sum(nonempty.astype(jnp.int32), axis=1, keepdims=True)
    seg = jnp.concatenate([n_seg, triples.reshape(pieces.shape[0], 3 * N_EXPERTS)], axis=1)

    t0 = jnp.arange(rows_max // EXPERT_TILE, dtype=jnp.int32) * tile_pieces
    tile_expert = jnp.minimum(jnp.sum((t0[:, None] >= seg_end[None, :]).astype(jnp.int32), axis=-1), N_EXPERTS - 1)
    copies_end = jnp.sum((tile_expert[:, None] == experts).astype(jnp.int32) * (seg_start + total)[None, :], axis=-1)
    tile_valid = jnp.clip((copies_end - t0) * PIECE, 0, EXPERT_TILE)
    n_used = (seg_end[-1] // tile_pieces).reshape(1)

    gaps = jnp.stack([seg_start + total, total_al - total], axis=1).reshape(-1)
    fill = n_used
    i32 = lambda a: a.astype(jnp.int32)
    return i32(seg.reshape(-1)), i32(n_local), i32(tile_expert), i32(tile_valid), i32(n_used), i32(gaps), i32(fill)


def _rope_tables(positions):
    inv_freq = 1.0 / (ROPE_THETA ** (jnp.arange(0, ROPE_DIM, 2, dtype=F32) / ROPE_DIM))
    ang = positions.astype(F32)[:, None, :] * inv_freq[None, :, None]
    cos, sin = jnp.cos(ang), jnp.sin(ang)
    return jnp.concatenate([cos, sin], axis=1)


def _pad_heads(w, heads, width):
    k = w.shape[0]
    w = w.reshape(k, heads, width)
    w = jnp.pad(w, ((0, 0), (0, 0), (0, HEAD_PAD - width)))
    return w.reshape(k, heads * HEAD_PAD)


def _layer(l, h, mem, tables, mix_norm_g, w_in, q_norm_g, w_q_up, kv_norm_g, w_kv_up, w_attn_branch,
           pool_w, pool_scale, w_pool_branch, w_mix_out, xattn_norm_g, mem_norm_g, w_xq, w_xkv, w_xo,
           ffn_norm_g, w_router_group, b_router_group, w_router_expert, b_router_expert,
           w_exp_gate, w_exp_up, w_exp_down, out_g, tm_proj):
    B, S, _ = h.shape
    row2 = lambda v: v.reshape(1, -1).astype(F32)

    wi = w_in[l]
    kr_cols = jnp.pad(wi[:, Q_LORA + KV_LORA:Q_LORA + KV_LORA + ROPE_DIM],
                      ((0, 0), (NOPE_DIM, LANES - NOPE_DIM - ROPE_DIM)))
    win = jnp.concatenate([wi[:, :Q_LORA + KV_LORA], kr_cols, wi[:, Q_LORA + KV_LORA + ROPE_DIM:]], axis=1).astype(BF16)
    scale = math.log2(math.e) / math.sqrt(NOPE_DIM + ROPE_DIM)
    wq3 = (w_q_up[l] * scale).reshape(Q_LORA, MLA_HEADS, NOPE_DIM + ROPE_DIM)
    half = ROPE_DIM // 2
    wq = jnp.concatenate([wq3[:, :, :NOPE_DIM].reshape(Q_LORA, -1),
                          wq3[:, :, NOPE_DIM:NOPE_DIM + half].reshape(Q_LORA, -1),
                          wq3[:, :, NOPE_DIM + half:].reshape(Q_LORA, -1)], axis=1).astype(BF16)
    wkv3 = w_kv_up[l].reshape(KV_LORA, MLA_HEADS, NOPE_DIM + V_DIM)
    wkv = jnp.concatenate([
        _pad_heads(wkv3[:, :, :NOPE_DIM].reshape(KV_LORA, -1), MLA_HEADS, NOPE_DIM),
        wkv3[:, :, NOPE_DIM:].reshape(KV_LORA, -1)], axis=1).astype(BF16)

    memkv = _mem_kv(mem.reshape(-1, D_MODEL), row2(mem_norm_g[l]), w_xkv[l].astype(BF16))
    memkv = memkv.reshape(B, -1, 2 * D_MODEL)

    qT, k, vT, ga, gyb = _in_proj(
        h, tables, row2(mix_norm_g[l]), win, row2(q_norm_g[l]), wq, row2(kv_norm_g[l]), wkv,
        pool_w[l].astype(BF16), row2(pool_scale[l]), w_pool_branch[l].astype(BF16), tm_proj)
    attn = _mla_attention(qT, k, vT)

    w_r = jnp.zeros((D_MODEL, 2 * LANES), F32)
    w_r = w_r.at[:, :N_EXPERTS].set(w_router_expert[l]).at[:, LANES:LANES + N_GROUPS].set(w_router_group[l])
    wr_hi = w_r.astype(BF16)
    wr_lo = (w_r - wr_hi.astype(F32)).astype(BF16)
    rb = jnp.zeros((1, 2 * LANES), F32)
    rb = rb.at[0, :N_EXPERTS].set(b_router_expert[l]).at[0, LANES:LANES + N_GROUPS].set(b_router_group[l])

    h2, xn, info, infoT, pieces = _mix_xattn(
        h, attn, ga, gyb, memkv, w_attn_branch[l].astype(BF16), w_mix_out[l].astype(BF16),
        row2(xattn_norm_g[l]), (w_xq[l] * (1.0 / math.sqrt(MEM_HEAD_DIM))).astype(BF16), w_xo[l].astype(BF16),
        row2(ffn_norm_g[l]), wr_hi, wr_lo, rb)

    T = B * S
    n_chunks = T // MOE_CHUNK
    tile_pieces = EXPERT_TILE // PIECE
    max_pieces = 2 * T // PIECE + n_chunks * N_EXPERTS + N_EXPERTS * tile_pieces
    rows_max = -(-max_pieces // tile_pieces) * EXPERT_TILE
    seg, n_local, tile_expert, tile_valid, n_used, gaps, fill = _routing_tables(
        pieces[:, :, 0].astype(jnp.int32), rows_max)

    xs = _dispatch(xn.reshape(T, D_MODEL), infoT, seg, n_local, gaps, fill, rows_max)
    ys = _experts(xs, w_exp_gate[l], w_exp_up[l], w_exp_down[l], tile_expert, tile_valid, n_used)
    out = _combine(h2.reshape(T, D_MODEL), info.reshape(T, LANES), row2(out_g), ys, seg, n_local)
    return out.reshape(B, S, D_MODEL)


def kernel(x, mem, positions, mix_norm_g, w_in, q_norm_g, w_q_up, kv_norm_g, w_kv_up, w_attn_branch, pool_w, pool_scale, w_pool_branch, w_mix_out, xattn_norm_g, mem_norm_g, w_xq, w_xkv, w_xo, ffn_norm_g, w_router_group, b_router_group, w_router_expert, b_router_expert, w_exp_gate, w_exp_up, w_exp_down, final_norm_g):
    depth = w_in.shape[0]
    assert depth == 1, "the combine kernel fuses the final RMSNorm, which is only valid after the last layer"
    assert x.shape[1] % Q_TILE == 0 and x.shape[1] % MOE_CHUNK == 0
    tables = _rope_tables(positions)
    return _layer(0, x, mem, tables, mix_norm_g, w_in, q_norm_g, w_q_up, kv_norm_g, w_kv_up, w_attn_branch,
                  pool_w, pool_scale, w_pool_branch, w_mix_out, xattn_norm_g, mem_norm_g, w_xq, w_xkv, w_xo,
                  ffn_norm_g, w_router_group, b_router_group, w_router_expert, b_router_expert,
                  w_exp_gate, w_exp_up, w_exp_down, final_norm_g, IN_PROJ_TILE)
```

```python
import functools
import math

import jax
import jax.numpy as jnp
from jax import lax
from jax.experimental import pallas as pl
from jax.experimental.pallas import tpu as pltpu

F32 = jnp.float32
BF16 = jnp.bfloat16

D_MODEL = 1024
CHUNK = 64
MLA_HEADS = 8
Q_LORA = 384
KV_LORA = 256
NOPE_DIM = 64
ROPE_DIM = 32
V_DIM = 64
ROPE_THETA = 10000.0
POOL_WIDTH = 512
POOL_WINDOWS = (2, 4, 8, 16)
POOL_GROUP_DIM = POOL_WIDTH // len(POOL_WINDOWS)
POOL_HALO = 16
MEM_HEADS = 4
MEM_HEAD_DIM = D_MODEL // MEM_HEADS
N_GROUPS = 4
EXPERTS_PER_GROUP = 8
N_EXPERTS = N_GROUPS * EXPERTS_PER_GROUP
EXPERT_FF = 256
EPS = 1e-6

LANES = 128
HEAD_PAD = LANES
Q_TILE = 512
KV_TILE = 512
KEY_SUB = 128
SCORE_SUB = 128
V_ROWS = V_DIM + 16
IN_PROJ_TILE = 512
MOE_CHUNK = 512
PIECE = 16
MAX_PIECES = 2 * MOE_CHUNK // PIECE + N_EXPERTS
COMBINE_SUB = 256
SEG_STRIDE = 1 + 3 * N_EXPERTS
CHUNK_ROWS = MAX_PIECES * PIECE
EXPERT_TILE = 1024
EXPERT_SUB = 256
PROJ_SUB = 256
MIX_SUB = 256
QK_AHEAD = 3
PV_BEHIND = 2

_C_Q = 0
_C_KV = _C_Q + Q_LORA
_C_KR = _C_KV + KV_LORA
_C_POOL = _C_KR + LANES
_C_GA = _C_POOL + POOL_WIDTH
_C_GB = _C_GA + D_MODEL
_C_END = _C_GB + D_MODEL

VMEM_LIMIT = 56 * 1024 * 1024


def _rms(x, g):
    return x * lax.rsqrt(jnp.mean(x * x, axis=-1, keepdims=True) + EPS) * g


def _dot(a, b):
    return jnp.dot(a, b, preferred_element_type=F32)


def _dot_nt(a, b):
    return lax.dot_general(a, b, (((1,), (1,)), ((), ())), preferred_element_type=F32)


def _mem_kv_kernel(mem_ref, g_ref, w_ref, kv_ref):
    mn = _rms(mem_ref[...], g_ref[...]).astype(BF16)
    kv_ref[...] = _dot(mn, w_ref[...]).astype(BF16)


def _mem_kv(mem2d, g, w_xkv):
    rows = mem2d.shape[0]
    tm = min(512, rows)
    assert rows % tm == 0
    return pl.pallas_call(
        _mem_kv_kernel,
        grid=(rows // tm,),
        in_specs=[
            pl.BlockSpec((tm, D_MODEL), lambda i: (i, 0)),
            pl.BlockSpec((1, D_MODEL), lambda i: (0, 0)),
            pl.BlockSpec((D_MODEL, 2 * D_MODEL), lambda i: (0, 0)),
        ],
        out_specs=pl.BlockSpec((tm, 2 * D_MODEL), lambda i: (i, 0)),
        out_shape=jax.ShapeDtypeStruct((rows, 2 * D_MODEL), BF16),
        compiler_params=pltpu.CompilerParams(vmem_limit_bytes=VMEM_LIMIT),
        name="mem_kv",
    )(mem2d, g, w_xkv)


def _rope(t, c, sa, sb):
    w = t.shape[-1]
    return t * c + pltpu.roll(t, ROPE_DIM // 2, 1) * sa + pltpu.roll(t, w - ROPE_DIM // 2, 1) * sb


def _in_proj_kernel(x_ref, rot_ref, g_ref, win_ref, qg_ref, wq_ref, kvg_ref, wkv_ref,
                    poolw_ref, pscale_ref, wpb_ref,
                    qT_out, k_out, vT_out, ga_out, gyb_out, hist_ref):
    tm = x_ref.shape[0]
    i = pl.program_id(1)
    blocks = [slice(b * PROJ_SUB, (b + 1) * PROJ_SUB) for b in range(tm // PROJ_SUB)]
    rows = lambda parts: jnp.concatenate(parts, axis=0)
    hn = [_rms(x_ref[r, :], g_ref[...]).astype(BF16) for r in blocks]
    proj = lambda lo, hi: [_dot(v, win_ref[:, lo:hi]) for v in hn]

    half = ROPE_DIM // 2
    cos8 = jnp.concatenate([rot_ref[0:half, :]] * MLA_HEADS, axis=0).T
    sin8 = jnp.concatenate([rot_ref[half:, :]] * MLA_HEADS, axis=0).T
    lane = lax.broadcasted_iota(jnp.int32, cos8.shape, 1)
    rope_lo = NOPE_DIM
    c1 = jnp.where(lane < rope_lo + ROPE_DIM, cos8, 0.0)
    sa1 = jnp.where((lane >= rope_lo + half) & (lane < rope_lo + ROPE_DIM), sin8, 0.0)
    sb1 = jnp.where((lane >= rope_lo) & (lane < rope_lo + half), -sin8, 0.0)

    q_lat = proj(_C_Q, _C_KV)
    kv_lat = proj(_C_KV, _C_KR)
    qn = [_rms(v, qg_ref[...]).astype(BF16) for v in q_lat]
    kvn = [_rms(v, kvg_ref[...]).astype(BF16) for v in kv_lat]
    q = rows([_dot(v, wq_ref[...]) for v in qn])
    n_nope = MLA_HEADS * NOPE_DIM
    x1 = q[:, n_nope:n_nope + LANES]
    x2 = q[:, n_nope + LANES:]
    qT = jnp.concatenate([q[:, :n_nope], x1 * cos8 - x2 * sin8, x2 * cos8 + x1 * sin8], axis=1).T
    pad = jnp.zeros((HEAD_PAD - NOPE_DIM - ROPE_DIM, tm), F32)
    qT = jnp.concatenate(
        [blk for h in range(MLA_HEADS) for blk in (
            qT[h * NOPE_DIM:(h + 1) * NOPE_DIM, :],
            qT[n_nope + h * half:n_nope + (h + 1) * half, :],
            qT[n_nope + LANES + h * half:n_nope + LANES + (h + 1) * half, :], pad)], axis=0).astype(BF16)

    k_nope = rows([_dot(v, wkv_ref[:, 0:MLA_HEADS * HEAD_PAD]) for v in kvn])
    kr = _rope(rows(proj(_C_KR, _C_POOL)), c1, sa1, sb1)
    k_out[...] = (k_nope + jnp.tile(kr, (1, MLA_HEADS))).astype(BF16)
    v = rows([_dot(t, wkv_ref[:, MLA_HEADS * HEAD_PAD:]) for t in kvn])
    for t in range(tm // Q_TILE):
        qT_out[t] = qT[:, t * Q_TILE:(t + 1) * Q_TILE]
    vT = v.T
    ones = jnp.ones((V_ROWS - V_DIM, tm), F32)
    vT = jnp.concatenate(
        [blk for h in range(MLA_HEADS) for blk in (vT[h * V_DIM:(h + 1) * V_DIM, :], ones)], axis=0).astype(BF16)
    for t in range(tm // KV_TILE):
        vT_out[t] = vT[:, t * KV_TILE:(t + 1) * KV_TILE]

    u = rows(proj(_C_POOL, _C_GA))

    @pl.when(i == 0)
    def _():
        hist_ref[...] = jnp.zeros_like(hist_ref)

    ext = jnp.concatenate([hist_ref[...], u], axis=0)
    hist_ref[...] = u[tm - POOL_HALO:, :]
    t_idx = i * tm + lax.broadcasted_iota(jnp.int32, (tm, POOL_GROUP_DIM), 0)
    ys = []
    for g, w in enumerate(POOL_WINDOWS):
        c0 = g * POOL_GROUP_DIM
        run = ext[:, c0:c0 + POOL_GROUP_DIM]
        span = 1
        while span < w:
            run = run + pltpu.roll(run, span, 0)
            span *= 2
        cnt = jnp.minimum(t_idx + 1, w).astype(F32)
        d = run[POOL_HALO:, :] / cnt - u[:, c0:c0 + POOL_GROUP_DIM]
        ys.append(_dot(d.astype(BF16), poolw_ref[g]))
    y = (jnp.concatenate(ys, axis=1) * pscale_ref[...]).astype(BF16)
    y_b = [_dot(y[r, :], wpb_ref[...]) for r in blocks]

    for r, t in zip(blocks, proj(_C_GA, _C_GB)):
        ga_out[r, :] = jax.nn.sigmoid(t).astype(BF16)
    for r, t, yb in zip(blocks, proj(_C_GB, _C_END), y_b):
        gyb_out[r, :] = (jax.nn.sigmoid(t) * yb).astype(BF16)


def _in_proj(x, rot, g, win, qg, wq, kvg, wkv, poolw, pscale, wpb, tm):
    B, S, _ = x.shape
    row = lambda b, i: (b, i, 0)
    const2 = lambda b, i: (0, 0)
    const3 = lambda b, i: (0, 0, 0)
    slab = lambda b, i: (b, i, 0, 0)
    return pl.pallas_call(
        _in_proj_kernel,
        grid=(B, S // tm),
        in_specs=[
            pl.BlockSpec((None, tm, D_MODEL), row),
            pl.BlockSpec((None, ROPE_DIM, tm), lambda b, i: (b, 0, i)),
            pl.BlockSpec((1, D_MODEL), const2),
            pl.BlockSpec(win.shape, const2),
            pl.BlockSpec((1, Q_LORA), const2),
            pl.BlockSpec(wq.shape, const2),
            pl.BlockSpec((1, KV_LORA), const2),
            pl.BlockSpec(wkv.shape, const2),
            pl.BlockSpec(poolw.shape, const3),
            pl.BlockSpec((1, POOL_WIDTH), const2),
            pl.BlockSpec(wpb.shape, const2),
        ],
        out_specs=[
            pl.BlockSpec((None, tm // Q_TILE, MLA_HEADS * HEAD_PAD, Q_TILE), slab),
            pl.BlockSpec((None, tm, MLA_HEADS * HEAD_PAD), row),
            pl.BlockSpec((None, tm // KV_TILE, MLA_HEADS * V_ROWS, KV_TILE), slab),
            pl.BlockSpec((None, tm, D_MODEL), row),
            pl.BlockSpec((None, tm, D_MODEL), row),
        ],
        out_shape=[jax.ShapeDtypeStruct((B, S // Q_TILE, MLA_HEADS * HEAD_PAD, Q_TILE), BF16),
                   jax.ShapeDtypeStruct((B, S, MLA_HEADS * HEAD_PAD), BF16),
                   jax.ShapeDtypeStruct((B, S // KV_TILE, MLA_HEADS * V_ROWS, KV_TILE), BF16),
                   jax.ShapeDtypeStruct((B, S, D_MODEL), BF16),
                   jax.ShapeDtypeStruct((B, S, D_MODEL), BF16)],
        scratch_shapes=[pltpu.VMEM((POOL_HALO, POOL_WIDTH), F32)],
        compiler_params=pltpu.CompilerParams(
            dimension_semantics=("arbitrary", "arbitrary"), vmem_limit_bytes=VMEM_LIMIT),
        name="in_proj",
    )(x, rot, g, win, qg, wq, kvg, wkv, poolw, pscale, wpb)


def _mla_kernel(qT_ref, k_ref, vT_ref, o_ref, m_ref, acc_ref):
    i = pl.program_id(1)
    n_sub = KV_TILE // KEY_SUB
    units = [(h, c) for c in range(n_sub) for h in range(MLA_HEADS)]
    qry_c = lax.broadcasted_iota(jnp.int32, (SCORE_SUB, Q_TILE), 1) // CHUNK
    parts = range(KEY_SUB // SCORE_SUB)

    def scores(j, h, c):
        hs = slice(h * HEAD_PAD, (h + 1) * HEAD_PAD)
        out = []
        for a in parts:
            rows = pl.ds(pl.multiple_of(j * KV_TILE + c * KEY_SUB + a * SCORE_SUB, SCORE_SUB), SCORE_SUB)
            out.append(_dot(k_ref[rows, hs], qT_ref[hs, :]))
        return out

    def fold(h, alpha, pv):
        acc_ref[h] = pv if alpha is None else alpha * acc_ref[h] + pv

    def sweep(tiles, diag):
        work = [(j, h, c) for j in tiles for (h, c) in units]
        ahead = [scores(*u) for u in work[:QK_AHEAD]]
        pending = []
        for n, (j, h, c) in enumerate(work):
            s = ahead.pop(0)
            if n + QK_AHEAD < len(work):
                ahead.append(scores(*work[n + QK_AHEAD]))
            first = False
            if diag is not None:
                for a in parts:
                    key_c0 = (diag * KV_TILE + c * KEY_SUB + a * SCORE_SUB) // CHUNK
                    key_c = key_c0 + lax.broadcasted_iota(jnp.int32, (SCORE_SUB, Q_TILE), 0) // CHUNK
                    s[a] = jnp.where(key_c <= qry_c, s[a], -jnp.inf)
                first = diag == 0 and c == 0
            s_max = functools.reduce(jnp.maximum, [jnp.max(v, axis=0, keepdims=True) for v in s])
            m_new = s_max if first else jnp.maximum(m_ref[h], s_max)
            p = jnp.concatenate([jnp.exp2(v - m_new).astype(BF16) for v in s], axis=0)
            pv = _dot(vT_ref[j, h * V_ROWS:(h + 1) * V_ROWS, c * KEY_SUB:(c + 1) * KEY_SUB], p)
            alpha = None if first else jnp.exp2(m_ref[h] - m_new)
            m_ref[h] = m_new
            pending.append((h, alpha, pv))
            if len(pending) > PV_BEHIND:
                fold(*pending.pop(0))
        for item in pending:
            fold(*item)

    n_diag = Q_TILE // KV_TILE
    for d in range(n_diag):
        sweep([i * n_diag + d], d)

    n_full = i * n_diag

    def body(jj, carry):
        sweep([2 * jj, 2 * jj + 1], None)
        return carry

    lax.fori_loop(0, n_full // 2, body, 0)

    @pl.when(n_full % 2 == 1)
    def _():
        sweep([n_full - 1], None)

    oT = jnp.concatenate([acc_ref[h, :V_DIM, :] / acc_ref[h, V_DIM:V_DIM + 1, :] for h in range(MLA_HEADS)], axis=0)
    o_ref[...] = oT.T.astype(BF16)


def _mla_attention(qT, k, vT):
    B, S, W = k.shape
    return pl.pallas_call(
        _mla_kernel,
        grid=(B, S // Q_TILE),
        in_specs=[
            pl.BlockSpec((None, None, W, Q_TILE), lambda b, i: (b, i, 0, 0)),
            pl.BlockSpec((None, S, W), lambda b, i: (b, 0, 0)),
            pl.BlockSpec((None, S // KV_TILE, MLA_HEADS * V_ROWS, KV_TILE), lambda b, i: (b, 0, 0, 0)),
        ],
        out_specs=pl.BlockSpec((None, Q_TILE, MLA_HEADS * V_DIM), lambda b, i: (b, i, 0)),
        out_shape=jax.ShapeDtypeStruct((B, S, MLA_HEADS * V_DIM), BF16),
        scratch_shapes=[pltpu.VMEM((MLA_HEADS, 1, Q_TILE), F32),
                        pltpu.VMEM((MLA_HEADS, V_ROWS, Q_TILE), F32)],
        compiler_params=pltpu.CompilerParams(
            dimension_semantics=("arbitrary", "arbitrary"), vmem_limit_bytes=VMEM_LIMIT),
        name="mla_attn",
    )(qT, k, vT)


def _route(logits_t, bias_t, tri_upper, tri_lower):
    tm = logits_t.shape[1]
    neg = -jnp.inf
    rg = lax.broadcasted_iota(jnp.int32, (8, tm), 0)
    re = lax.broadcasted_iota(jnp.int32, (N_EXPERTS, tm), 0)
    top = lambda v: jnp.max(v, axis=0, keepdims=True)

    lg = jnp.where(rg < N_GROUPS, logits_t[LANES:LANES + 8, :] + bias_t[LANES:LANES + 8, :], neg)
    ge = jnp.exp(lg - top(lg))
    gp = ge / jnp.sum(ge, axis=0, keepdims=True)
    g_w = top(gp)
    g_idx = jnp.min(jnp.where(gp == g_w, rg, 8), axis=0, keepdims=True)

    sel = re // EXPERTS_PER_GROUP == g_idx
    le = jnp.where(sel, logits_t[:N_EXPERTS, :] + bias_t[:N_EXPERTS, :], neg)
    ee = jnp.exp(le - top(le))
    ep = jnp.where(sel, ee / jnp.sum(ee, axis=0, keepdims=True), -1.0)
    w1 = top(ep)
    i1 = jnp.min(jnp.where(ep == w1, re, N_EXPERTS), axis=0, keepdims=True)
    ep2 = jnp.where(re == i1, -1.0, ep)
    w2 = top(ep2)
    i2 = jnp.min(jnp.where(ep2 == w2, re, N_EXPERTS), axis=0, keepdims=True)
    den = w1 + w2
    c1 = g_w * (w1 / den)
    c2 = g_w * (w2 / den)

    oh1 = (re == i1).astype(F32)
    oh2 = (re == i2).astype(F32)
    both = oh1 + oh2
    earlier = _dot(both.astype(BF16), tri_upper)
    pieces = jnp.floor((jnp.sum(both, axis=1, keepdims=True) + (PIECE - 1)) * (1.0 / PIECE))
    start = _dot(tri_lower, jnp.broadcast_to(pieces, (N_EXPERTS, LANES)).astype(BF16))[:, 0:1] * PIECE
    pos1 = jnp.sum(oh1 * (earlier + start), axis=0, keepdims=True)
    pos2 = jnp.sum(oh2 * (earlier + start), axis=0, keepdims=True)
    info_t = jnp.concatenate([pos1, pos2, c1, c2, jnp.zeros((4, tm), F32)], axis=0)
    return info_t, pieces


def _mix_kernel(x_ref, attn_ref, ga_ref, gyb_ref, kv_ref, wab_ref, wmix_ref, xg_ref, wxq_ref, wxo_ref,
                fg_ref, wr_hi_ref, wr_lo_ref, rb_ref, tri_ref, lower_ref,
                h_out, xn_out, info_out, infoT_out, pieces_out, logits_ref):
    tm = x_ref.shape[0]

    @pl.when(pl.program_id(0) == 0)
    def _():
        logits_ref[...] = jnp.zeros_like(logits_ref)

    info_t, pieces = _route(logits_ref[...].T, rb_ref[...], tri_ref[...], lower_ref[...])
    infoT_out[...] = info_t
    info_out[...] = jnp.concatenate([info_t, jnp.zeros((LANES - 8, tm), F32)], axis=0).T
    pieces_out[...] = jnp.broadcast_to(pieces, (N_EXPERTS, LANES))

    blocks = [slice(b * MIX_SUB, (b + 1) * MIX_SUB) for b in range(tm // MIX_SUB)]
    y_a = [_dot(attn_ref[r, :], wab_ref[...]) for r in blocks]
    merged = [(ga_ref[r, :].astype(F32) * y + gyb_ref[r, :].astype(F32)).astype(BF16) for r, y in zip(blocks, y_a)]
    h1 = [x_ref[r, :] + _dot(m, wmix_ref[...]) for r, m in zip(blocks, merged)]

    hn = [_rms(h, xg_ref[...]).astype(BF16) for h in h1]
    q = [_dot(v, wxq_ref[...]).astype(BF16) for v in hn]
    pairs = [(b, h) for h in range(MEM_HEADS) for b in range(len(blocks))]
    hs = lambda h: slice(h * MEM_HEAD_DIM, (h + 1) * MEM_HEAD_DIM)
    vs = lambda h: slice(D_MODEL + h * MEM_HEAD_DIM, D_MODEL + (h + 1) * MEM_HEAD_DIM)
    s = [_dot_nt(q[b][:, hs(h)], kv_ref[:, hs(h)]) for b, h in pairs]
    p = [jnp.exp(v - jnp.max(v, axis=-1, keepdims=True)) for v in s]
    o = [_dot(v.astype(BF16), kv_ref[:, vs(h)]) for v, (b, h) in zip(p, pairs)]
    heads = [[None] * MEM_HEADS for _ in blocks]
    for (b, h), ov, pv in zip(pairs, o, p):
        heads[b][h] = (ov / jnp.sum(pv, axis=-1, keepdims=True)).astype(BF16)
    h2 = [h + _dot(jnp.concatenate(hd, axis=1), wxo_ref[...]) for h, hd in zip(h1, heads)]

    xn = [_rms(h, fg_ref[...]) for h in h2]
    xn_hi = [v.astype(BF16) for v in xn]
    xn_lo = [(v - hi.astype(F32)).astype(BF16) for v, hi in zip(xn, xn_hi)]
    logits = [_dot(hi, wr_hi_ref[...]) + (_dot(hi, wr_lo_ref[...]) + _dot(lo, wr_hi_ref[...]))
              for hi, lo in zip(xn_hi, xn_lo)]
    for r, h, hi in zip(blocks, h2, xn_hi):
        h_out[r, :] = h
        xn_out[r, :] = hi
    for r, v in zip(blocks, logits):
        logits_ref[r, :] = v


def _mix_xattn(x, attn, ga, gyb, memkv, wab, wmix, xg, wxq, wxo, fg, wr_hi, wr_lo, rb):
    B, S, _ = x.shape
    M = memkv.shape[1]
    tm = MOE_CHUNK
    nt = S // tm
    n_tiles = B * nt
    cur = lambda t: jnp.minimum(t, n_tiles - 1)
    prev = lambda t: jnp.maximum(t - 1, 0)
    row = lambda t: (cur(t) // nt, cur(t) % nt, 0)
    const2 = lambda t: (0, 0)
    tri = (lax.broadcasted_iota(jnp.int32, (tm, tm), 0) < lax.broadcasted_iota(jnp.int32, (tm, tm), 1)).astype(BF16)
    lower = (lax.broadcasted_iota(jnp.int32, (N_EXPERTS, N_EXPERTS), 1)
             < lax.broadcasted_iota(jnp.int32, (N_EXPERTS, N_EXPERTS), 0)).astype(BF16)
    rb = jnp.broadcast_to(rb.reshape(2 * LANES, 1), (2 * LANES, tm))
    return pl.pallas_call(
        _mix_kernel,
        grid=(n_tiles + 1,),
        in_specs=[
            pl.BlockSpec((None, tm, D_MODEL), row),
            pl.BlockSpec((None, tm, MLA_HEADS * V_DIM), row),
            pl.BlockSpec((None, tm, D_MODEL), row),
            pl.BlockSpec((None, tm, D_MODEL), row),
            pl.BlockSpec((None, M, 2 * D_MODEL), lambda t: (cur(t) // nt, 0, 0)),
            pl.BlockSpec(wab.shape, const2),
            pl.BlockSpec(wmix.shape, const2),
            pl.BlockSpec((1, D_MODEL), const2),
            pl.BlockSpec(wxq.shape, const2),
            pl.BlockSpec(wxo.shape, const2),
            pl.BlockSpec((1, D_MODEL), const2),
            pl.BlockSpec(wr_hi.shape, const2),
            pl.BlockSpec(wr_lo.shape, const2),
            pl.BlockSpec((2 * LANES, tm), const2),
            pl.BlockSpec((tm, tm), const2),
            pl.BlockSpec((N_EXPERTS, N_EXPERTS), const2),
        ],
        out_specs=[
            pl.BlockSpec((None, tm, D_MODEL), row),
            pl.BlockSpec((None, tm, D_MODEL), row),
            pl.BlockSpec((None, tm, LANES), lambda t: (prev(t) // nt, prev(t) % nt, 0)),
            pl.BlockSpec((8, tm), lambda t: (0, prev(t))),
            pl.BlockSpec((None, N_EXPERTS, LANES), lambda t: (prev(t), 0, 0)),
        ],
        out_shape=[jax.ShapeDtypeStruct((B, S, D_MODEL), F32),
                   jax.ShapeDtypeStruct((B, S, D_MODEL), BF16),
                   jax.ShapeDtypeStruct((B, S, LANES), F32),
                   jax.ShapeDtypeStruct((8, B * S), F32),
                   jax.ShapeDtypeStruct((B * nt, N_EXPERTS, LANES), F32)],
        scratch_shapes=[pltpu.VMEM((tm, 2 * LANES), F32)],
        compiler_params=pltpu.CompilerParams(
            dimension_semantics=("arbitrary",), vmem_limit_bytes=VMEM_LIMIT),
        name="mix_xattn",
    )(x, attn, ga, gyb, memkv, wab, wmix, xg, wxq, wxo, fg, wr_hi, wr_lo, rb, tri, lower)


def _piece_copy(src_ref, dst_ref, sem):
    return pltpu.make_async_copy(src_ref, dst_ref, sem)


def _for_segments(seg_ref, chunk, fn):
    base = chunk * SEG_STRIDE

    def body(k, carry):
        at = base + 1 + 3 * k
        fn(seg_ref[at], seg_ref[at + 1], seg_ref[at + 2])
        return carry

    lax.fori_loop(0, seg_ref[base], body, 0)


def _dispatch_kernel(seg_ref, np_ref, gap_ref, fill_ref, xn_ref, infoT_ref, xs_hbm, buf_ref, zero_ref, sem_ref):
    c = pl.program_id(0)
    n = pl.num_programs(0)
    slot = c % 2

    tile_pieces = EXPERT_TILE // PIECE
    n_tiles = xs_hbm.shape[0] // tile_pieces

    def for_gaps(fn):
        def body(e, carry):
            cnt = gap_ref[2 * e + 1]

            @pl.when(cnt > 0)
            def _():
                fn(_piece_copy(zero_ref.at[pl.ds(0, cnt)], xs_hbm.at[pl.ds(gap_ref[2 * e], cnt)], sem_ref.at[2]))
            return carry

        lax.fori_loop(0, N_EXPERTS, body, 0)

    def tail_copy(t):
        return _piece_copy(zero_ref, xs_hbm.at[pl.ds(t * tile_pieces, tile_pieces)], sem_ref.at[2])

    @pl.when(c == 0)
    def _():
        zero_ref[...] = jnp.zeros_like(zero_ref)
        for_gaps(lambda cp: cp.start())
        lax.fori_loop(fill_ref[0], n_tiles, lambda t, carry: (tail_copy(t).start(), carry)[1], 0)

    def start_all(cc, s):
        _for_segments(seg_ref, cc, lambda lo, hi, cnt: _piece_copy(
            buf_ref.at[s, pl.ds(lo, cnt)], xs_hbm.at[pl.ds(hi, cnt)], sem_ref.at[s]).start())

    def wait_all(cc, s):
        n_pieces = np_ref[cc]

        @pl.when(n_pieces > 0)
        def _():
            _piece_copy(buf_ref.at[s, pl.ds(0, n_pieces)], xs_hbm.at[pl.ds(0, n_pieces)], sem_ref.at[s]).wait()

    @pl.when(c >= 2)
    def _():
        wait_all(c - 2, slot)

    pos1 = infoT_ref[0:1, :]
    pos2 = infoT_ref[1:2, :]
    r = lax.broadcasted_iota(jnp.int32, (CHUNK_ROWS, MOE_CHUNK), 0).astype(F32)
    onehot = jnp.where((r == pos1) | (r == pos2), 1.0, 0.0).astype(BF16)
    buf_ref[slot] = _dot(onehot, xn_ref[...]).astype(BF16).reshape(MAX_PIECES, PIECE, D_MODEL)
    start_all(c, slot)

    @pl.when(c == n - 1)
    def _():
        @pl.when(c >= 1)
        def _():
            wait_all(c - 1, 1 - slot)
        wait_all(c, slot)
        for_gaps(lambda cp: cp.wait())
        lax.fori_loop(fill_ref[0], n_tiles, lambda t, carry: (tail_copy(t).wait(), carry)[1], 0)


def _dispatch(xn, infoT, seg, npc, gaps, fill, rows_max):
    T = xn.shape[0]
    grid_spec = pltpu.PrefetchScalarGridSpec(
        num_scalar_prefetch=4,
        grid=(T // MOE_CHUNK,),
        in_specs=[
            pl.BlockSpec((MOE_CHUNK, D_MODEL), lambda c, *_: (c, 0)),
            pl.BlockSpec((8, MOE_CHUNK), lambda c, *_: (0, c)),
        ],
        out_specs=pl.BlockSpec(memory_space=pl.ANY),
        scratch_shapes=[pltpu.VMEM((2, MAX_PIECES, PIECE, D_MODEL), BF16),
                        pltpu.VMEM((EXPERT_TILE // PIECE, PIECE, D_MODEL), BF16),
                        pltpu.SemaphoreType.DMA((3,))],
    )
    xs = pl.pallas_call(
        _dispatch_kernel,
        grid_spec=grid_spec,
        out_shape=jax.ShapeDtypeStruct((rows_max // PIECE, PIECE, D_MODEL), BF16),
        compiler_params=pltpu.CompilerParams(
            dimension_semantics=("arbitrary",), vmem_limit_bytes=VMEM_LIMIT),
        name="moe_dispatch",
    )(seg, npc, gaps, fill, xn, infoT)
    return xs.reshape(rows_max, D_MODEL)


def _expert_kernel(te_ref, tv_ref, nu_ref, x_ref, wg_ref, wu_ref, wd_ref, y_ref, wgu_bf, wd_bf):
    i = pl.program_id(0)

    @pl.when(i < nu_ref[0])
    def _():
        @pl.when((i == 0) | (te_ref[i] != te_ref[jnp.maximum(i - 1, 0)]))
        def _():
            wgu_bf[:, :EXPERT_FF] = wg_ref[...].astype(BF16)
            wgu_bf[:, EXPERT_FF:] = wu_ref[...].astype(BF16)
            wd_bf[...] = wd_ref[...].astype(BF16)

        blocks = [slice(b * EXPERT_SUB, (b + 1) * EXPERT_SUB) for b in range(EXPERT_TILE // EXPERT_SUB)]
        row = lax.broadcasted_iota(jnp.int32, (EXPERT_SUB, D_MODEL), 0)
        gus = []
        for b, rows in enumerate(blocks):
            x = x_ref[rows, :]
            x = jnp.where(row < tv_ref[i] - b * EXPERT_SUB, x, jnp.zeros_like(x))
            gus.append(_dot(x, wgu_bf[...]))
        hids = []
        for gu in gus:
            gate = gu[:, :EXPERT_FF]
            hids.append((gate * jax.nn.sigmoid(gate) * gu[:, EXPERT_FF:]).astype(BF16))
        for rows, hid in zip(blocks, hids):
            y_ref[rows, :] = _dot(hid, wd_bf[...]).astype(BF16)

    @pl.when(i >= nu_ref[0])
    def _():
        y_ref[...] = jnp.zeros_like(y_ref)


def _experts(xs, w_gate, w_up, w_down, tile_expert, tile_valid, n_used):
    rows_max = xs.shape[0]
    last = lambda i, nu: jnp.minimum(i, nu[0] - 1)
    expert = lambda i, te, tv, nu: (te[last(i, nu)], 0, 0)
    grid_spec = pltpu.PrefetchScalarGridSpec(
        num_scalar_prefetch=3,
        grid=(rows_max // EXPERT_TILE,),
        in_specs=[
            pl.BlockSpec((EXPERT_TILE, D_MODEL), lambda i, te, tv, nu: (last(i, nu), 0)),
            pl.BlockSpec((None, D_MODEL, EXPERT_FF), expert),
            pl.BlockSpec((None, D_MODEL, EXPERT_FF), expert),
            pl.BlockSpec((None, EXPERT_FF, D_MODEL), expert),
        ],
        out_specs=pl.BlockSpec((EXPERT_TILE, D_MODEL), lambda i, te, tv, nu: (i, 0)),
        scratch_shapes=[pltpu.VMEM((D_MODEL, 2 * EXPERT_FF), BF16), pltpu.VMEM((EXPERT_FF, D_MODEL), BF16)],
    )
    return pl.pallas_call(
        _expert_kernel,
        grid_spec=grid_spec,
        out_shape=jax.ShapeDtypeStruct((rows_max, D_MODEL), BF16),
        compiler_params=pltpu.CompilerParams(
            dimension_semantics=("arbitrary",), vmem_limit_bytes=VMEM_LIMIT),
        name="moe_experts",
    )(tile_expert, tile_valid, n_used, xs, w_gate, w_up, w_down)


def _combine_kernel(seg_ref, np_ref, h_ref, info_ref, fg_ref, ys_hbm, o_ref, buf_ref, sem_ref):
    c = pl.program_id(0)
    n = pl.num_programs(0)
    slot = c % 2

    def start_all(cc, s):
        _for_segments(seg_ref, cc, lambda lo, hi, cnt: _piece_copy(
            ys_hbm.at[pl.ds(hi, cnt)], buf_ref.at[s, pl.ds(lo, cnt)], sem_ref.at[s]).start())

    def wait_all(cc, s):
        n_pieces = np_ref[cc]

        @pl.when(n_pieces > 0)
        def _():
            _piece_copy(ys_hbm.at[pl.ds(0, n_pieces)], buf_ref.at[s, pl.ds(0, n_pieces)], sem_ref.at[s]).wait()

    @pl.when(c == 0)
    def _():
        buf_ref[...] = jnp.zeros_like(buf_ref)
        start_all(0, 0)

    @pl.when(c + 1 < n)
    def _():
        start_all(c + 1, 1 - slot)

    wait_all(c, slot)
    info = info_ref[...]
    r = lax.broadcasted_iota(jnp.int32, (COMBINE_SUB, CHUNK_ROWS), 1).astype(F32)
    sorted_rows = buf_ref[slot].reshape(CHUNK_ROWS, D_MODEL)
    blocks = [slice(b * COMBINE_SUB, (b + 1) * COMBINE_SUB) for b in range(MOE_CHUNK // COMBINE_SUB)]
    moe = []
    for t in blocks:
        weights = (jnp.where(r == info[t, 0:1], info[t, 2:3], 0.0) + jnp.where(r == info[t, 1:2], info[t, 3:4], 0.0))
        moe.append(_dot(weights.astype(BF16), sorted_rows))
    for t, m in zip(blocks, moe):
        o_ref[t, :] = _rms(h_ref[t, :] + m, fg_ref[...])


def _combine(h2, info, fg, ys, seg, npc):
    T = h2.shape[0]
    grid_spec = pltpu.PrefetchScalarGridSpec(
        num_scalar_prefetch=2,
        grid=(T // MOE_CHUNK,),
        in_specs=[
            pl.BlockSpec((MOE_CHUNK, D_MODEL), lambda c, seg, npc: (c, 0)),
            pl.BlockSpec((MOE_CHUNK, LANES), lambda c, seg, npc: (c, 0)),
            pl.BlockSpec((1, D_MODEL), lambda c, seg, npc: (0, 0)),
            pl.BlockSpec(memory_space=pl.ANY),
        ],
        out_specs=pl.BlockSpec((MOE_CHUNK, D_MODEL), lambda c, seg, npc: (c, 0)),
        scratch_shapes=[pltpu.VMEM((2, MAX_PIECES, PIECE, D_MODEL), BF16), pltpu.SemaphoreType.DMA((2,))],
    )
    ys = ys.reshape(-1, PIECE, D_MODEL)
    return pl.pallas_call(
        _combine_kernel,
        grid_spec=grid_spec,
        out_shape=jax.ShapeDtypeStruct((T, D_MODEL), F32),
        compiler_params=pltpu.CompilerParams(
            dimension_semantics=("arbitrary",), vmem_limit_bytes=VMEM_LIMIT),
        name="moe_combine",
    )(seg, npc, h2, info, fg, ys)


def _routing_tables(pieces, rows_max):
    tile_pieces = EXPERT_TILE // PIECE
    total = jnp.sum(pieces, axis=0)
    total_al = (total + tile_pieces - 1) // tile_pieces * tile_pieces
    seg_end = jnp.cumsum(total_al)
    seg_start = seg_end - total_al
    chunk_off = jnp.cumsum(pieces, axis=0) - pieces
    loc_end = jnp.cumsum(pieces, axis=1)
    loc_start = loc_end - pieces
    n_local = loc_end[:, -1]
    experts = jnp.arange(N_EXPERTS, dtype=jnp.int32)
    nonempty = pieces > 0
    slot = jnp.cumsum(nonempty.astype(jnp.int32), axis=1) - 1
    pick = (nonempty[:, None, :] & (slot[:, None, :] == experts[None, :, None])).astype(jnp.int32)
    compact = lambda a: jnp.sum(pick * a[:, None, :], axis=-1)
    triples = jnp.stack([compact(loc_start), compact(seg_start[None, :] + chunk_off), compact(pieces)], axis=-1)
    n_seg = jnp.sum(nonempty.astype(jnp.int32), axis=1, keepdims=True)
    seg = jnp.concatenate([n_seg, triples.reshape(pieces.shape[0], 3 * N_EXPERTS)], axis=1)

    t0 = jnp.arange(rows_max // EXPERT_TILE, dtype=jnp.int32) * tile_pieces
    tile_expert = jnp.minimum(jnp.sum((t0[:, None] >= seg_end[None, :]).astype(jnp.int32), axis=-1), N_EXPERTS - 1)
    copies_end = jnp.sum((tile_expert[:, None] == experts).astype(jnp.int32) * (seg_start + total)[None, :], axis=-1)
    tile_valid = jnp.clip((copies_end - t0) * PIECE, 0, EXPERT_TILE)
    n_used = (seg_end[-1] // tile_pieces).reshape(1)

    gaps = jnp.stack([seg_start + total, total_al - total], axis=1).reshape(-1)
    fill = n_used
    i32 = lambda a: a.astype(jnp.int32)
    return i32(seg.reshape(-1)), i32(n_local), i32(tile_expert), i32(tile_valid), i32(n_used), i32(gaps), i32(fill)


def _rope_tables(positions):
    inv_freq = 1.0 / (ROPE_THETA ** (jnp.arange(0, ROPE_DIM, 2, dtype=F32) / ROPE_DIM))
    ang = positions.astype(F32)[:, None, :] * inv_freq[None, :, None]
    cos, sin = jnp.cos(ang), jnp.sin(ang)
    return jnp.concatenate([cos, sin], axis=1)


def _pad_heads(w, heads, width):
    k = w.shape[0]
    w = w.reshape(k, heads, width)
    w = jnp.pad(w, ((0, 0), (0, 0), (0, HEAD_PAD - width)))
    return w.reshape(k, heads * HEAD_PAD)


def _layer(l, h, mem, tables, mix_norm_g, w_in, q_norm_g, w_q_up, kv_norm_g, w_kv_up, w_attn_branch,
           pool_w, pool_scale, w_pool_branch, w_mix_out, xattn_norm_g, mem_norm_g, w_xq, w_xkv, w_xo,
           ffn_norm_g, w_router_group, b_router_group, w_router_expert, b_router_expert,
           w_exp_gate, w_exp_up, w_exp_down, out_g, tm_proj):
    B, S, _ = h.shape
    row2 = lambda v: v.reshape(1, -1).astype(F32)

    wi = w_in[l]
    kr_cols = jnp.pad(wi[:, Q_LORA + KV_LORA:Q_LORA + KV_LORA + ROPE_DIM],
                      ((0, 0), (NOPE_DIM, LANES - NOPE_DIM - ROPE_DIM)))
    win = jnp.concatenate([wi[:, :Q_LORA + KV_LORA], kr_cols, wi[:, Q_LORA + KV_LORA + ROPE_DIM:]], axis=1).astype(BF16)
    scale = math.log2(math.e) / math.sqrt(NOPE_DIM + ROPE_DIM)
    wq3 = (w_q_up[l] * scale).reshape(Q_LORA, MLA_HEADS, NOPE_DIM + ROPE_DIM)
    half = ROPE_DIM // 2
    wq = jnp.concatenate([wq3[:, :, :NOPE_DIM].reshape(Q_LORA, -1),
                          wq3[:, :, NOPE_DIM:NOPE_DIM + half].reshape(Q_LORA, -1),
                          wq3[:, :, NOPE_DIM + half:].reshape(Q_LORA, -1)], axis=1).astype(BF16)
    wkv3 = w_kv_up[l].reshape(KV_LORA, MLA_HEADS, NOPE_DIM + V_DIM)
    wkv = jnp.concatenate([
        _pad_heads(wkv3[:, :, :NOPE_DIM].reshape(KV_LORA, -1), MLA_HEADS, NOPE_DIM),
        wkv3[:, :, NOPE_DIM:].reshape(KV_LORA, -1)], axis=1).astype(BF16)

    memkv = _mem_kv(mem.reshape(-1, D_MODEL), row2(mem_norm_g[l]), w_xkv[l].astype(BF16))
    memkv = memkv.reshape(B, -1, 2 * D_MODEL)

    qT, k, vT, ga, gyb = _in_proj(
        h, tables, row2(mix_norm_g[l]), win, row2(q_norm_g[l]), wq, row2(kv_norm_g[l]), wkv,
        pool_w[l].astype(BF16), row2(pool_scale[l]), w_pool_branch[l].astype(BF16), tm_proj)
    attn = _mla_attention(qT, k, vT)

    w_r = jnp.zeros((D_MODEL, 2 * LANES), F32)
    w_r = w_r.at[:, :N_EXPERTS].set(w_router_expert[l]).at[:, LANES:LANES + N_GROUPS].set(w_router_group[l])
    wr_hi = w_r.astype(BF16)
    wr_lo = (w_r - wr_hi.astype(F32)).astype(BF16)
    rb = jnp.zeros((1, 2 * LANES), F32)
    rb = rb.at[0, :N_EXPERTS].set(b_router_expert[l]).at[0, LANES:LANES + N_GROUPS].set(b_router_group[l])

    h2, xn, info, infoT, pieces = _mix_xattn(
        h, attn, ga, gyb, memkv, w_attn_branch[l].astype(BF16), w_mix_out[l].astype(BF16),
        row2(xattn_norm_g[l]), (w_xq[l] * (1.0 / math.sqrt(MEM_HEAD_DIM))).astype(BF16), w_xo[l].astype(BF16),
        row2(ffn_norm_g[l]), wr_hi, wr_lo, rb)

    T = B * S
    n_chunks = T // MOE_CHUNK
    tile_pieces = EXPERT_TILE // PIECE
    max_pieces = 2 * T // PIECE + n_chunks * N_EXPERTS + N_EXPERTS * tile_pieces
    rows_max = -(-max_pieces // tile_pieces) * EXPERT_TILE
    seg, n_local, tile_expert, tile_valid, n_used, gaps, fill = _routing_tables(
        pieces[:, :, 0].astype(jnp.int32), rows_max)

    xs = _dispatch(xn.reshape(T, D_MODEL), infoT, seg, n_local, gaps, fill, rows_max)
    ys = _experts(xs, w_exp_gate[l], w_exp_up[l], w_exp_down[l], tile_expert, tile_valid, n_used)
    out = _combine(h2.reshape(T, D_MODEL), info.reshape(T, LANES), row2(out_g), ys, seg, n_local)
    return out.reshape(B, S, D_MODEL)


def kernel(x, mem, positions, mix_norm_g, w_in, q_norm_g, w_q_up, kv_norm_g, w_kv_up, w_attn_branch, pool_w, pool_scale, w_pool_branch, w_mix_out, xattn_norm_g, mem_norm_g, w_xq, w_xkv, w_xo, ffn_norm_g, w_router_group, b_router_group, w_router_expert, b_router_expert, w_exp_gate, w_exp_up, w_exp_down, final_norm_g):
    depth = w_in.shape[0]
    assert depth == 1, "the combine kernel fuses the final RMSNorm, which is only valid after the last layer"
    assert x.shape[1] % Q_TILE == 0 and x.shape[1] % MOE_CHUNK == 0
    tables = _rope_tables(positions)
    return _layer(0, x, mem, tables, mix_norm_g, w_in, q_norm_g, w_q_up, kv_norm_g, w_kv_up, w_attn_branch,
                  pool_w, pool_scale, w_pool_branch, w_mix_out, xattn_norm_g, mem_norm_g, w_xq, w_xkv, w_xo,
                  ffn_norm_g, w_router_group, b_router_group, w_router_expert, b_router_expert,
                  w_exp_gate, w_exp_up, w_exp_down, final_norm_g, IN_PROJ_TILE)
```
